```python
import math
import jax, jax.numpy as jnp
from jax import lax
import numpy as np


D_MODEL = 2048
BATCH = 32
SEQ = 256
DEPTH = 1
DEC_BATCH = 8
DEC_SEQ = 1024
PAST_LEN = 512

GRID_W = 64
NORM_EPS = 1e-6
RET_HEADS = 8
RET_DK = 128
RET_DV = 256
RET_CHUNK = 128
RET_QK_W = RET_HEADS * RET_DK
RET_V_W = RET_HEADS * RET_DV
MLA_HEADS = 16
MLA_Q_RANK = 512
MLA_KV_RANK = 512
MLA_NOPE = 128
MLA_ROPE = 64
MLA_V = 128
ROPE_BASE = 10000.0
ATTN_Q_BLOCK = 128
N_EXPERTS = 256
TOP_K = 8
N_GROUPS = 8
TOPK_GROUPS = 4
D_EXPERT = 512
D_SHARED = 512
ROUTED_SCALE = 2.5
EXPERT_BLOCK = 64
IN_WIDTHS = (RET_QK_W, RET_QK_W, RET_V_W, RET_V_W, MLA_Q_RANK, MLA_KV_RANK, MLA_ROPE, D_MODEL, D_MODEL)
IN_SPLITS = tuple(sum(IN_WIDTHS[:i + 1]) for i in range(len(IN_WIDTHS) - 1))
IN_WIDTH = sum(IN_WIDTHS)

kernel_name = 'hybrid_retention_mla_moe_diffusion_step'


def rms_norm(x, w):
    x32 = x.astype(jnp.float32)
    y = x32 * lax.rsqrt(jnp.mean(x32 * x32, axis=-1, keepdims=True) + NORM_EPS)
    return y.astype(x.dtype) * w.astype(x.dtype)


def rotate_half(x):
    x1, x2 = jnp.split(x, 2, axis=-1)
    return jnp.concatenate([-x2, x1], axis=-1)


def axial_rope_tables(n_tokens, dtype):
    rows = n_tokens // GRID_W
    row = jnp.broadcast_to(jnp.arange(rows, dtype=jnp.float32)[:, None], (rows, GRID_W)).reshape(-1)
    col = jnp.broadcast_to(jnp.arange(GRID_W, dtype=jnp.float32)[None, :], (rows, GRID_W)).reshape(-1)
    axis_dim = MLA_ROPE // 2
    freqs = jnp.power(ROPE_BASE, -jnp.arange(0, axis_dim, 2, dtype=jnp.float32) / axis_dim)
    ang_r = row[:, None] * freqs
    ang_c = col[:, None] * freqs
    ang = jnp.concatenate([ang_r, ang_r, ang_c, ang_c], axis=-1)
    return jnp.cos(ang).astype(dtype), jnp.sin(ang).astype(dtype)


def apply_axial_rope(x, cos, sin):
    xr, xc = jnp.split(x, 2, axis=-1)
    return x * cos + jnp.concatenate([rotate_half(xr), rotate_half(xc)], axis=-1) * sin


def retention_direction(q, k, v, log_gamma, s0):
    b, n_h, t, _ = q.shape
    dv = v.shape[-1]
    n_chunks = t // RET_CHUNK
    idx = jnp.arange(RET_CHUNK, dtype=jnp.float32)
    diff = idx[:, None] - idx[None, :]
    intra = jnp.where(diff >= 0, jnp.exp(jnp.maximum(diff, 0.0) * log_gamma[:, None, None]), 0.0)
    q_dec = jnp.exp((idx + 1.0) * log_gamma[:, None])[..., None]
    k_dec = jnp.exp((RET_CHUNK - 1.0 - idx) * log_gamma[:, None])[..., None]
    c_dec = jnp.exp(RET_CHUNK * log_gamma)[:, None, None]

    def chunks(a):
        return a.reshape(b, n_h, n_chunks, RET_CHUNK, a.shape[-1]).transpose(2, 0, 1, 3, 4)

    def step(s, inp):
        qc, kc, vc = inp
        scores = jnp.einsum('bhid,bhjd->bhij', qc, kc) * intra
        o = jnp.einsum('bhij,bhje->bhie', scores, vc) + jnp.einsum('bhid,bhde->bhie', qc * q_dec, s)
        s = c_dec * s + jnp.einsum('bhjd,bhje->bhde', kc * k_dec, vc)
        return s, o

    s_fin, o = lax.scan(step, s0, (chunks(q), chunks(k), chunks(v)))
    o = o.transpose(1, 2, 0, 3, 4).reshape(b, n_h, t, dv)
    return o, s_fin


def mla_attention(q_nope, q_pe, k_nope, k_pe, v):
    b, tq, n_h, dn = q_nope.shape
    dr = q_pe.shape[-1]
    n_blocks = tq // ATTN_Q_BLOCK
    scale = (dn + dr) ** -0.5
    qn = q_nope.reshape(b, n_blocks, ATTN_Q_BLOCK, n_h, dn).swapaxes(0, 1)
    qp = q_pe.reshape(b, n_blocks, ATTN_Q_BLOCK, n_h, dr).swapaxes(0, 1)

    def block(args):
        qn_b, qp_b = args
        s = jnp.einsum('bqhd,bkhd->bhqk', qn_b, k_nope) + jnp.einsum('bqhd,bkd->bhqk', qp_b, k_pe)
        p = jax.nn.softmax(s.astype(jnp.float32) * scale, axis=-1).astype(v.dtype)
        return jnp.einsum('bhqk,bkhd->bqhd', p, v)

    out = lax.map(block, (qn, qp))
    return out.swapaxes(0, 1).reshape(b, tq, n_h, v.shape[-1])


def token_mixer(h, lp, ret_s0_fwd, ret_s0_bwd, ctx_ckv, ctx_kpe):
    b, t, _ = h.shape
    proj = h @ lp['w_in']
    rq, rk, rv, rg, q_a, kv_a, k_pe, gate_ret, gate_mla = jnp.split(proj, IN_SPLITS, axis=-1)

    def heads(a, d):
        return a.reshape(b, t, RET_HEADS, d).transpose(0, 2, 1, 3).astype(jnp.float32)
    q = heads(rq, RET_DK)
    k = heads(rk, RET_DK) * (RET_DK ** -0.5)
    v = heads(rv, RET_DV)
    log_gamma = jax.nn.log_sigmoid(lp['ret_decay_logit'].astype(jnp.float32))
    o_f, s_f = retention_direction(q, k, v, log_gamma[0], ret_s0_fwd.astype(jnp.float32))
    o_b, s_b = retention_direction(q[:, :, ::-1], k[:, :, ::-1], v[:, :, ::-1], log_gamma[1],
                                   ret_s0_bwd.astype(jnp.float32))
    o = o_f + o_b[:, :, ::-1]
    mu = jnp.mean(o, axis=-1, keepdims=True)
    var = jnp.mean(jnp.square(o - mu), axis=-1, keepdims=True)
    o = (o - mu) * lax.rsqrt(var + NORM_EPS)
    o = o.transpose(0, 2, 1, 3).reshape(b, t, RET_V_W).astype(h.dtype) * lp['ret_gn_w'].astype(h.dtype)
    ret_out = (jax.nn.silu(rg) * o) @ lp['w_ret_o']

    q_full = (rms_norm(q_a, lp['mla_q_norm']) @ lp['w_q_b']).reshape(b, t, MLA_HEADS, MLA_NOPE + MLA_ROPE)
    q_nope, q_pe = q_full[..., :MLA_NOPE], q_full[..., MLA_NOPE:]
    ckv = rms_norm(kv_a, lp['mla_kv_norm'])
    if ctx_ckv is None:
        ckv_all, kpe_all = ckv, k_pe
    else:
        cos, sin = axial_rope_tables(t, h.dtype)
        q_pe = apply_axial_rope(q_pe, cos[:, None, :], sin[:, None, :])
        k_pe_lat = apply_axial_rope(k_pe, cos, sin)
        ckv_all = jnp.concatenate([ctx_ckv.astype(h.dtype), ckv], axis=1)
        kpe_all = jnp.concatenate([ctx_kpe.astype(h.dtype), k_pe_lat], axis=1)
    kv = (ckv_all @ lp['w_kv_b']).reshape(b, ckv_all.shape[1], MLA_HEADS, MLA_NOPE + MLA_V)
    k_nope, v_mla = kv[..., :MLA_NOPE], kv[..., MLA_NOPE:]
    attn = mla_attention(q_nope, q_pe, k_nope, kpe_all, v_mla)
    mla_out = attn.reshape(b, t, MLA_HEADS * MLA_V) @ lp['w_mla_o']

    merged = jax.nn.sigmoid(gate_ret) * ret_out + jax.nn.sigmoid(gate_mla) * mla_out
    return merged @ lp['w_out'], ckv, k_pe, s_f, s_b


def routed_experts(xf, e_idx, w, w_exp_in, w_exp_out):
    n_tok = xf.shape[0]
    n_assign = n_tok * TOP_K
    e_flat = e_idx.reshape(n_assign)
    tok_flat = jnp.arange(n_assign, dtype=jnp.int32) // TOP_K
    w_flat = w.reshape(n_assign)
    order = jnp.argsort(e_flat)
    sorted_e = e_flat[order]
    counts = jnp.bincount(e_flat, length=N_EXPERTS)
    starts = jnp.cumsum(counts) - counts
    padded = (counts + EXPERT_BLOCK - 1) // EXPERT_BLOCK * EXPERT_BLOCK
    pends = jnp.cumsum(padded)
    pstarts = pends - padded
    dest = pstarts[sorted_e] + jnp.arange(n_assign, dtype=jnp.int32) - starts[sorted_e]
    n_blocks = (n_assign + N_EXPERTS * (EXPERT_BLOCK - 1) + EXPERT_BLOCK - 1) // EXPERT_BLOCK
    n_rows = n_blocks * EXPERT_BLOCK
    buf_tok = jnp.zeros((n_rows,), jnp.int32).at[dest].set(tok_flat[order])
    buf_w = jnp.zeros((n_rows,), jnp.float32).at[dest].set(w_flat[order])
    block_e = jnp.clip(jnp.searchsorted(pends, jnp.arange(n_blocks, dtype=jnp.int32) * EXPERT_BLOCK, side='right'),
                       0, N_EXPERTS - 1)

    def step(y, inp):
        tok, wb, e = inp
        xb = xf[tok]
        gate, up = jnp.split(xb @ w_exp_in[e], 2, axis=-1)
        out = (jax.nn.silu(gate) * up) @ w_exp_out[e]
        return y.at[tok].add(out * wb[:, None].astype(out.dtype)), None

    y, _ = lax.scan(step, jnp.zeros_like(xf),
                    (buf_tok.reshape(n_blocks, EXPERT_BLOCK), buf_w.reshape(n_blocks, EXPERT_BLOCK), block_e))
    return y


def moe_ffn(h, lp):
    b, t, d = h.shape
    xf = h.reshape(b * t, d)
    logits = xf.astype(jnp.float32) @ lp['w_router'].astype(jnp.float32)
    scores = jax.nn.sigmoid(logits)
    choice = scores + lp['router_bias'].astype(jnp.float32)
    grouped = choice.reshape(b * t, N_GROUPS, N_EXPERTS // N_GROUPS)
    group_scores = lax.top_k(grouped, 2)[0].sum(-1)
    _, g_idx = lax.top_k(group_scores, TOPK_GROUPS)
    g_mask = jnp.any(g_idx[..., None] == jnp.arange(N_GROUPS)[None, None, :], axis=-2)
    e_mask = jnp.repeat(g_mask, N_EXPERTS // N_GROUPS, axis=-1)
    _, e_idx = lax.top_k(jnp.where(e_mask, choice, -jnp.inf), TOP_K)
    w = jnp.take_along_axis(scores, e_idx, axis=-1)
    w = w / jnp.sum(w, axis=-1, keepdims=True) * ROUTED_SCALE
    routed = routed_experts(xf, e_idx, w, lp['w_exp_in'], lp['w_exp_out'])
    s_gate, s_up = jnp.split(xf @ lp['w_shared_in'], 2, axis=-1)
    shared = (jax.nn.silu(s_gate) * s_up) @ lp['w_shared_out']
    return (routed + shared).reshape(b, t, d)


def trunk_layer(x, cond, lp, ret_s0_fwd, ret_s0_bwd, ctx_ckv, ctx_kpe):
    mod = jax.nn.silu(cond) @ lp['w_mod'] + lp['b_mod']
    shift1, scale1, gate1, shift2, scale2, gate2 = [m[:, None, :] for m in jnp.split(mod, 6, axis=-1)]
    h = rms_norm(x, lp['norm_mix']) * (1.0 + scale1) + shift1
    mix, ckv, k_pe, s_f, s_b = token_mixer(h, lp, ret_s0_fwd, ret_s0_bwd, ctx_ckv, ctx_kpe)
    x = x + gate1 * mix
    h = rms_norm(x, lp['norm_ffn']) * (1.0 + scale2) + shift2
    x = x + gate2 * moe_ffn(h, lp)
    return x, ckv, k_pe, s_f, s_b


def setup_inputs(seed: int = 0) -> dict:
    key = jax.random.key(seed)
    ks = iter(jax.random.split(key, 40))
    f32 = jnp.float32

    def nrm(shape, scale):
        return jax.random.normal(next(ks), shape, f32) * scale

    def gain(shape):
        return 1.0 + nrm(shape, 0.02)

    gamma0 = 1.0 - jnp.power(2.0, -5.0 - jnp.arange(RET_HEADS, dtype=f32))
    logit0 = jnp.log(gamma0) - jnp.log1p(-gamma0)
    return {
        'x_prompt': nrm((BATCH, SEQ, D_MODEL), 1.0),
        'x_sample': nrm((DEC_BATCH, DEC_SEQ, D_MODEL), 1.0),
        'c': nrm((DEC_BATCH, D_MODEL), 1.0),
        'cache_mla_ckv': nrm((DEC_BATCH, DEPTH, PAST_LEN, MLA_KV_RANK), 1.0),
        'cache_mla_krope': nrm((DEC_BATCH, DEPTH, PAST_LEN, MLA_ROPE), 1.0),
        'state_ret': nrm((DEC_BATCH, DEPTH, 2, RET_HEADS, RET_DK, RET_DV), 0.1),
        'c_ctx': nrm((D_MODEL,), 1.0),
        'w_mod': nrm((DEPTH, D_MODEL, 6 * D_MODEL), 0.5 * D_MODEL ** -0.5),
        'b_mod': nrm((DEPTH, 6 * D_MODEL), 0.02),
        'norm_mix': gain((DEPTH, D_MODEL)),
        'norm_ffn': gain((DEPTH, D_MODEL)),
        'w_in': nrm((DEPTH, D_MODEL, IN_WIDTH), D_MODEL ** -0.5),
        'ret_decay_logit': logit0[None, None, :] + nrm((DEPTH, 2, RET_HEADS), 0.1),
        'ret_gn_w': gain((DEPTH, RET_V_W)),
        'w_ret_o': nrm((DEPTH, RET_V_W, D_MODEL), RET_V_W ** -0.5),
        'mla_q_norm': gain((DEPTH, MLA_Q_RANK)),
        'w_q_b': nrm((DEPTH, MLA_Q_RANK, MLA_HEADS * (MLA_NOPE + MLA_ROPE)), MLA_Q_RANK ** -0.5),
        'mla_kv_norm': gain((DEPTH, MLA_KV_RANK)),
        'w_kv_b': nrm((DEPTH, MLA_KV_RANK, MLA_HEADS * (MLA_NOPE + MLA_V)), MLA_KV_RANK ** -0.5),
        'w_mla_o': nrm((DEPTH, MLA_HEADS * MLA_V, D_MODEL), (MLA_HEADS * MLA_V) ** -0.5),
        'w_out': nrm((DEPTH, D_MODEL, D_MODEL), D_MODEL ** -0.5),
        'w_router': nrm((DEPTH, D_MODEL, N_EXPERTS), D_MODEL ** -0.5),
        'router_bias': nrm((DEPTH, N_EXPERTS), 0.01),
        'w_exp_in': nrm((DEPTH, N_EXPERTS, D_MODEL, 2 * D_EXPERT), D_MODEL ** -0.5),
        'w_exp_out': nrm((DEPTH, N_EXPERTS, D_EXPERT, D_MODEL), D_EXPERT ** -0.5),
        'w_shared_in': nrm((DEPTH, D_MODEL, 2 * D_SHARED), D_MODEL ** -0.5),
        'w_shared_out': nrm((DEPTH, D_SHARED, D_MODEL), D_SHARED ** -0.5),
        'norm_final': gain((D_MODEL,)),
    }


def reference(x_prompt, x_sample, c, cache_mla_ckv, cache_mla_krope, state_ret, c_ctx,
              w_mod, b_mod, norm_mix, norm_ffn, w_in, ret_decay_logit, ret_gn_w, w_ret_o,
              mla_q_norm, w_q_b, mla_kv_norm, w_kv_b, w_mla_o, w_out, w_router, router_bias,
              w_exp_in, w_exp_out, w_shared_in, w_shared_out, norm_final):
    y_p = x_prompt
    y_s = x_sample
    ckv_layers, kpe_layers, ret_layers = [], [], []
    zero_state = jnp.zeros((x_prompt.shape[0], RET_HEADS, RET_DK, RET_DV), jnp.float32)
    for l in range(DEPTH):
        lp = {
            'w_mod': w_mod[l], 'b_mod': b_mod[l], 'norm_mix': norm_mix[l], 'norm_ffn': norm_ffn[l],
            'w_in': w_in[l], 'ret_decay_logit': ret_decay_logit[l], 'ret_gn_w': ret_gn_w[l],
            'w_ret_o': w_ret_o[l], 'mla_q_norm': mla_q_norm[l], 'w_q_b': w_q_b[l],
            'mla_kv_norm': mla_kv_norm[l], 'w_kv_b': w_kv_b[l], 'w_mla_o': w_mla_o[l], 'w_out': w_out[l],
            'w_router': w_router[l], 'router_bias': router_bias[l], 'w_exp_in': w_exp_in[l],
            'w_exp_out': w_exp_out[l], 'w_shared_in': w_shared_in[l], 'w_shared_out': w_shared_out[l],
        }
        y_p, ckv, k_pe, s_f, s_b = trunk_layer(y_p, c_ctx[None, :], lp, zero_state, zero_state, None, None)
        ckv_layers.append(ckv)
        kpe_layers.append(k_pe)
        ret_layers.append(jnp.stack([s_f, s_b], axis=1))
        y_s, _, _, _, _ = trunk_layer(y_s, c, lp, state_ret[:, l, 0], state_ret[:, l, 1],
                                      cache_mla_ckv[:, l], cache_mla_krope[:, l])
    y_prompt = rms_norm(y_p, norm_final)
    y_sample = rms_norm(y_s, norm_final)
    new_cache_mla_ckv = jnp.stack(ckv_layers, axis=1)
    new_cache_mla_krope = jnp.stack(kpe_layers, axis=1)
    new_state_ret = jnp.stack(ret_layers, axis=1).astype(x_prompt.dtype)
    return (y_prompt, y_sample, new_cache_mla_ckv, new_cache_mla_krope, new_state_ret)
```

```python
import functools

import jax
import jax.numpy as jnp
from jax import lax
from jax.experimental import pallas as pl
from jax.experimental.pallas import tpu as pltpu

F32 = jnp.float32
BF16 = jnp.bfloat16
U32 = jnp.uint32
I32 = jnp.int32

D_MODEL = 2048
GRID_W = 64
NORM_EPS = 1e-6
RET_HEADS = 8
RET_DK = 128
RET_DV = 256
RET_CHUNK = 128
RET_QK_W = RET_HEADS * RET_DK
RET_V_W = RET_HEADS * RET_DV
MLA_HEADS = 16
MLA_Q_RANK = 512
MLA_KV_RANK = 512
MLA_NOPE = 128
MLA_ROPE = 64
MLA_V = 128
ROPE_BASE = 10000.0
N_EXPERTS = 256
TOP_K = 8
N_GROUPS = 8
TOPK_GROUPS = 4
GROUP_SIZE = N_EXPERTS // N_GROUPS
D_EXPERT = 512
D_SHARED = 512
ROUTED_SCALE = 2.5

LANES = 128
HALF = D_MODEL // 2
Q_HEAD_W = 256
N_MOD = 6
MOD_ROWS = 16
TM_NORM = 256
EXPERT_BM = 128
MIB = 1 << 20


def _cparams(sem, vmem_mib=None):
    kw = dict(dimension_semantics=sem)
    if vmem_mib is not None:
        kw["vmem_limit_bytes"] = vmem_mib * MIB
    return pltpu.CompilerParams(**kw)


def _sigmoid(x):
    return 1.0 / (1.0 + jnp.exp(-x))


def _silu(x):
    return x * _sigmoid(x)


def _rms(x, w):
    ms = jnp.mean(x * x, axis=-1, keepdims=True)
    return x * lax.rsqrt(ms + NORM_EPS) * w


def _pack_bf16_pair(lo, hi):
    lo_b = lax.bitcast_convert_type(lo.astype(BF16).astype(F32), U32)
    hi_b = lax.bitcast_convert_type(hi.astype(BF16).astype(F32), U32)
    return (lo_b >> 16) | (hi_b & jnp.uint32(0xFFFF0000))


def _unpack_bf16_pair(w):
    lo = lax.bitcast_convert_type(w << 16, F32)
    hi = lax.bitcast_convert_type(w & jnp.uint32(0xFFFF0000), F32)
    return lo, hi


def _rope128(x, cos, sin):
    lane = lax.broadcasted_iota(I32, x.shape, 1)
    first = (lane & 31) < 16
    rot = jnp.where(first, -pltpu.roll(x, LANES - 16, 1), pltpu.roll(x, 16, 1))
    return x * cos + rot * sin


def _mod_kernel(c_ref, w_ref, b_ref, o_ref):
    s = _silu(c_ref[...]).astype(BF16)
    o_ref[...] = jnp.dot(s, w_ref[...].astype(BF16), preferred_element_type=F32) + b_ref[...]


def _modulation(cond, w_mod, b_mod):
    n = w_mod.shape[1]
    tn = 1024
    return pl.pallas_call(
        _mod_kernel,
        grid=(n // tn,),
        in_specs=[pl.BlockSpec((MOD_ROWS, D_MODEL), lambda j: (0, 0)),
                  pl.BlockSpec((D_MODEL, tn), lambda j: (0, j)),
                  pl.BlockSpec((1, tn), lambda j: (0, j))],
        out_specs=pl.BlockSpec((MOD_ROWS, tn), lambda j: (0, j)),
        out_shape=jax.ShapeDtypeStruct((MOD_ROWS, n), F32),
        compiler_params=_cparams(("arbitrary",), 40),
        name="modulation",
    )(cond, w_mod, b_mod)


def _mod_spec(which, tm, n_ctx_tok, dec_seq):
    def index(i, *_):
        tok = i * tm
        row = jnp.where(tok < n_ctx_tok, 0, 1 + (tok - n_ctx_tok) // dec_seq)
        return (row, which, 0, 0)
    return pl.BlockSpec((None, None, 1, D_MODEL), index)


def _normmod_kernel(x_ref, w_ref, sc_ref, sh_ref, o_ref):
    y = _rms(x_ref[...], w_ref[...])
    o_ref[...] = (y * (1.0 + sc_ref[...]) + sh_ref[...]).astype(BF16)


def _norm_modulate(x, w, mod4, n_ctx_tok, dec_seq):
    t = x.shape[0]
    tm = TM_NORM
    return pl.pallas_call(
        _normmod_kernel,
        grid=(t // tm,),
        in_specs=[pl.BlockSpec((tm, D_MODEL), lambda i: (i, 0)),
                  pl.BlockSpec((1, D_MODEL), lambda i: (0, 0)),
                  _mod_spec(1, tm, n_ctx_tok, dec_seq),
                  _mod_spec(0, tm, n_ctx_tok, dec_seq)],
        out_specs=pl.BlockSpec((tm, D_MODEL), lambda i: (i, 0)),
        out_shape=jax.ShapeDtypeStruct((t, D_MODEL), BF16),
        compiler_params=_cparams(("parallel",)),
        name="norm_modulate",
    )(x, w, mod4, mod4)


def _mm_kernel(a_ref, w_ref, o_ref, *, act):
    acc = jnp.dot(a_ref[...].astype(BF16), w_ref[...], preferred_element_type=F32)
    if act == "sigmoid":
        acc = _sigmoid(acc)
    o_ref[...] = acc.astype(o_ref.dtype)


def _matmul(a, w, *, tm, tn, act=None, name):
    m, k = a.shape
    n = w.shape[1]
    tm = min(tm, m)
    assert m % tm == 0 and n % tn == 0
    return pl.pallas_call(
        functools.partial(_mm_kernel, act=act),
        grid=(n // tn, m // tm),
        in_specs=[pl.BlockSpec((tm, k), lambda j, i: (i, 0)),
                  pl.BlockSpec((k, tn), lambda j, i: (0, j))],
        out_specs=pl.BlockSpec((tm, tn), lambda j, i: (i, j)),
        out_shape=jax.ShapeDtypeStruct((m, n), BF16),
        compiler_params=_cparams(("parallel", "parallel"), 48),
        name=name,
    )(a, w)


def _lowrank_kernel(h_ref, w_ref, qnw_ref, kvnw_ref, cos_ref, sin_ref,
                    qn_ref, ckv_ref, ckvb_ref, kpe_ref, kper_ref):
    acc = jnp.dot(h_ref[...], w_ref[...], preferred_element_type=F32)
    qn_ref[...] = _rms(acc[:, :MLA_Q_RANK], qnw_ref[...]).astype(BF16)
    ckv = _rms(acc[:, MLA_Q_RANK:MLA_Q_RANK + MLA_KV_RANK], kvnw_ref[...])
    ckv_ref[...] = ckv
    ckvb_ref[...] = ckv.astype(BF16)
    kp = acc[:, MLA_Q_RANK + MLA_KV_RANK:]
    kpe_ref[...] = kp
    kper_ref[...] = _rope128(kp, cos_ref[...], sin_ref[...]).astype(BF16)


def _rope_spec(tm, n_ctx_tok, dec_seq):
    per_seq = dec_seq // tm

    def index(i, *_):
        tok = i * tm
        return (jnp.where(tok < n_ctx_tok, 0, 1 + ((tok - n_ctx_tok) // tm) % per_seq), 0)
    return pl.BlockSpec((tm, LANES), index)


def _lowrank(h, w_low, qnw, kvnw, cos_t, sin_t, n_ctx_tok, dec_seq):
    t = h.shape[0]
    tm = TM_NORM
    n = w_low.shape[1]
    row = lambda i: (i, 0)
    const = lambda i: (0, 0)
    return pl.pallas_call(
        _lowrank_kernel,
        grid=(t // tm,),
        in_specs=[pl.BlockSpec((tm, D_MODEL), row),
                  pl.BlockSpec((D_MODEL, n), const),
                  pl.BlockSpec((1, MLA_Q_RANK), const),
                  pl.BlockSpec((1, MLA_KV_RANK), const),
                  _rope_spec(tm, n_ctx_tok, dec_seq),
                  _rope_spec(tm, n_ctx_tok, dec_seq)],
        out_specs=[pl.BlockSpec((tm, MLA_Q_RANK), row),
                   pl.BlockSpec((tm, MLA_KV_RANK), row),
                   pl.BlockSpec((tm, MLA_KV_RANK), row),
                   pl.BlockSpec((tm, LANES), row),
                   pl.BlockSpec((tm, LANES), row)],
        out_shape=[jax.ShapeDtypeStruct((t, MLA_Q_RANK), BF16),
                   jax.ShapeDtypeStruct((t, MLA_KV_RANK), F32),
                   jax.ShapeDtypeStruct((t, MLA_KV_RANK), BF16),
                   jax.ShapeDtypeStruct((t, LANES), F32),
                   jax.ShapeDtypeStruct((t, LANES), BF16)],
        compiler_params=_cparams(("parallel",), 40),
        name="lowrank_proj",
    )(h, w_low, qnw, kvnw, cos_t, sin_t)


def _qb_kernel(qn_ref, w_ref, cos_ref, sin_ref, o_ref):
    qn = qn_ref[...]
    cos = cos_ref[...]
    sin = sin_ref[...]
    for h in range(MLA_HEADS):
        lo = h * Q_HEAD_W
        acc = jnp.dot(qn, w_ref[:, lo:lo + Q_HEAD_W], preferred_element_type=F32)
        o_ref[:, lo:lo + MLA_NOPE] = acc[:, :MLA_NOPE].astype(BF16)
        o_ref[:, lo + MLA_NOPE:lo + Q_HEAD_W] = _rope128(acc[:, MLA_NOPE:], cos, sin).astype(BF16)


def _q_up(qn, w_qb, cos_t, sin_t, n_ctx_tok, dec_seq):
    t = qn.shape[0]
    tm = TM_NORM
    n = w_qb.shape[1]
    return pl.pallas_call(
        _qb_kernel,
        grid=(t // tm,),
        in_specs=[pl.BlockSpec((tm, MLA_Q_RANK), lambda i: (i, 0)),
                  pl.BlockSpec((MLA_Q_RANK, n), lambda i: (0, 0)),
                  _rope_spec(tm, n_ctx_tok, dec_seq),
                  _rope_spec(tm, n_ctx_tok, dec_seq)],
        out_specs=pl.BlockSpec((tm, n), lambda i: (i, 0)),
        out_shape=jax.ShapeDtypeStruct((t, n), BF16),
        compiler_params=_cparams(("parallel",), 40),
        name="q_up_rope",
    )(qn, w_qb, cos_t, sin_t)


def _ret_kernel(logit_ref, q_ref, k_ref, v_ref, g_ref, gnw_ref, *rest, n_chunks, has_s0, emit_state):
    rest = list(rest)
    s0_ref = rest.pop(0) if has_s0 else None
    o_ref = rest.pop(0)
    sfin_ref = rest.pop(0) if emit_state else None
    sf_scr, sb_scr = rest
    c_len = RET_CHUNK
    head = pl.program_id(1)

    def log_gamma(direction):
        l = jnp.full((c_len, 1), logit_ref[direction, head], F32)
        return jnp.minimum(l, 0.0) - jnp.log1p(jnp.exp(-jnp.abs(l)))

    lgf_c = log_gamma(0)
    lgb_c = log_gamma(1)
    ii = lax.broadcasted_iota(I32, (c_len, c_len), 0).astype(F32)
    jj = lax.broadcasted_iota(I32, (c_len, c_len), 1).astype(F32)
    k_scale = RET_DK ** -0.5
    decay = (jnp.where(ii >= jj, jnp.exp(jnp.maximum(ii - jj, 0.0) * lgf_c), 0.0)
             + jnp.where(jj >= ii, jnp.exp(jnp.maximum(jj - ii, 0.0) * lgb_c), 0.0)) * k_scale
    pos = lax.broadcasted_iota(I32, (c_len, 1), 0).astype(F32)
    qdec_f = jnp.exp((pos + 1.0) * lgf_c)
    qdec_b = jnp.exp((c_len - pos) * lgb_c)
    kdec_f = jnp.exp((c_len - 1.0 - pos) * lgf_c) * k_scale
    kdec_b = jnp.exp(pos * lgb_c) * k_scale
    cdec_f = jnp.exp(c_len * lgf_c)
    cdec_b = jnp.exp(c_len * lgb_c)
    tn_dims = (((0,), (0,)), ((), ()))
    nt_dims = (((1,), (1,)), ((), ()))

    def chunk(c):
        return pl.ds(c * c_len, c_len)

    for c in range(n_chunks):
        kc = k_ref[chunk(c), :].astype(F32)
        vc = v_ref[chunk(c), :]
        sf_scr[c] = lax.dot_general((kc * kdec_f).astype(BF16), vc, tn_dims, preferred_element_type=F32)
        sb_scr[c] = lax.dot_general((kc * kdec_b).astype(BF16), vc, tn_dims, preferred_element_type=F32)

    s = s0_ref[0] if has_s0 else jnp.zeros((RET_DK, RET_DV), F32)
    for c in range(n_chunks):
        kv = sf_scr[c]
        sf_scr[c] = s
        s = cdec_f * s + kv
    if emit_state:
        sfin_ref[0] = s
    s = s0_ref[1] if has_s0 else jnp.zeros((RET_DK, RET_DV), F32)
    for c in reversed(range(n_chunks)):
        kv = sb_scr[c]
        sb_scr[c] = s
        s = cdec_b * s + kv
    if emit_state:
        sfin_ref[1] = s

    gnw = gnw_ref[...]
    for c in range(n_chunks):
        qc = q_ref[chunk(c), :]
        kc = k_ref[chunk(c), :]
        vc = v_ref[chunk(c), :]
        sc = lax.dot_general(qc, kc, nt_dims, preferred_element_type=F32) * decay
        qf = qc.astype(F32)
        qcat = jnp.concatenate([(qf * qdec_f).astype(BF16), (qf * qdec_b).astype(BF16)], axis=1)
        scat = jnp.concatenate([sf_scr[c], sb_scr[c]], axis=0).astype(BF16)
        o = (jnp.dot(sc.astype(BF16), vc, preferred_element_type=F32)
             + jnp.dot(qcat, scat, preferred_element_type=F32))
        mu = jnp.mean(o, axis=-1, keepdims=True)
        var = jnp.mean(jnp.square(o - mu), axis=-1, keepdims=True)
        on = (o - mu) * lax.rsqrt(var + NORM_EPS) * gnw
        o_ref[chunk(c), :] = (_silu(g_ref[chunk(c), :].astype(F32)) * on).astype(BF16)


def _retention(qkvg, logits, gn_w, s0, *, n_seq, seq, row_block0, emit_state):
    n_chunks = seq // RET_CHUNK
    has_s0 = s0 is not None
    k_off = RET_QK_W // RET_DK
    v_off = 2 * RET_QK_W // RET_DV
    g_off = (2 * RET_QK_W + RET_V_W) // RET_DV
    in_specs = [pl.BlockSpec(memory_space=pltpu.SMEM),
                pl.BlockSpec((seq, RET_DK), lambda b, h: (row_block0 + b, h)),
                pl.BlockSpec((seq, RET_DK), lambda b, h: (row_block0 + b, k_off + h)),
                pl.BlockSpec((seq, RET_DV), lambda b, h: (row_block0 + b, v_off + h)),
                pl.BlockSpec((seq, RET_DV), lambda b, h: (row_block0 + b, g_off + h)),
                pl.BlockSpec((1, RET_DV), lambda b, h: (0, h))]
    args = [logits, qkvg, qkvg, qkvg, qkvg, gn_w]
    state_spec = pl.BlockSpec((None, 2, None, RET_DK, RET_DV), lambda b, h: (b, 0, h, 0, 0))
    if has_s0:
        in_specs.append(state_spec)
        args.append(s0)
    out_specs = [pl.BlockSpec((seq, RET_DV), lambda b, h: (b, h))]
    out_shape = [jax.ShapeDtypeStruct((n_seq * seq, RET_V_W), BF16)]
    if emit_state:
        out_specs.append(state_spec)
        out_shape.append(jax.ShapeDtypeStruct((n_seq, 2, RET_HEADS, RET_DK, RET_DV), F32))
    res = pl.pallas_call(
        functools.partial(_ret_kernel, n_chunks=n_chunks, has_s0=has_s0, emit_state=emit_state),
        grid=(n_seq, RET_HEADS),
        in_specs=in_specs,
        out_specs=out_specs,
        out_shape=out_shape,
        scratch_shapes=[pltpu.VMEM((n_chunks, RET_DK, RET_DV), F32),
                        pltpu.VMEM((n_chunks, RET_DK, RET_DV), F32)],
        compiler_params=_cparams(("parallel", "parallel")),
        name="retention_seq%d" % seq,
    )(*args)
    return res


ATTN_SCALE = (MLA_NOPE + MLA_ROPE) ** -0.5


def _softmax_pv(q, kcat, v, o_ref):
    s = lax.dot_general(q, kcat, (((1,), (1,)), ((), ())), preferred_element_type=F32) * ATTN_SCALE
    m = jnp.max(s, axis=-1, keepdims=True)
    e = jnp.exp(s - m)
    l = jnp.sum(e, axis=-1, keepdims=True)
    o = jnp.dot(e.astype(BF16), v, preferred_element_type=F32) / l
    o_ref[...] = o.astype(o_ref.dtype)


def _attn_ctx_kernel(q_ref, kv_ref, kpe_ref, o_ref):
    kv = kv_ref[...]
    kcat = jnp.concatenate([kv[:, :MLA_NOPE], kpe_ref[...]], axis=1)
    _softmax_pv(q_ref[...], kcat, kv[:, MLA_NOPE:], o_ref)


def _attention_ctx(q_all, kv_all, kpe_all, n_seq, seq):
    return pl.pallas_call(
        _attn_ctx_kernel,
        grid=(n_seq, MLA_HEADS),
        in_specs=[pl.BlockSpec((seq, Q_HEAD_W), lambda b, h: (b, h)),
                  pl.BlockSpec((seq, MLA_NOPE + MLA_V), lambda b, h: (b, h)),
                  pl.BlockSpec((seq, LANES), lambda b, h: (b, 0))],
        out_specs=pl.BlockSpec((seq, MLA_V), lambda b, h: (b, h)),
        out_shape=jax.ShapeDtypeStruct((n_seq * seq, MLA_HEADS * MLA_V), BF16),
        compiler_params=_cparams(("parallel", "parallel")),
        name="attention_ctx",
    )(q_all, kv_all, kpe_all)


def _attn_lat_kernel(q_ref, kvc_ref, kvl_ref, kpc_ref, kpl_ref, o_ref, kcat, vcat, *, past, seq):
    @pl.when(pl.program_id(2) == 0)
    def _():
        kcat[0:past, 0:MLA_NOPE] = kvc_ref[:, :MLA_NOPE]
        kcat[0:past, MLA_NOPE:] = kpc_ref[...]
        kcat[past:past + seq, 0:MLA_NOPE] = kvl_ref[:, :MLA_NOPE]
        kcat[past:past + seq, MLA_NOPE:] = kpl_ref[...]
        vcat[0:past, :] = kvc_ref[:, MLA_NOPE:]
        vcat[past:past + seq, :] = kvl_ref[:, MLA_NOPE:]

    _softmax_pv(q_ref[...], kcat[...], vcat[...], o_ref)


def _attention_lat(q_all, kv_ctx, kv_all, kpe_ctx, kpe_all, n_seq, seq, past, n_ctx_tok):
    tq = 512
    nq = seq // tq
    q0 = n_ctx_tok // tq
    s0 = n_ctx_tok // seq
    kvw = MLA_NOPE + MLA_V
    return pl.pallas_call(
        functools.partial(_attn_lat_kernel, past=past, seq=seq),
        grid=(n_seq, MLA_HEADS, nq),
        in_specs=[pl.BlockSpec((tq, Q_HEAD_W), lambda b, h, i: (q0 + b * nq + i, h)),
                  pl.BlockSpec((past, kvw), lambda b, h, i: (b, h)),
                  pl.BlockSpec((seq, kvw), lambda b, h, i: (s0 + b, h)),
                  pl.BlockSpec((past, LANES), lambda b, h, i: (b, 0)),
                  pl.BlockSpec((seq, LANES), lambda b, h, i: (s0 + b, 0))],
        out_specs=pl.BlockSpec((tq, MLA_V), lambda b, h, i: (b * nq + i, h)),
        out_shape=jax.ShapeDtypeStruct((n_seq * seq, MLA_HEADS * MLA_V), BF16),
        scratch_shapes=[pltpu.VMEM((past + seq, MLA_NOPE + LANES), BF16),
                        pltpu.VMEM((past + seq, MLA_V), BF16)],
        compiler_params=_cparams(("parallel", "parallel", "arbitrary"), 40),
        name="attention_lat",
    )(q_all, kv_ctx, kv_all, kpe_ctx, kpe_all)


def _merge_kernel(r_ref, a_ref, wr_ref, wm_ref, gr_ref, gm_ref, o_ref):
    ro = jnp.dot(r_ref[...], wr_ref[...], preferred_element_type=F32)
    mo = jnp.dot(a_ref[...], wm_ref[...], preferred_element_type=F32)
    o_ref[...] = (gr_ref[...].astype(F32) * ro + gm_ref[...].astype(F32) * mo).astype(BF16)


def _merge(r_act, attn, w_ret_o, w_mla_o, gates):
    t = r_act.shape[0]
    tm, tn = 512, 1024
    nj = D_MODEL // tn
    return pl.pallas_call(
        _merge_kernel,
        grid=(nj, t // tm),
        in_specs=[pl.BlockSpec((tm, RET_V_W), lambda j, i: (i, 0)),
                  pl.BlockSpec((tm, MLA_HEADS * MLA_V), lambda j, i: (i, 0)),
                  pl.BlockSpec((RET_V_W, tn), lambda j, i: (0, j)),
                  pl.BlockSpec((MLA_HEADS * MLA_V, tn), lambda j, i: (0, j)),
                  pl.BlockSpec((tm, tn), lambda j, i: (i, j)),
                  pl.BlockSpec((tm, tn), lambda j, i: (i, nj + j))],
        out_specs=pl.BlockSpec((tm, tn), lambda j, i: (i, j)),
        out_shape=jax.ShapeDtypeStruct((t, D_MODEL), BF16),
        compiler_params=_cparams(("parallel", "parallel"), 48),
        name="branch_merge",
    )(r_act, attn, w_ret_o, w_mla_o, gates, gates)


def _outproj_kernel(m_ref, w_ref, x_ref, g1_ref, nw_ref, sc_ref, sh_ref,
                    x1_ref, hhi_ref, hlo_ref, hp_ref):
    mix = jnp.dot(m_ref[...], w_ref[...], preferred_element_type=F32)
    x1 = x_ref[...] + g1_ref[...] * mix
    x1_ref[...] = x1
    h2 = _rms(x1, nw_ref[...]) * (1.0 + sc_ref[...]) + sh_ref[...]
    hi = h2.astype(BF16)
    hif = hi.astype(F32)
    hhi_ref[...] = hi
    hlo_ref[...] = (h2 - hif).astype(BF16)
    hp_ref[...] = _pack_bf16_pair(hif[:, :HALF], hif[:, HALF:])


def _out_proj(merged, w_out, x, norm_w, mod4, n_ctx_tok, dec_seq):
    t = x.shape[0]
    tm = TM_NORM
    row = lambda i: (i, 0)
    const = lambda i: (0, 0)
    return pl.pallas_call(
        _outproj_kernel,
        grid=(t // tm,),
        in_specs=[pl.BlockSpec((tm, D_MODEL), row),
                  pl.BlockSpec((D_MODEL, D_MODEL), const),
                  pl.BlockSpec((tm, D_MODEL), row),
                  _mod_spec(2, tm, n_ctx_tok, dec_seq),
                  pl.BlockSpec((1, D_MODEL), const),
                  _mod_spec(4, tm, n_ctx_tok, dec_seq),
                  _mod_spec(3, tm, n_ctx_tok, dec_seq)],
        out_specs=[pl.BlockSpec((tm, D_MODEL), row),
                   pl.BlockSpec((tm, D_MODEL), row),
                   pl.BlockSpec((tm, D_MODEL), row),
                   pl.BlockSpec((tm, HALF), row)],
        out_shape=[jax.ShapeDtypeStruct((t, D_MODEL), F32),
                   jax.ShapeDtypeStruct((t, D_MODEL), BF16),
                   jax.ShapeDtypeStruct((t, D_MODEL), BF16),
                   jax.ShapeDtypeStruct((t, HALF), U32)],
        compiler_params=_cparams(("parallel",), 48),
        name="out_proj_norm",
    )(merged, w_out, x, mod4, norm_w, mod4, mod4)


def _router_kernel(hhi_ref, hlo_ref, wrt_ref, bias_ref, eidx_ref, wts_ref, rank_ref, cnt_ref, carry):
    tm = hhi_ref.shape[0]
    n_e = N_EXPERTS

    @pl.when(pl.program_id(0) == 0)
    def _():
        carry[...] = jnp.zeros_like(carry)

    wr = wrt_ref[...]
    wr_hi = wr.astype(BF16)
    wr_lo = (wr - wr_hi.astype(F32)).astype(BF16)
    hh = hhi_ref[...]
    hl = hlo_ref[...]
    nt_dims = (((1,), (1,)), ((), ()))
    logits = (lax.dot_general(wr_hi, hh, nt_dims, preferred_element_type=F32)
              + lax.dot_general(wr_hi, hl, nt_dims, preferred_element_type=F32)
              + lax.dot_general(wr_lo, hh, nt_dims, preferred_element_type=F32))
    scores = _sigmoid(logits)
    choice = scores + bias_ref[...]
    neg = -jnp.inf
    row = lax.broadcasted_iota(I32, (n_e, tm), 0).astype(F32)
    grow = lax.broadcasted_iota(I32, (GROUP_SIZE, tm), 0).astype(F32)

    gs = []
    for g in range(N_GROUPS):
        blk = choice[g * GROUP_SIZE:(g + 1) * GROUP_SIZE]
        m1 = jnp.max(blk, axis=0, keepdims=True)
        i1 = jnp.min(jnp.where(blk == m1, grow, float(GROUP_SIZE)), axis=0, keepdims=True)
        m2 = jnp.max(jnp.where(grow == i1, neg, blk), axis=0, keepdims=True)
        gs.append(m1 + m2)
    gsc = jnp.concatenate(gs, axis=0)
    gid = lax.broadcasted_iota(I32, (N_GROUPS, tm), 0).astype(F32)
    gsel = jnp.zeros((N_GROUPS, tm), F32)
    for _ in range(TOPK_GROUPS):
        m = jnp.max(gsc, axis=0, keepdims=True)
        idx = jnp.min(jnp.where(gsc == m, gid, float(N_GROUPS)), axis=0, keepdims=True)
        hit = gid == idx
        gsel = jnp.where(hit, 1.0, gsel)
        gsc = jnp.where(hit, neg, gsc)
    masked = jnp.concatenate(
        [jnp.where(gsel[g:g + 1] > 0.0, choice[g * GROUP_SIZE:(g + 1) * GROUP_SIZE], neg)
         for g in range(N_GROUPS)], axis=0)

    idxs, ws = [], []
    onehot = jnp.zeros((n_e, tm), F32)
    for _ in range(TOP_K):
        m = jnp.max(masked, axis=0, keepdims=True)
        idx = jnp.min(jnp.where(masked == m, row, float(n_e)), axis=0, keepdims=True)
        hit = row == idx
        ws.append(jnp.sum(jnp.where(hit, scores, 0.0), axis=0, keepdims=True))
        masked = jnp.where(hit, neg, masked)
        onehot = jnp.where(hit, 1.0, onehot)
        idxs.append(idx)
    w = jnp.concatenate(ws, axis=0)
    wts_ref[...] = w / jnp.sum(w, axis=0, keepdims=True) * ROUTED_SCALE
    eidx_ref[...] = jnp.concatenate(idxs, axis=0).astype(I32)

    t_src = lax.broadcasted_iota(I32, (tm, tm), 0)
    t_dst = lax.broadcasted_iota(I32, (tm, tm), 1)
    before = jnp.where(t_src < t_dst, 1.0, 0.0).astype(BF16)
    prior = jnp.dot(onehot.astype(BF16), before, preferred_element_type=F32) + carry[:, :1]
    rank_ref[...] = jnp.concatenate(
        [jnp.sum(jnp.where(row == idx, prior, 0.0), axis=0, keepdims=True) for idx in idxs],
        axis=0).astype(I32)
    carry[...] = carry[...] + jnp.sum(onehot, axis=1, keepdims=True)
    cnt_ref[...] = carry[...]


def _router(h_hi, h_lo, wr_t, bias_col):
    t = h_hi.shape[0]
    tm = 512
    tok = lambda i: (0, i)
    const = lambda i: (0, 0)
    return pl.pallas_call(
        _router_kernel,
        grid=(t // tm,),
        in_specs=[pl.BlockSpec((tm, D_MODEL), lambda i: (i, 0)),
                  pl.BlockSpec((tm, D_MODEL), lambda i: (i, 0)),
                  pl.BlockSpec((N_EXPERTS, D_MODEL), const),
                  pl.BlockSpec((N_EXPERTS, 1), const)],
        out_specs=[pl.BlockSpec((TOP_K, tm), tok),
                   pl.BlockSpec((TOP_K, tm), tok),
                   pl.BlockSpec((TOP_K, tm), tok),
                   pl.BlockSpec((N_EXPERTS, LANES), const)],
        out_shape=[jax.ShapeDtypeStruct((TOP_K, t), I32),
                   jax.ShapeDtypeStruct((TOP_K, t), F32),
                   jax.ShapeDtypeStruct((TOP_K, t), I32),
                   jax.ShapeDtypeStruct((N_EXPERTS, LANES), F32)],
        scratch_shapes=[pltpu.VMEM((N_EXPERTS, LANES), F32)],
        compiler_params=_cparams(("arbitrary",), 40),
        name="router_topk",
    )(h_hi, h_lo, wr_t, bias_col)


def _dest_kernel(eidx_ref, rank_ref, pstart_ref, dest_ref):
    tm = eidx_ref.shape[1]
    row = lax.broadcasted_iota(I32, (N_EXPERTS, tm), 0)
    pstart = pstart_ref[...]
    eidx = eidx_ref[...]
    base = jnp.concatenate(
        [jnp.sum(jnp.where(row == eidx[k:k + 1], pstart, 0.0), axis=0, keepdims=True)
         for k in range(TOP_K)], axis=0)
    dest_ref[...] = base.astype(I32) + rank_ref[...]


def _dest_rows(eidx, rank, pstart_col):
    t = eidx.shape[1]
    tm = 1024
    tok = lambda i: (0, i)
    return pl.pallas_call(
        _dest_kernel,
        grid=(t // tm,),
        in_specs=[pl.BlockSpec((TOP_K, tm), tok),
                  pl.BlockSpec((TOP_K, tm), tok),
                  pl.BlockSpec((N_EXPERTS, 1), lambda i: (0, 0))],
        out_specs=pl.BlockSpec((TOP_K, tm), tok),
        out_shape=jax.ShapeDtypeStruct((TOP_K, t), I32),
        compiler_params=_cparams(("parallel",)),
        name="dest_rows",
    )(eidx, rank, pstart_col)


def _dispatch_kernel(dest_ref, hp_ref, xs_in_ref, xs_ref, sem):
    del xs_in_ref
    tm = hp_ref.shape[0]

    def issue(t, c):
        for k in range(TOP_K):
            pltpu.make_async_copy(hp_ref.at[pl.ds(t, 1)], xs_ref.at[pl.ds(dest_ref[k, t], 1)], sem).start()
        return c

    lax.fori_loop(0, tm, issue, 0)
    for _ in range(TOP_K):
        pltpu.make_async_copy(hp_ref, xs_ref.at[pl.ds(0, tm)], sem).wait()


def _dispatch(dest, hp, xs_zero):
    t = hp.shape[0]
    tm = TM_NORM
    return pl.pallas_call(
        _dispatch_kernel,
        grid=(t // tm,),
        in_specs=[pl.BlockSpec((TOP_K, tm), lambda i: (0, i), memory_space=pltpu.SMEM),
                  pl.BlockSpec((tm, HALF), lambda i: (i, 0)),
                  pl.BlockSpec(memory_space=pl.ANY)],
        out_specs=pl.BlockSpec(memory_space=pl.ANY),
        out_shape=jax.ShapeDtypeStruct(xs_zero.shape, U32),
        scratch_shapes=[pltpu.SemaphoreType.DMA(())],
        input_output_aliases={2: 0},
        compiler_params=_cparams(("arbitrary",)),
        name="moe_dispatch",
    )(dest, hp, xs_zero)


def _expert_kernel(be_ref, nu_ref, xs_ref, wi_ref, wo_ref, ys_ref, wi_s, wo_s):
    b = pl.program_id(0)

    @pl.when(b < nu_ref[0])
    def _():
        prev = be_ref[jnp.maximum(b - 1, 0)]

        @pl.when((b == 0) | (be_ref[b] != prev))
        def _():
            rows = 256

            def cast_in(i, c):
                r = pl.multiple_of(i * rows, rows)
                wi_s[pl.ds(r, rows), :] = wi_ref[pl.ds(r, rows), :].astype(BF16)
                return c

            def cast_out(i, c):
                r = pl.multiple_of(i * rows, rows)
                wo_s[pl.ds(r, rows), :] = wo_ref[pl.ds(r, rows), :].astype(BF16)
                return c

            lax.fori_loop(0, D_MODEL // rows, cast_in, 0)
            lax.fori_loop(0, D_EXPERT // rows, cast_out, 0)

        lo, hi = _unpack_bf16_pair(xs_ref[...])
        gu = (jnp.dot(lo.astype(BF16), wi_s[:HALF, :], preferred_element_type=F32)
              + jnp.dot(hi.astype(BF16), wi_s[HALF:, :], preferred_element_type=F32))
        act = (_silu(gu[:, :D_EXPERT]) * gu[:, D_EXPERT:]).astype(BF16)
        out = jnp.dot(act, wo_s[...], preferred_element_type=F32)
        ys_ref[...] = _pack_bf16_pair(out[:, :HALF], out[:, HALF:])


def _experts(block_expert, n_used, xs, w_exp_in, w_exp_out):
    bm = EXPERT_BM
    nb = xs.shape[0] // bm
    live = lambda b, be, nu: (jnp.minimum(b, nu[0] - 1), 0)
    grid_spec = pltpu.PrefetchScalarGridSpec(
        num_scalar_prefetch=2,
        grid=(nb,),
        in_specs=[pl.BlockSpec((bm, HALF), live),
                  pl.BlockSpec((None, D_MODEL, 2 * D_EXPERT), lambda b, be, nu: (be[b], 0, 0)),
                  pl.BlockSpec((None, D_EXPERT, D_MODEL), lambda b, be, nu: (be[b], 0, 0))],
        out_specs=pl.BlockSpec((bm, HALF), live),
        scratch_shapes=[pltpu.VMEM((D_MODEL, 2 * D_EXPERT), BF16),
                        pltpu.VMEM((D_EXPERT, D_MODEL), BF16)],
    )
    return pl.pallas_call(
        _expert_kernel,
        grid_spec=grid_spec,
        out_shape=jax.ShapeDtypeStruct(xs.shape, U32),
        input_output_aliases={2: 0},
        compiler_params=_cparams(("arbitrary",), 48),
        name="routed_experts",
    )(block_expert, n_used, xs, w_exp_in, w_exp_out)


def _combine_kernel(dest_ref, wt_ref, x1_ref, hhi_ref, g2_ref, wsi_ref, wso_ref, nf_ref, ys_ref,
                    out_ref, rows, sem):
    tm = x1_ref.shape[0]

    def issue(t, c):
        for k in range(TOP_K):
            pltpu.make_async_copy(ys_ref.at[pl.ds(dest_ref[k, t], 1)], rows.at[k, pl.ds(t, 1)], sem).start()
        return c

    lax.fori_loop(0, tm, issue, 0)

    su = jnp.dot(hhi_ref[...], wsi_ref[...], preferred_element_type=F32)
    sact = (_silu(su[:, :D_SHARED]) * su[:, D_SHARED:]).astype(BF16)
    shared = jnp.dot(sact, wso_ref[...], preferred_element_type=F32)

    for k in range(TOP_K):
        pltpu.make_async_copy(ys_ref.at[pl.ds(0, tm)], rows.at[k], sem).wait()

    wt = wt_ref[...]
    acc_lo = jnp.zeros((tm, HALF), F32)
    acc_hi = jnp.zeros((tm, HALF), F32)
    for k in range(TOP_K):
        lo, hi = _unpack_bf16_pair(rows[k])
        wk = wt[:, k:k + 1]
        acc_lo = acc_lo + wk * lo
        acc_hi = acc_hi + wk * hi
    moe = jnp.concatenate([acc_lo, acc_hi], axis=1) + shared
    x2 = x1_ref[...] + g2_ref[...] * moe
    out_ref[...] = _rms(x2, nf_ref[...])


def _combine(dest, wts_t, x1, h_hi, mod4, w_sh_in, w_sh_out, norm_final, ys, n_ctx_tok, dec_seq):
    t = x1.shape[0]
    tm = TM_NORM
    row = lambda i: (i, 0)
    const = lambda i: (0, 0)
    return pl.pallas_call(
        _combine_kernel,
        grid=(t // tm,),
        in_specs=[pl.BlockSpec((TOP_K, tm), lambda i: (0, i), memory_space=pltpu.SMEM),
                  pl.BlockSpec((tm, TOP_K), row),
                  pl.BlockSpec((tm, D_MODEL), row),
                  pl.BlockSpec((tm, D_MODEL), row),
                  _mod_spec(5, tm, n_ctx_tok, dec_seq),
                  pl.BlockSpec((D_MODEL, 2 * D_SHARED), const),
                  pl.BlockSpec((D_SHARED, D_MODEL), const),
                  pl.BlockSpec((1, D_MODEL), const),
                  pl.BlockSpec(memory_space=pl.ANY)],
        out_specs=pl.BlockSpec((tm, D_MODEL), row),
        out_shape=jax.ShapeDtypeStruct((t, D_MODEL), F32),
        scratch_shapes=[pltpu.VMEM((TOP_K, tm, HALF), U32), pltpu.SemaphoreType.DMA(())],
        compiler_params=_cparams(("arbitrary",), 48),
        name="moe_combine",
    )(dest, wts_t, x1, h_hi, mod4, w_sh_in, w_sh_out, norm_final, ys)


def _rope_tables(tm, dec_seq):
    rows = dec_seq // GRID_W
    row = jnp.broadcast_to(jnp.arange(rows, dtype=F32)[:, None], (rows, GRID_W)).reshape(-1)
    col = jnp.broadcast_to(jnp.arange(GRID_W, dtype=F32)[None, :], (rows, GRID_W)).reshape(-1)
    axis_dim = MLA_ROPE // 2
    freqs = jnp.power(ROPE_BASE, -jnp.arange(0, axis_dim, 2, dtype=F32) / axis_dim)
    ang_r = row[:, None] * freqs
    ang_c = col[:, None] * freqs
    ang = jnp.concatenate([ang_r, ang_r, ang_c, ang_c], axis=-1)
    pad = LANES - MLA_ROPE
    cos = jnp.concatenate([jnp.cos(ang), jnp.ones((dec_seq, pad), F32)], axis=1)
    sin = jnp.concatenate([jnp.sin(ang), jnp.zeros((dec_seq, pad), F32)], axis=1)
    cos = jnp.concatenate([jnp.ones((tm, LANES), F32), cos], axis=0)
    sin = jnp.concatenate([jnp.zeros((tm, LANES), F32), sin], axis=0)
    return cos, sin


def _block_plan(counts, n_blocks):
    bm = EXPERT_BM
    padded = (counts + bm - 1) // bm * bm
    pends = jnp.cumsum(padded)
    pstart = pends - padded
    n_used = (pends[-1] // bm).astype(I32)
    bidx = jnp.arange(n_blocks, dtype=I32)
    be = jnp.clip(jnp.searchsorted(pends, bidx * bm, side="right"), 0, N_EXPERTS - 1).astype(I32)
    be = jnp.where(bidx < n_used, be, be[n_used - 1])
    return pstart, be, n_used.reshape(1)


def kernel(x_prompt, x_sample, c, cache_mla_ckv, cache_mla_krope, state_ret, c_ctx, w_mod, b_mod, norm_mix, norm_ffn, w_in, ret_decay_logit, ret_gn_w, w_ret_o, mla_q_norm, w_q_b, mla_kv_norm, w_kv_b, w_mla_o, w_out, w_router, router_bias, w_exp_in, w_exp_out, w_shared_in, w_shared_out, norm_final):
    n_ctx, seq, d = x_prompt.shape
    n_lat, dec_seq, _ = x_sample.shape
    past = cache_mla_ckv.shape[2]
    n_ctx_tok = n_ctx * seq
    n_lat_tok = n_lat * dec_seq
    t = n_ctx_tok + n_lat_tok
    assert d == D_MODEL and w_mod.shape[0] == 1 and n_lat + 1 <= MOD_ROWS
    assert seq % TM_NORM == 0 and dec_seq % TM_NORM == 0 and n_ctx_tok % dec_seq == 0

    x = jnp.concatenate([x_prompt.reshape(n_ctx_tok, d), x_sample.reshape(n_lat_tok, d)], axis=0)

    cond = jnp.concatenate([c_ctx[None, :], c, jnp.zeros((MOD_ROWS - 1 - n_lat, d), F32)], axis=0)
    mod = _modulation(cond, w_mod.reshape(d, N_MOD * d), b_mod.reshape(1, N_MOD * d))
    mod4 = mod.reshape(MOD_ROWS, N_MOD, 1, d)

    w_in2 = w_in.reshape(d, -1)
    o_g = 2 * RET_QK_W + RET_V_W
    o_qa = o_g + RET_V_W
    o_kpe = o_qa + MLA_Q_RANK + MLA_KV_RANK
    o_gr = o_kpe + MLA_ROPE
    w_qkvg = w_in2[:, :o_qa].astype(BF16)
    w_low = jnp.pad(w_in2[:, o_qa:o_gr], ((0, 0), (0, LANES - MLA_ROPE))).astype(BF16)
    w_gates = w_in2[:, o_gr:].astype(BF16)
    wq3 = w_q_b.reshape(MLA_Q_RANK, MLA_HEADS, MLA_NOPE + MLA_ROPE)
    w_qb = jnp.pad(wq3, ((0, 0), (0, 0), (0, Q_HEAD_W - MLA_NOPE - MLA_ROPE))
                   ).reshape(MLA_Q_RANK, MLA_HEADS * Q_HEAD_W).astype(BF16)
    w_kvb = w_kv_b.reshape(MLA_KV_RANK, -1).astype(BF16)
    cos_t, sin_t = _rope_tables(TM_NORM, dec_seq)

    h1 = _norm_modulate(x, norm_mix.reshape(1, d), mod4, n_ctx_tok, dec_seq)
    qkvg = _matmul(h1, w_qkvg, tm=1024, tn=1024, name="proj_qkvg")
    gates = _matmul(h1, w_gates, tm=1024, tn=1024, act="sigmoid", name="proj_gates")
    qn, ckv, ckv_b, kpe_raw, kpe_rot = _lowrank(
        h1, w_low, mla_q_norm.reshape(1, -1), mla_kv_norm.reshape(1, -1), cos_t, sin_t, n_ctx_tok, dec_seq)

    logits = ret_decay_logit.reshape(2, RET_HEADS)
    gn_w = ret_gn_w.reshape(1, RET_V_W)
    r_ctx, ret_state = _retention(qkvg, logits, gn_w, None, n_seq=n_ctx, seq=seq, row_block0=0,
                                  emit_state=True)
    (r_lat,) = _retention(qkvg, logits, gn_w, state_ret.reshape(n_lat, 2, RET_HEADS, RET_DK, RET_DV),
                          n_seq=n_lat, seq=dec_seq, row_block0=n_ctx_tok // dec_seq, emit_state=False)
    r_act = jnp.concatenate([r_ctx, r_lat], axis=0)

    q_all = _q_up(qn, w_qb, cos_t, sin_t, n_ctx_tok, dec_seq)
    kv_all = _matmul(ckv_b, w_kvb, tm=1024, tn=1024, name="kv_up")
    kv_ctx = _matmul(cache_mla_ckv.reshape(n_lat * past, MLA_KV_RANK), w_kvb, tm=1024, tn=1024,
                     name="kv_up_cache")
    kpe_ctx = jnp.pad(cache_mla_krope.reshape(n_lat * past, MLA_ROPE),
                      ((0, 0), (0, LANES - MLA_ROPE))).astype(BF16)
    a_ctx = _attention_ctx(q_all, kv_all, kpe_rot, n_ctx, seq)
    a_lat = _attention_lat(q_all, kv_ctx, kv_all, kpe_ctx, kpe_rot, n_lat, dec_seq, past, n_ctx_tok)
    attn = jnp.concatenate([a_ctx, a_lat], axis=0)

    merged = _merge(r_act, attn, w_ret_o.reshape(RET_V_W, d).astype(BF16),
                    w_mla_o.reshape(MLA_HEADS * MLA_V, d).astype(BF16), gates)
    x1, h_hi, h_lo, hp = _out_proj(merged, w_out.reshape(d, d).astype(BF16), x, norm_ffn.reshape(1, d),
                                   mod4, n_ctx_tok, dec_seq)

    eidx, wts, rank, cnt = _router(h_hi, h_lo, w_router.reshape(d, N_EXPERTS).T,
                                   router_bias.reshape(N_EXPERTS, 1))
    n_blocks = t * TOP_K // EXPERT_BM + N_EXPERTS
    pstart, block_expert, n_used = _block_plan(cnt[:, 0].astype(I32), n_blocks)
    dest = _dest_rows(eidx, rank, pstart.astype(F32).reshape(N_EXPERTS, 1))
    xs = _dispatch(dest, hp, jnp.zeros((n_blocks * EXPERT_BM, HALF), U32))
    ys = _experts(block_expert, n_used, xs, w_exp_in.reshape(N_EXPERTS, d, 2 * D_EXPERT),
                  w_exp_out.reshape(N_EXPERTS, D_EXPERT, d))
    y = _combine(dest, wts.T, x1, h_hi, mod4, w_shared_in.reshape(d, 2 * D_SHARED).astype(BF16),
                 w_shared_out.reshape(D_SHARED, d).astype(BF16), norm_final.reshape(1, d), ys,
                 n_ctx_tok, dec_seq)

    y_prompt = y[:n_ctx_tok].reshape(n_ctx, seq, d)
    y_sample = y[n_ctx_tok:].reshape(n_lat, dec_seq, d)
    new_ckv = ckv[:n_ctx_tok].reshape(n_ctx, 1, seq, MLA_KV_RANK)
    new_krope = kpe_raw[:n_ctx_tok, :MLA_ROPE].reshape(n_ctx, 1, seq, MLA_ROPE)
    new_state = ret_state.reshape(n_ctx, 1, 2, RET_HEADS, RET_DK, RET_DV)
    return (y_prompt, y_sample, new_ckv, new_krope, new_state)
```

```python
import functools

import jax
import jax.numpy as jnp
from jax import lax
from jax.experimental import pallas as pl
from jax.experimental.pallas import tpu as pltpu

F32 = jnp.float32
BF16 = jnp.bfloat16
U32 = jnp.uint32
I32 = jnp.int32

D_MODEL = 2048
GRID_W = 64
NORM_EPS = 1e-6
RET_HEADS = 8
RET_DK = 128
RET_DV = 256
RET_CHUNK = 128
RET_QK_W = RET_HEADS * RET_DK
RET_V_W = RET_HEADS * RET_DV
MLA_HEADS = 16
MLA_Q_RANK = 512
MLA_KV_RANK = 512
MLA_NOPE = 128
MLA_ROPE = 64
MLA_V = 128
ROPE_BASE = 10000.0
N_EXPERTS = 256
TOP_K = 8
N_GROUPS = 8
TOPK_GROUPS = 4
GROUP_SIZE = N_EXPERTS // N_GROUPS
D_EXPERT = 512
D_SHARED = 512
ROUTED_SCALE = 2.5

LANES = 128
SUBLANES = 8
ROW_TILES = D_MODEL // (2 * LANES)
assert ROW_TILES == SUBLANES
Q_HEAD_W = 256
N_MOD = 6
MOD_ROWS = 16
TM_NORM = 256
EXPERT_BM = 128
MIB = 1 << 20


def _cparams(sem, vmem_mib=None):
    kw = dict(dimension_semantics=sem)
    if vmem_mib is not None:
        kw["vmem_limit_bytes"] = vmem_mib * MIB
    return pltpu.CompilerParams(**kw)


def _sigmoid(x):
    return 1.0 / (1.0 + jnp.exp(-x))


def _silu(x):
    return x * _sigmoid(x)


def _rms(x, w):
    ms = jnp.mean(x * x, axis=-1, keepdims=True)
    return x * lax.rsqrt(ms + NORM_EPS) * w


def _pack_bf16_pair(lo, hi):
    return pltpu.pack_elementwise([lo, hi], packed_dtype=BF16)


def _unpack_bf16_pair(w):
    lo = pltpu.unpack_elementwise(w, index=0, packed_dtype=BF16, unpacked_dtype=F32)
    hi = pltpu.unpack_elementwise(w, index=1, packed_dtype=BF16, unpacked_dtype=F32)
    return lo, hi


def _store_row_tiles(ref, x):
    m = x.shape[0]
    for s in range(ROW_TILES):
        lo = x[:, 2 * LANES * s:2 * LANES * s + LANES]
        hi = x[:, 2 * LANES * s + LANES:2 * LANES * (s + 1)]
        ref[pl.ds(s, m, SUBLANES), :] = _pack_bf16_pair(lo, hi)


def _load_row_tiles(ref, m):
    parts = []
    for s in range(ROW_TILES):
        parts.extend(_unpack_bf16_pair(ref[pl.ds(s, m, SUBLANES), :]))
    return parts


def _rope128(x, cos, sin):
    lane = lax.broadcasted_iota(I32, x.shape, 1)
    first = (lane & 31) < 16
    rot = jnp.where(first, -pltpu.roll(x, LANES - 16, 1), pltpu.roll(x, 16, 1))
    return x * cos + rot * sin


def _mod_kernel(c_ref, w_ref, b_ref, o_ref):
    s = _silu(c_ref[...]).astype(BF16)
    o_ref[...] = jnp.dot(s, w_ref[...].astype(BF16), preferred_element_type=F32) + b_ref[...]


def _modulation(cond, w_mod, b_mod):
    n = w_mod.shape[1]
    tn = 1024
    return pl.pallas_call(
        _mod_kernel,
        grid=(n // tn,),
        in_specs=[pl.BlockSpec((MOD_ROWS, D_MODEL), lambda j: (0, 0)),
                  pl.BlockSpec((D_MODEL, tn), lambda j: (0, j)),
                  pl.BlockSpec((1, tn), lambda j: (0, j))],
        out_specs=pl.BlockSpec((MOD_ROWS, tn), lambda j: (0, j)),
        out_shape=jax.ShapeDtypeStruct((MOD_ROWS, n), F32),
        compiler_params=_cparams(("arbitrary",), 40),
        name="modulation",
    )(cond, w_mod, b_mod)


def _mod_spec(which, tm, n_ctx_tok, dec_seq):
    def index(i, *_):
        tok = i * tm
        row = jnp.where(tok < n_ctx_tok, 0, 1 + (tok - n_ctx_tok) // dec_seq)
        return (row, which, 0, 0)
    return pl.BlockSpec((None, None, 1, D_MODEL), index)


def _normmod_kernel(x_ref, w_ref, sc_ref, sh_ref, o_ref):
    y = _rms(x_ref[...], w_ref[...])
    o_ref[...] = (y * (1.0 + sc_ref[...]) + sh_ref[...]).astype(BF16)


def _norm_modulate(x, w, mod4, n_ctx_tok, dec_seq):
    t = x.shape[0]
    tm = TM_NORM
    return pl.pallas_call(
        _normmod_kernel,
        grid=(t // tm,),
        in_specs=[pl.BlockSpec((tm, D_MODEL), lambda i: (i, 0)),
                  pl.BlockSpec((1, D_MODEL), lambda i: (0, 0)),
                  _mod_spec(1, tm, n_ctx_tok, dec_seq),
                  _mod_spec(0, tm, n_ctx_tok, dec_seq)],
        out_specs=pl.BlockSpec((tm, D_MODEL), lambda i: (i, 0)),
        out_shape=jax.ShapeDtypeStruct((t, D_MODEL), BF16),
        compiler_params=_cparams(("parallel",)),
        name="norm_modulate",
    )(x, w, mod4, mod4)


def _mm_kernel(a_ref, w_ref, o_ref, *, act):
    acc = jnp.dot(a_ref[...].astype(BF16), w_ref[...], preferred_element_type=F32)
    if act == "sigmoid":
        acc = _sigmoid(acc)
    o_ref[...] = acc.astype(o_ref.dtype)


def _matmul(a, w, *, tm, tn, act=None, name):
    m, k = a.shape
    n = w.shape[1]
    tm = min(tm, m)
    assert m % tm == 0 and n % tn == 0
    return pl.pallas_call(
        functools.partial(_mm_kernel, act=act),
        grid=(n // tn, m // tm),
        in_specs=[pl.BlockSpec((tm, k), lambda j, i: (i, 0)),
                  pl.BlockSpec((k, tn), lambda j, i: (0, j))],
        out_specs=pl.BlockSpec((tm, tn), lambda j, i: (i, j)),
        out_shape=jax.ShapeDtypeStruct((m, n), BF16),
        compiler_params=_cparams(("parallel", "parallel"), 48),
        name=name,
    )(a, w)


def _lowrank_kernel(h_ref, w_ref, qnw_ref, kvnw_ref, cos_ref, sin_ref,
                    qn_ref, ckv_ref, ckvb_ref, kpe_ref, kper_ref):
    acc = jnp.dot(h_ref[...], w_ref[...], preferred_element_type=F32)
    qn_ref[...] = _rms(acc[:, :MLA_Q_RANK], qnw_ref[...]).astype(BF16)
    ckv = _rms(acc[:, MLA_Q_RANK:MLA_Q_RANK + MLA_KV_RANK], kvnw_ref[...])
    ckv_ref[...] = ckv
    ckvb_ref[...] = ckv.astype(BF16)
    kp = acc[:, MLA_Q_RANK + MLA_KV_RANK:]
    kpe_ref[...] = kp
    kper_ref[...] = _rope128(kp, cos_ref[...], sin_ref[...]).astype(BF16)


def _rope_spec(tm, n_ctx_tok, dec_seq):
    per_seq = dec_seq // tm

    def index(i, *_):
        tok = i * tm
        return (jnp.where(tok < n_ctx_tok, 0, 1 + ((tok - n_ctx_tok) // tm) % per_seq), 0)
    return pl.BlockSpec((tm, LANES), index)


def _lowrank(h, w_low, qnw, kvnw, cos_t, sin_t, n_ctx_tok, dec_seq):
    t = h.shape[0]
    tm = TM_NORM
    n = w_low.shape[1]
    row = lambda i: (i, 0)
    const = lambda i: (0, 0)
    return pl.pallas_call(
        _lowrank_kernel,
        grid=(t // tm,),
        in_specs=[pl.BlockSpec((tm, D_MODEL), row),
                  pl.BlockSpec((D_MODEL, n), const),
                  pl.BlockSpec((1, MLA_Q_RANK), const),
                  pl.BlockSpec((1, MLA_KV_RANK), const),
                  _rope_spec(tm, n_ctx_tok, dec_seq),
                  _rope_spec(tm, n_ctx_tok, dec_seq)],
        out_specs=[pl.BlockSpec((tm, MLA_Q_RANK), row),
                   pl.BlockSpec((tm, MLA_KV_RANK), row),
                   pl.BlockSpec((tm, MLA_KV_RANK), row),
                   pl.BlockSpec((tm, LANES), row),
                   pl.BlockSpec((tm, LANES), row)],
        out_shape=[jax.ShapeDtypeStruct((t, MLA_Q_RANK), BF16),
                   jax.ShapeDtypeStruct((t, MLA_KV_RANK), F32),
                   jax.ShapeDtypeStruct((t, MLA_KV_RANK), BF16),
                   jax.ShapeDtypeStruct((t, LANES), F32),
                   jax.ShapeDtypeStruct((t, LANES), BF16)],
        compiler_params=_cparams(("parallel",), 40),
        name="lowrank_proj",
    )(h, w_low, qnw, kvnw, cos_t, sin_t)


Q_PRESCALE = (MLA_NOPE + MLA_ROPE) ** -0.5 * 1.4426950408889634


def _qb_kernel(qn_ref, w_ref, cos_ref, sin_ref, o_ref):
    qn = qn_ref[...]
    cos = cos_ref[...]
    sin = sin_ref[...]
    for h in range(MLA_HEADS):
        lo = h * Q_HEAD_W
        acc = jnp.dot(qn, w_ref[:, lo:lo + Q_HEAD_W], preferred_element_type=F32) * Q_PRESCALE
        o_ref[:, lo:lo + MLA_NOPE] = acc[:, :MLA_NOPE].astype(BF16)
        o_ref[:, lo + MLA_NOPE:lo + Q_HEAD_W] = _rope128(acc[:, MLA_NOPE:], cos, sin).astype(BF16)


def _q_up(qn, w_qb, cos_t, sin_t, n_ctx_tok, dec_seq):
    t = qn.shape[0]
    tm = TM_NORM
    n = w_qb.shape[1]
    return pl.pallas_call(
        _qb_kernel,
        grid=(t // tm,),
        in_specs=[pl.BlockSpec((tm, MLA_Q_RANK), lambda i: (i, 0)),
                  pl.BlockSpec((MLA_Q_RANK, n), lambda i: (0, 0)),
                  _rope_spec(tm, n_ctx_tok, dec_seq),
                  _rope_spec(tm, n_ctx_tok, dec_seq)],
        out_specs=pl.BlockSpec((tm, n), lambda i: (i, 0)),
        out_shape=jax.ShapeDtypeStruct((t, n), BF16),
        compiler_params=_cparams(("parallel",), 40),
        name="q_up_rope",
    )(qn, w_qb, cos_t, sin_t)


def _ret_kernel(logit_ref, q_ref, k_ref, v_ref, g_ref, gnw_ref, *rest, n_chunks, has_s0, emit_state):
    rest = list(rest)
    s0_ref = rest.pop(0) if has_s0 else None
    o_ref = rest.pop(0)
    sfin_ref = rest.pop(0) if emit_state else None
    sf_scr, sb_scr = rest
    c_len = RET_CHUNK
    head = pl.program_id(1)

    def log_gamma(direction):
        l = jnp.full((c_len, 1), logit_ref[direction, head], F32)
        return jnp.minimum(l, 0.0) - jnp.log1p(jnp.exp(-jnp.abs(l)))

    lgf_c = log_gamma(0)
    lgb_c = log_gamma(1)
    ii = lax.broadcasted_iota(I32, (c_len, c_len), 0).astype(F32)
    jj = lax.broadcasted_iota(I32, (c_len, c_len), 1).astype(F32)
    k_scale = RET_DK ** -0.5
    decay = (jnp.where(ii >= jj, jnp.exp(jnp.maximum(ii - jj, 0.0) * lgf_c), 0.0)
             + jnp.where(jj >= ii, jnp.exp(jnp.maximum(jj - ii, 0.0) * lgb_c), 0.0)) * k_scale
    pos = lax.broadcasted_iota(I32, (c_len, 1), 0).astype(F32)
    qdec_f = jnp.exp((pos + 1.0) * lgf_c)
    qdec_b = jnp.exp((c_len - pos) * lgb_c)
    kdec_f = jnp.exp((c_len - 1.0 - pos) * lgf_c) * k_scale
    kdec_b = jnp.exp(pos * lgb_c) * k_scale
    cdec_f = jnp.exp(c_len * lgf_c)
    cdec_b = jnp.exp(c_len * lgb_c)
    tn_dims = (((0,), (0,)), ((), ()))
    nt_dims = (((1,), (1,)), ((), ()))

    def chunk(c):
        return pl.ds(c * c_len, c_len)

    for c in range(n_chunks):
        kc = k_ref[chunk(c), :].astype(F32)
        vc = v_ref[chunk(c), :]
        sf_scr[c] = lax.dot_general((kc * kdec_f).astype(BF16), vc, tn_dims, preferred_element_type=F32)
        sb_scr[c] = lax.dot_general((kc * kdec_b).astype(BF16), vc, tn_dims, preferred_element_type=F32)

    s = s0_ref[0] if has_s0 else jnp.zeros((RET_DK, RET_DV), F32)
    for c in range(n_chunks):
        kv = sf_scr[c]
        sf_scr[c] = s
        s = cdec_f * s + kv
    if emit_state:
        sfin_ref[0] = s
    s = s0_ref[1] if has_s0 else jnp.zeros((RET_DK, RET_DV), F32)
    for c in reversed(range(n_chunks)):
        kv = sb_scr[c]
        sb_scr[c] = s
        s = cdec_b * s + kv
    if emit_state:
        sfin_ref[1] = s

    gnw = gnw_ref[...]
    for c in range(n_chunks):
        qc = q_ref[chunk(c), :]
        kc = k_ref[chunk(c), :]
        vc = v_ref[chunk(c), :]
        sc = lax.dot_general(qc, kc, nt_dims, preferred_element_type=F32) * decay
        qf = qc.astype(F32)
        qcat = jnp.concatenate([(qf * qdec_f).astype(BF16), (qf * qdec_b).astype(BF16)], axis=1)
        scat = jnp.concatenate([sf_scr[c], sb_scr[c]], axis=0).astype(BF16)
        o = (jnp.dot(sc.astype(BF16), vc, preferred_element_type=F32)
             + jnp.dot(qcat, scat, preferred_element_type=F32))
        mu = jnp.mean(o, axis=-1, keepdims=True)
        var = jnp.mean(jnp.square(o - mu), axis=-1, keepdims=True)
        on = (o - mu) * lax.rsqrt(var + NORM_EPS) * gnw
        o_ref[chunk(c), :] = (_silu(g_ref[chunk(c), :].astype(F32)) * on).astype(BF16)


def _retention(qkvg, logits, gn_w, s0, *, n_seq, seq, row_block0, emit_state):
    n_chunks = seq // RET_CHUNK
    has_s0 = s0 is not None
    k_off = RET_QK_W // RET_DK
    v_off = 2 * RET_QK_W // RET_DV
    g_off = (2 * RET_QK_W + RET_V_W) // RET_DV
    in_specs = [pl.BlockSpec(memory_space=pltpu.SMEM),
                pl.BlockSpec((seq, RET_DK), lambda b, h: (row_block0 + b, h)),
                pl.BlockSpec((seq, RET_DK), lambda b, h: (row_block0 + b, k_off + h)),
                pl.BlockSpec((seq, RET_DV), lambda b, h: (row_block0 + b, v_off + h)),
                pl.BlockSpec((seq, RET_DV), lambda b, h: (row_block0 + b, g_off + h)),
                pl.BlockSpec((1, RET_DV), lambda b, h: (0, h))]
    args = [logits, qkvg, qkvg, qkvg, qkvg, gn_w]
    state_spec = pl.BlockSpec((None, 2, None, RET_DK, RET_DV), lambda b, h: (b, 0, h, 0, 0))
    if has_s0:
        in_specs.append(state_spec)
        args.append(s0)
    out_specs = [pl.BlockSpec((seq, RET_DV), lambda b, h: (b, h))]
    out_shape = [jax.ShapeDtypeStruct((n_seq * seq, RET_V_W), BF16)]
    if emit_state:
        out_specs.append(state_spec)
        out_shape.append(jax.ShapeDtypeStruct((n_seq, 2, RET_HEADS, RET_DK, RET_DV), F32))
    res = pl.pallas_call(
        functools.partial(_ret_kernel, n_chunks=n_chunks, has_s0=has_s0, emit_state=emit_state),
        grid=(n_seq, RET_HEADS),
        in_specs=in_specs,
        out_specs=out_specs,
        out_shape=out_shape,
        scratch_shapes=[pltpu.VMEM((n_chunks, RET_DK, RET_DV), F32),
                        pltpu.VMEM((n_chunks, RET_DK, RET_DV), F32)],
        compiler_params=_cparams(("parallel", "parallel")),
        name="retention_seq%d" % seq,
    )(*args)
    return res


V_EXT_W = MLA_V + LANES


def _ones_column(rows):
    lane = lax.broadcasted_iota(I32, (rows, LANES), 1)
    return jnp.where(lane == 0, 1.0, 0.0).astype(BF16)


def _softmax_pv(q, kcat, v_ext):
    s = lax.dot_general(q, kcat, (((1,), (1,)), ((), ())), preferred_element_type=F32)
    e = jnp.exp2(s - jnp.max(s, axis=-1, keepdims=True))
    o = jnp.dot(e.astype(BF16), v_ext, preferred_element_type=F32)
    return o[:, :MLA_V] / o[:, MLA_V:MLA_V + 1]


def _attn_ctx_kernel(q_ref, kv_ref, kpe_ref, o_ref):
    kpe = kpe_ref[...]
    ones = _ones_column(kpe.shape[0])
    kvw = MLA_NOPE + MLA_V
    for h in range(MLA_HEADS):
        kv = kv_ref[:, h * kvw:(h + 1) * kvw]
        kcat = jnp.concatenate([kv[:, :MLA_NOPE], kpe], axis=1)
        v_ext = jnp.concatenate([kv[:, MLA_NOPE:], ones], axis=1)
        o = _softmax_pv(q_ref[:, h * Q_HEAD_W:(h + 1) * Q_HEAD_W], kcat, v_ext)
        o_ref[:, h * MLA_V:(h + 1) * MLA_V] = o.astype(o_ref.dtype)


def _attention_ctx(q_all, kv_all, kpe_all, n_seq, seq):
    return pl.pallas_call(
        _attn_ctx_kernel,
        grid=(n_seq,),
        in_specs=[pl.BlockSpec((seq, MLA_HEADS * Q_HEAD_W), lambda b: (b, 0)),
                  pl.BlockSpec((seq, MLA_HEADS * (MLA_NOPE + MLA_V)), lambda b: (b, 0)),
                  pl.BlockSpec((seq, LANES), lambda b: (b, 0))],
        out_specs=pl.BlockSpec((seq, MLA_HEADS * MLA_V), lambda b: (b, 0)),
        out_shape=jax.ShapeDtypeStruct((n_seq * seq, MLA_HEADS * MLA_V), BF16),
        compiler_params=_cparams(("parallel",), 40),
        name="attention_ctx",
    )(q_all, kv_all, kpe_all)


def _attn_lat_kernel(q_ref, kvc_ref, kvl_ref, kpc_ref, kpl_ref, o_ref, kcat, vcat, *, past, seq):
    @pl.when(pl.program_id(2) == 0)
    def _():
        kcat[0:past, 0:MLA_NOPE] = kvc_ref[:, :MLA_NOPE]
        kcat[0:past, MLA_NOPE:] = kpc_ref[...]
        kcat[past:past + seq, 0:MLA_NOPE] = kvl_ref[:, :MLA_NOPE]
        kcat[past:past + seq, MLA_NOPE:] = kpl_ref[...]
        vcat[0:past, 0:MLA_V] = kvc_ref[:, MLA_NOPE:]
        vcat[past:past + seq, 0:MLA_V] = kvl_ref[:, MLA_NOPE:]
        vcat[:, MLA_V:] = _ones_column(past + seq)

    o_ref[...] = _softmax_pv(q_ref[...], kcat[...], vcat[...]).astype(o_ref.dtype)


def _attention_lat(q_all, kv_ctx, kv_all, kpe_ctx, kpe_all, n_seq, seq, past, n_ctx_tok):
    tq = 512
    nq = seq // tq
    q0 = n_ctx_tok // tq
    s0 = n_ctx_tok // seq
    kvw = MLA_NOPE + MLA_V
    return pl.pallas_call(
        functools.partial(_attn_lat_kernel, past=past, seq=seq),
        grid=(n_seq, MLA_HEADS, nq),
        in_specs=[pl.BlockSpec((tq, Q_HEAD_W), lambda b, h, i: (q0 + b * nq + i, h)),
                  pl.BlockSpec((past, kvw), lambda b, h, i: (b, h)),
                  pl.BlockSpec((seq, kvw), lambda b, h, i: (s0 + b, h)),
                  pl.BlockSpec((past, LANES), lambda b, h, i: (b, 0)),
                  pl.BlockSpec((seq, LANES), lambda b, h, i: (s0 + b, 0))],
        out_specs=pl.BlockSpec((tq, MLA_V), lambda b, h, i: (b * nq + i, h)),
        out_shape=jax.ShapeDtypeStruct((n_seq * seq, MLA_HEADS * MLA_V), BF16),
        scratch_shapes=[pltpu.VMEM((past + seq, MLA_NOPE + LANES), BF16),
                        pltpu.VMEM((past + seq, V_EXT_W), BF16)],
        compiler_params=_cparams(("parallel", "parallel", "arbitrary"), 40),
        name="attention_lat",
    )(q_all, kv_ctx, kv_all, kpe_ctx, kpe_all)


def _merge_kernel(r_ref, a_ref, wr_ref, wm_ref, gr_ref, gm_ref, o_ref):
    ro = jnp.dot(r_ref[...], wr_ref[...], preferred_element_type=F32)
    mo = jnp.dot(a_ref[...], wm_ref[...], preferred_element_type=F32)
    o_ref[...] = (gr_ref[...].astype(F32) * ro + gm_ref[...].astype(F32) * mo).astype(BF16)


def _merge(r_act, attn, w_ret_o, w_mla_o, gates):
    t = r_act.shape[0]
    tm, tn = 512, 1024
    nj = D_MODEL // tn
    return pl.pallas_call(
        _merge_kernel,
        grid=(nj, t // tm),
        in_specs=[pl.BlockSpec((tm, RET_V_W), lambda j, i: (i, 0)),
                  pl.BlockSpec((tm, MLA_HEADS * MLA_V), lambda j, i: (i, 0)),
                  pl.BlockSpec((RET_V_W, tn), lambda j, i: (0, j)),
                  pl.BlockSpec((MLA_HEADS * MLA_V, tn), lambda j, i: (0, j)),
                  pl.BlockSpec((tm, tn), lambda j, i: (i, j)),
                  pl.BlockSpec((tm, tn), lambda j, i: (i, nj + j))],
        out_specs=pl.BlockSpec((tm, tn), lambda j, i: (i, j)),
        out_shape=jax.ShapeDtypeStruct((t, D_MODEL), BF16),
        compiler_params=_cparams(("parallel", "parallel"), 48),
        name="branch_merge",
    )(r_act, attn, w_ret_o, w_mla_o, gates, gates)


def _outproj_kernel(m_ref, w_ref, x_ref, g1_ref, nw_ref, sc_ref, sh_ref,
                    x1_ref, hhi_ref, hlo_ref, hp_ref):
    mix = jnp.dot(m_ref[...], w_ref[...], preferred_element_type=F32)
    x1 = x_ref[...] + g1_ref[...] * mix
    x1_ref[...] = x1
    h2 = _rms(x1, nw_ref[...]) * (1.0 + sc_ref[...]) + sh_ref[...]
    hi = h2.astype(BF16)
    hif = hi.astype(F32)
    hhi_ref[...] = hi
    hlo_ref[...] = (h2 - hif).astype(BF16)
    _store_row_tiles(hp_ref, hif)


def _out_proj(merged, w_out, x, norm_w, mod4, n_ctx_tok, dec_seq):
    t = x.shape[0]
    tm = TM_NORM
    row = lambda i: (i, 0)
    const = lambda i: (0, 0)
    return pl.pallas_call(
        _outproj_kernel,
        grid=(t // tm,),
        in_specs=[pl.BlockSpec((tm, D_MODEL), row),
                  pl.BlockSpec((D_MODEL, D_MODEL), const),
                  pl.BlockSpec((tm, D_MODEL), row),
                  _mod_spec(2, tm, n_ctx_tok, dec_seq),
                  pl.BlockSpec((1, D_MODEL), const),
                  _mod_spec(4, tm, n_ctx_tok, dec_seq),
                  _mod_spec(3, tm, n_ctx_tok, dec_seq)],
        out_specs=[pl.BlockSpec((tm, D_MODEL), row),
                   pl.BlockSpec((tm, D_MODEL), row),
                   pl.BlockSpec((tm, D_MODEL), row),
                   pl.BlockSpec((tm * ROW_TILES, LANES), row)],
        out_shape=[jax.ShapeDtypeStruct((t, D_MODEL), F32),
                   jax.ShapeDtypeStruct((t, D_MODEL), BF16),
                   jax.ShapeDtypeStruct((t, D_MODEL), BF16),
                   jax.ShapeDtypeStruct((t * ROW_TILES, LANES), U32)],
        compiler_params=_cparams(("parallel",), 48),
        name="out_proj_norm",
    )(merged, w_out, x, mod4, norm_w, mod4, mod4)


def _router_kernel(hhi_ref, hlo_ref, wrt_ref, bias_ref, eidx_ref, wts_ref, rank_ref, cnt_ref, carry):
    tm = hhi_ref.shape[0]
    n_e = N_EXPERTS

    @pl.when(pl.program_id(0) == 0)
    def _():
        carry[...] = jnp.zeros_like(carry)

    wr = wrt_ref[...]
    wr_hi = wr.astype(BF16)
    wr_lo = (wr - wr_hi.astype(F32)).astype(BF16)
    hh = hhi_ref[...]
    hl = hlo_ref[...]
    nt_dims = (((1,), (1,)), ((), ()))
    logits = (lax.dot_general(wr_hi, hh, nt_dims, preferred_element_type=F32)
              + lax.dot_general(wr_hi, hl, nt_dims, preferred_element_type=F32)
              + lax.dot_general(wr_lo, hh, nt_dims, preferred_element_type=F32))
    scores = _sigmoid(logits)
    choice = scores + bias_ref[...]
    neg = -jnp.inf
    row = lax.broadcasted_iota(I32, (n_e, tm), 0).astype(F32)
    grow = lax.broadcasted_iota(I32, (GROUP_SIZE, tm), 0).astype(F32)

    gs = []
    for g in range(N_GROUPS):
        blk = choice[g * GROUP_SIZE:(g + 1) * GROUP_SIZE]
        m1 = jnp.max(blk, axis=0, keepdims=True)
        i1 = jnp.min(jnp.where(blk == m1, grow, float(GROUP_SIZE)), axis=0, keepdims=True)
        m2 = jnp.max(jnp.where(grow == i1, neg, blk), axis=0, keepdims=True)
        gs.append(m1 + m2)
    gsc = jnp.concatenate(gs, axis=0)
    gid = lax.broadcasted_iota(I32, (N_GROUPS, tm), 0).astype(F32)
    gsel = jnp.zeros((N_GROUPS, tm), F32)
    for _ in range(TOPK_GROUPS):
        m = jnp.max(gsc, axis=0, keepdims=True)
        idx = jnp.min(jnp.where(gsc == m, gid, float(N_GROUPS)), axis=0, keepdims=True)
        hit = gid == idx
        gsel = jnp.where(hit, 1.0, gsel)
        gsc = jnp.where(hit, neg, gsc)
    masked = jnp.concatenate(
        [jnp.where(gsel[g:g + 1] > 0.0, choice[g * GROUP_SIZE:(g + 1) * GROUP_SIZE], neg)
         for g in range(N_GROUPS)], axis=0)

    idxs, ws = [], []
    onehot = jnp.zeros((n_e, tm), F32)
    for _ in range(TOP_K):
        m = jnp.max(masked, axis=0, keepdims=True)
        idx = jnp.min(jnp.where(masked == m, row, float(n_e)), axis=0, keepdims=True)
        hit = row == idx
        ws.append(jnp.sum(jnp.where(hit, scores, 0.0), axis=0, keepdims=True))
        masked = jnp.where(hit, neg, masked)
        onehot = jnp.where(hit, 1.0, onehot)
        idxs.append(idx)
    w = jnp.concatenate(ws, axis=0)
    wts_ref[...] = w / jnp.sum(w, axis=0, keepdims=True) * ROUTED_SCALE
    eidx_ref[...] = jnp.concatenate(idxs, axis=0).astype(I32)

    t_src = lax.broadcasted_iota(I32, (tm, tm), 0)
    t_dst = lax.broadcasted_iota(I32, (tm, tm), 1)
    before = jnp.where(t_src < t_dst, 1.0, 0.0).astype(BF16)
    prior = jnp.dot(onehot.astype(BF16), before, preferred_element_type=F32) + carry[:, :1]
    rank_ref[...] = jnp.concatenate(
        [jnp.sum(jnp.where(row == idx, prior, 0.0), axis=0, keepdims=True) for idx in idxs],
        axis=0).astype(I32)
    carry[...] = carry[...] + jnp.sum(onehot, axis=1, keepdims=True)
    cnt_ref[...] = carry[...]


def _router(h_hi, h_lo, wr_t, bias_col):
    t = h_hi.shape[0]
    tm = 512
    tok = lambda i: (0, i)
    const = lambda i: (0, 0)
    return pl.pallas_call(
        _router_kernel,
        grid=(t // tm,),
        in_specs=[pl.BlockSpec((tm, D_MODEL), lambda i: (i, 0)),
                  pl.BlockSpec((tm, D_MODEL), lambda i: (i, 0)),
                  pl.BlockSpec((N_EXPERTS, D_MODEL), const),
                  pl.BlockSpec((N_EXPERTS, 1), const)],
        out_specs=[pl.BlockSpec((TOP_K, tm), tok),
                   pl.BlockSpec((TOP_K, tm), tok),
                   pl.BlockSpec((TOP_K, tm), tok),
                   pl.BlockSpec((N_EXPERTS, LANES), const)],
        out_shape=[jax.ShapeDtypeStruct((TOP_K, t), I32),
                   jax.ShapeDtypeStruct((TOP_K, t), F32),
                   jax.ShapeDtypeStruct((TOP_K, t), I32),
                   jax.ShapeDtypeStruct((N_EXPERTS, LANES), F32)],
        scratch_shapes=[pltpu.VMEM((N_EXPERTS, LANES), F32)],
        compiler_params=_cparams(("arbitrary",), 40),
        name="router_topk",
    )(h_hi, h_lo, wr_t, bias_col)


def _dest_kernel(eidx_ref, rank_ref, pstart_ref, dest_ref):
    tm = eidx_ref.shape[1]
    row = lax.broadcasted_iota(I32, (N_EXPERTS, tm), 0)
    pstart = pstart_ref[...]
    eidx = eidx_ref[...]
    base = jnp.concatenate(
        [jnp.sum(jnp.where(row == eidx[k:k + 1], pstart, 0.0), axis=0, keepdims=True)
         for k in range(TOP_K)], axis=0)
    dest_ref[...] = (base.astype(I32) + rank_ref[...]) * ROW_TILES


def _dest_rows(eidx, rank, pstart_col):
    t = eidx.shape[1]
    tm = 1024
    tok = lambda i: (0, i)
    return pl.pallas_call(
        _dest_kernel,
        grid=(t // tm,),
        in_specs=[pl.BlockSpec((TOP_K, tm), tok),
                  pl.BlockSpec((TOP_K, tm), tok),
                  pl.BlockSpec((N_EXPERTS, 1), lambda i: (0, 0))],
        out_specs=pl.BlockSpec((TOP_K, tm), tok),
        out_shape=jax.ShapeDtypeStruct((TOP_K, t), I32),
        compiler_params=_cparams(("parallel",)),
        name="dest_rows",
    )(eidx, rank, pstart_col)


def _dispatch_kernel(dest_ref, hp_ref, xs_in_ref, xs_ref, sem):
    del xs_in_ref
    tm = hp_ref.shape[0]

    def issue(t, c):
        src = hp_ref.at[pl.ds(pl.multiple_of(t * ROW_TILES, ROW_TILES), ROW_TILES)]
        for k in range(TOP_K):
            r = pl.multiple_of(dest_ref[t * TOP_K + k], ROW_TILES)
            pltpu.make_async_copy(src, xs_ref.at[pl.ds(r, ROW_TILES)], sem).start(priority=k % 2)
        return c

    lax.fori_loop(0, tm // ROW_TILES, issue, 0)
    for _ in range(TOP_K):
        pltpu.make_async_copy(hp_ref, xs_ref.at[pl.ds(0, tm)], sem).wait()


def _dispatch(dest, hp, xs_zero):
    tm = TM_NORM * ROW_TILES
    return pl.pallas_call(
        _dispatch_kernel,
        grid=(hp.shape[0] // tm,),
        in_specs=[pl.BlockSpec((TM_NORM * TOP_K,), lambda i: (i,), memory_space=pltpu.SMEM),
                  pl.BlockSpec((tm, LANES), lambda i: (i, 0)),
                  pl.BlockSpec(memory_space=pl.ANY)],
        out_specs=pl.BlockSpec(memory_space=pl.ANY),
        out_shape=jax.ShapeDtypeStruct(xs_zero.shape, U32),
        scratch_shapes=[pltpu.SemaphoreType.DMA(())],
        input_output_aliases={2: 0},
        compiler_params=_cparams(("arbitrary",)),
        name="moe_dispatch",
    )(dest, hp, xs_zero)


N_WSLOTS = 2


def _expert_kernel(uidx_ref, ue_ref, meta_ref, xs_ref, wi_hbm, wo_hbm, ys_ref,
                   wi_st, wo_st, wi_s, wo_s, sem):
    b = pl.program_id(0)
    n_used = meta_ref[0]
    n_ue = meta_ref[1]

    def weight_copies(j, slot):
        e = ue_ref[j]
        return (pltpu.make_async_copy(wi_hbm.at[e], wi_st.at[slot], sem.at[0, slot]),
                pltpu.make_async_copy(wo_hbm.at[e], wo_st.at[slot], sem.at[1, slot]))

    @pl.when(b < n_used)
    def _():
        j = uidx_ref[b]

        @pl.when(b == 0)
        def _():
            for cp in weight_copies(0, 0):
                cp.start()

            @pl.when(n_ue > 1)
            def _():
                for cp in weight_copies(1, 1):
                    cp.start()

        @pl.when((b == 0) | (j != uidx_ref[jnp.maximum(b - 1, 0)]))
        def _():
            slot = j % N_WSLOTS
            for cp in weight_copies(j, slot):
                cp.wait()
            rows = 256

            def cast_in(i, c):
                r = pl.multiple_of(i * rows, rows)
                wi_s[pl.ds(r, rows), :] = wi_st[slot, pl.ds(r, rows), :].astype(BF16)
                return c

            def cast_out(i, c):
                r = pl.multiple_of(i * rows, rows)
                wo_s[pl.ds(r, rows), :] = wo_st[slot, pl.ds(r, rows), :].astype(BF16)
                return c

            lax.fori_loop(0, D_MODEL // rows, cast_in, 0)
            lax.fori_loop(0, D_EXPERT // rows, cast_out, 0)

            @pl.when(j + N_WSLOTS < n_ue)
            def _():
                for cp in weight_copies(j + N_WSLOTS, slot):
                    cp.start()

        bm = EXPERT_BM
        x = jnp.concatenate([p.astype(BF16) for p in _load_row_tiles(xs_ref, bm)], axis=1)
        gu = jnp.dot(x, wi_s[...], preferred_element_type=F32)
        act = (_silu(gu[:, :D_EXPERT]) * gu[:, D_EXPERT:]).astype(BF16)
        _store_row_tiles(ys_ref, jnp.dot(act, wo_s[...], preferred_element_type=F32))


def _experts(uidx, ue, meta, xs, w_exp_in, w_exp_out):
    rows = EXPERT_BM * ROW_TILES
    nb = xs.shape[0] // rows
    live = lambda b, uidx, ue, meta: (jnp.minimum(b, meta[0] - 1), 0)
    grid_spec = pltpu.PrefetchScalarGridSpec(
        num_scalar_prefetch=3,
        grid=(nb,),
        in_specs=[pl.BlockSpec((rows, LANES), live),
                  pl.BlockSpec(memory_space=pl.ANY),
                  pl.BlockSpec(memory_space=pl.ANY)],
        out_specs=pl.BlockSpec((rows, LANES), live),
        scratch_shapes=[pltpu.VMEM((N_WSLOTS, D_MODEL, 2 * D_EXPERT), F32),
                        pltpu.VMEM((N_WSLOTS, D_EXPERT, D_MODEL), F32),
                        pltpu.VMEM((D_MODEL, 2 * D_EXPERT), BF16),
                        pltpu.VMEM((D_EXPERT, D_MODEL), BF16),
                        pltpu.SemaphoreType.DMA((2, N_WSLOTS))],
    )
    return pl.pallas_call(
        _expert_kernel,
        grid_spec=grid_spec,
        out_shape=jax.ShapeDtypeStruct(xs.shape, U32),
        input_output_aliases={3: 0},
        compiler_params=_cparams(("arbitrary",), 52),
        name="routed_experts",
    )(uidx, ue, meta, xs, w_exp_in, w_exp_out)


def _combine_kernel(dest_ref, dnext_ref, wt_ref, x1_ref, hhi_ref, g2_ref, wsi_ref, wso_ref, nf_ref, ys_ref,
                    out_ref, rows, sem):
    tm = x1_ref.shape[0]
    i = pl.program_id(0)
    n = pl.num_programs(0)

    def gather(idx_ref, slot):
        def issue(t, c):
            for k in range(TOP_K):
                r = pl.multiple_of(idx_ref[t * TOP_K + k], ROW_TILES)
                dst = rows.at[slot, k, pl.ds(pl.multiple_of(t * ROW_TILES, ROW_TILES), ROW_TILES)]
                pltpu.make_async_copy(ys_ref.at[pl.ds(r, ROW_TILES)], dst, sem.at[slot]).start(priority=k % 2)
            return c

        lax.fori_loop(0, tm, issue, 0)

    @pl.when(i == 0)
    def _():
        gather(dest_ref, 0)

    @pl.when(i + 1 < n)
    def _():
        gather(dnext_ref, (i + 1) % 2)

    su = jnp.dot(hhi_ref[...], wsi_ref[...], preferred_element_type=F32)
    sact = (_silu(su[:, :D_SHARED]) * su[:, D_SHARED:]).astype(BF16)
    shared = jnp.dot(sact, wso_ref[...], preferred_element_type=F32)

    slot = i % 2
    for k in range(TOP_K):
        pltpu.make_async_copy(ys_ref.at[pl.ds(0, tm * ROW_TILES)], rows.at[slot, k], sem.at[slot]).wait()

    wt = wt_ref[...]
    acc = None
    for k in range(TOP_K):
        wk = wt[:, k:k + 1]
        parts = [wk * p for p in _load_row_tiles(rows.at[slot, k], tm)]
        acc = parts if acc is None else [a + p for a, p in zip(acc, parts)]
    moe = jnp.concatenate(acc, axis=1) + shared
    x2 = x1_ref[...] + g2_ref[...] * moe
    out_ref[...] = _rms(x2, nf_ref[...])


def _combine(dest, wts_t, x1, h_hi, mod4, w_sh_in, w_sh_out, norm_final, ys, n_ctx_tok, dec_seq):
    t = x1.shape[0]
    tm = TM_NORM
    n = t // tm
    row = lambda i: (i, 0)
    const = lambda i: (0, 0)
    return pl.pallas_call(
        _combine_kernel,
        grid=(n,),
        in_specs=[pl.BlockSpec((tm * TOP_K,), lambda i: (i,), memory_space=pltpu.SMEM),
                  pl.BlockSpec((tm * TOP_K,), lambda i: (jnp.minimum(i + 1, n - 1),), memory_space=pltpu.SMEM),
                  pl.BlockSpec((tm, TOP_K), row),
                  pl.BlockSpec((tm, D_MODEL), row),
                  pl.BlockSpec((tm, D_MODEL), row),
                  _mod_spec(5, tm, n_ctx_tok, dec_seq),
                  pl.BlockSpec((D_MODEL, 2 * D_SHARED), const),
                  pl.BlockSpec((D_SHARED, D_MODEL), const),
                  pl.BlockSpec((1, D_MODEL), const),
                  pl.BlockSpec(memory_space=pl.ANY)],
        out_specs=pl.BlockSpec((tm, D_MODEL), row),
        out_shape=jax.ShapeDtypeStruct((t, D_MODEL), F32),
        scratch_shapes=[pltpu.VMEM((2, TOP_K, tm * ROW_TILES, LANES), U32), pltpu.SemaphoreType.DMA((2,))],
        compiler_params=_cparams(("arbitrary",), 56),
        name="moe_combine",
    )(dest, dest, wts_t, x1, h_hi, mod4, w_sh_in, w_sh_out, norm_final, ys)


def _rope_tables(tm, dec_seq):
    rows = dec_seq // GRID_W
    row = jnp.broadcast_to(jnp.arange(rows, dtype=F32)[:, None], (rows, GRID_W)).reshape(-1)
    col = jnp.broadcast_to(jnp.arange(GRID_W, dtype=F32)[None, :], (rows, GRID_W)).reshape(-1)
    axis_dim = MLA_ROPE // 2
    freqs = jnp.power(ROPE_BASE, -jnp.arange(0, axis_dim, 2, dtype=F32) / axis_dim)
    ang_r = row[:, None] * freqs
    ang_c = col[:, None] * freqs
    ang = jnp.concatenate([ang_r, ang_r, ang_c, ang_c], axis=-1)
    pad = LANES - MLA_ROPE
    cos = jnp.concatenate([jnp.cos(ang), jnp.ones((dec_seq, pad), F32)], axis=1)
    sin = jnp.concatenate([jnp.sin(ang), jnp.zeros((dec_seq, pad), F32)], axis=1)
    cos = jnp.concatenate([jnp.ones((tm, LANES), F32), cos], axis=0)
    sin = jnp.concatenate([jnp.zeros((tm, LANES), F32), sin], axis=0)
    return cos, sin


def _block_plan(counts, n_blocks):
    bm = EXPERT_BM
    nblk = (counts + bm - 1) // bm
    bend = jnp.cumsum(nblk)
    bstart = bend - nblk
    present = nblk > 0
    pcum = jnp.cumsum(present.astype(I32))
    bidx = jnp.arange(n_blocks, dtype=I32)
    uidx = jnp.sum((present[None, :] & (bstart[None, :] <= bidx[:, None])).astype(I32), axis=1) - 1
    order = jnp.arange(N_EXPERTS, dtype=I32)
    ue = jnp.minimum(jnp.sum((pcum[None, :] <= order[:, None]).astype(I32), axis=1), N_EXPERTS - 1)
    meta = jnp.stack([bend[-1], pcum[-1]]).astype(I32)
    return bstart * bm, uidx.astype(I32), ue.astype(I32), meta


def kernel(x_prompt, x_sample, c, cache_mla_ckv, cache_mla_krope, state_ret, c_ctx, w_mod, b_mod, norm_mix, norm_ffn, w_in, ret_decay_logit, ret_gn_w, w_ret_o, mla_q_norm, w_q_b, mla_kv_norm, w_kv_b, w_mla_o, w_out, w_router, router_bias, w_exp_in, w_exp_out, w_shared_in, w_shared_out, norm_final):
    n_ctx, seq, d = x_prompt.shape
    n_lat, dec_seq, _ = x_sample.shape
    past = cache_mla_ckv.shape[2]
    n_ctx_tok = n_ctx * seq
    n_lat_tok = n_lat * dec_seq
    t = n_ctx_tok + n_lat_tok
    assert d == D_MODEL and w_mod.shape[0] == 1 and n_lat + 1 <= MOD_ROWS
    assert seq % TM_NORM == 0 and dec_seq % TM_NORM == 0 and n_ctx_tok % dec_seq == 0

    x = jnp.concatenate([x_prompt.reshape(n_ctx_tok, d), x_sample.reshape(n_lat_tok, d)], axis=0)

    cond = jnp.concatenate([c_ctx[None, :], c, jnp.zeros((MOD_ROWS - 1 - n_lat, d), F32)], axis=0)
    mod = _modulation(cond, w_mod.reshape(d, N_MOD * d), b_mod.reshape(1, N_MOD * d))
    mod4 = mod.reshape(MOD_ROWS, N_MOD, 1, d)

    w_in2 = w_in.reshape(d, -1)
    o_g = 2 * RET_QK_W + RET_V_W
    o_qa = o_g + RET_V_W
    o_kpe = o_qa + MLA_Q_RANK + MLA_KV_RANK
    o_gr = o_kpe + MLA_ROPE
    w_qkvg = w_in2[:, :o_qa].astype(BF16)
    w_low = jnp.pad(w_in2[:, o_qa:o_gr], ((0, 0), (0, LANES - MLA_ROPE))).astype(BF16)
    w_gates = w_in2[:, o_gr:].astype(BF16)
    wq3 = w_q_b.reshape(MLA_Q_RANK, MLA_HEADS, MLA_NOPE + MLA_ROPE)
    w_qb = jnp.pad(wq3, ((0, 0), (0, 0), (0, Q_HEAD_W - MLA_NOPE - MLA_ROPE))
                   ).reshape(MLA_Q_RANK, MLA_HEADS * Q_HEAD_W).astype(BF16)
    w_kvb = w_kv_b.reshape(MLA_KV_RANK, -1).astype(BF16)
    cos_t, sin_t = _rope_tables(TM_NORM, dec_seq)

    h1 = _norm_modulate(x, norm_mix.reshape(1, d), mod4, n_ctx_tok, dec_seq)
    qkvg = _matmul(h1, w_qkvg, tm=1024, tn=1024, name="proj_qkvg")
    gates = _matmul(h1, w_gates, tm=1024, tn=1024, act="sigmoid", name="proj_gates")
    qn, ckv, ckv_b, kpe_raw, kpe_rot = _lowrank(
        h1, w_low, mla_q_norm.reshape(1, -1), mla_kv_norm.reshape(1, -1), cos_t, sin_t, n_ctx_tok, dec_seq)

    logits = ret_decay_logit.reshape(2, RET_HEADS)
    gn_w = ret_gn_w.reshape(1, RET_V_W)
    r_ctx, ret_state = _retention(qkvg, logits, gn_w, None, n_seq=n_ctx, seq=seq, row_block0=0,
                                  emit_state=True)
    (r_lat,) = _retention(qkvg, logits, gn_w, state_ret.reshape(n_lat, 2, RET_HEADS, RET_DK, RET_DV),
                          n_seq=n_lat, seq=dec_seq, row_block0=n_ctx_tok // dec_seq, emit_state=False)
    r_act = jnp.concatenate([r_ctx, r_lat], axis=0)

    q_all = _q_up(qn, w_qb, cos_t, sin_t, n_ctx_tok, dec_seq)
    kv_all = _matmul(ckv_b, w_kvb, tm=1024, tn=1024, name="kv_up")
    kv_ctx = _matmul(cache_mla_ckv.reshape(n_lat * past, MLA_KV_RANK), w_kvb, tm=1024, tn=1024,
                     name="kv_up_cache")
    kpe_ctx = jnp.pad(cache_mla_krope.reshape(n_lat * past, MLA_ROPE),
                      ((0, 0), (0, LANES - MLA_ROPE))).astype(BF16)
    a_ctx = _attention_ctx(q_all, kv_all, kpe_rot, n_ctx, seq)
    a_lat = _attention_lat(q_all, kv_ctx, kv_all, kpe_ctx, kpe_rot, n_lat, dec_seq, past, n_ctx_tok)
    attn = jnp.concatenate([a_ctx, a_lat], axis=0)

    merged = _merge(r_act, attn, w_ret_o.reshape(RET_V_W, d).astype(BF16),
                    w_mla_o.reshape(MLA_HEADS * MLA_V, d).astype(BF16), gates)
    x1, h_hi, h_lo, hp = _out_proj(merged, w_out.reshape(d, d).astype(BF16), x, norm_ffn.reshape(1, d),
                                   mod4, n_ctx_tok, dec_seq)

    eidx, wts, rank, cnt = _router(h_hi, h_lo, w_router.reshape(d, N_EXPERTS).T,
                                   router_bias.reshape(N_EXPERTS, 1))
    n_blocks = t * TOP_K // EXPERT_BM + N_EXPERTS
    pstart, uidx, ue, meta = _block_plan(cnt[:, 0].astype(I32), n_blocks)
    dest = _dest_rows(eidx, rank, pstart.astype(F32).reshape(N_EXPERTS, 1)).T.reshape(-1)
    xs = _dispatch(dest, hp, jnp.zeros((n_blocks * EXPERT_BM * ROW_TILES, LANES), U32))
    ys = _experts(uidx, ue, meta, xs, w_exp_in.reshape(N_EXPERTS, d, 2 * D_EXPERT),
                  w_exp_out.reshape(N_EXPERTS, D_EXPERT, d))
    y = _combine(dest, wts.T, x1, h_hi, mod4, w_shared_in.reshape(d, 2 * D_SHARED).astype(BF16),
                 w_shared_out.reshape(D_SHARED, d).astype(BF16), norm_final.reshape(1, d), ys,
                 n_ctx_tok, dec_seq)

    y_prompt = y[:n_ctx_tok].reshape(n_ctx, seq, d)
    y_sample = y[n_ctx_tok:].reshape(n_lat, dec_seq, d)
    new_ckv = ckv[:n_ctx_tok].reshape(n_ctx, 1, seq, MLA_KV_RANK)
    new_krope = kpe_raw[:n_ctx_tok, :MLA_ROPE].reshape(n_ctx, 1, seq, MLA_ROPE)
    new_state = ret_state.reshape(n_ctx, 1, 2, RET_HEADS, RET_DK, RET_DV)
    return (y_prompt, y_sample, new_ckv, new_krope, new_state)
```

```python
import functools

import jax
import jax.numpy as jnp
from jax import lax
from jax.experimental import pallas as pl
from jax.experimental.pallas import tpu as pltpu

F32 = jnp.float32
BF16 = jnp.bfloat16
U32 = jnp.uint32
I32 = jnp.int32

D_MODEL = 2048
GRID_W = 64
NORM_EPS = 1e-6
RET_HEADS = 8
RET_DK = 128
RET_DV = 256
RET_CHUNK = 128
RET_QK_W = RET_HEADS * RET_DK
RET_V_W = RET_HEADS * RET_DV
MLA_HEADS = 16
MLA_Q_RANK = 512
MLA_KV_RANK = 512
MLA_NOPE = 128
MLA_ROPE = 64
MLA_V = 128
ROPE_BASE = 10000.0
N_EXPERTS = 256
TOP_K = 8
N_GROUPS = 8
TOPK_GROUPS = 4
GROUP_SIZE = N_EXPERTS // N_GROUPS
D_EXPERT = 512
D_SHARED = 512
ROUTED_SCALE = 2.5

LANES = 128
SUBLANES = 8
ROW_TILES = D_MODEL // (2 * LANES)
assert ROW_TILES == SUBLANES
Q_HEAD_W = 256
N_MOD = 6
MOD_ROWS = 16
TM_NORM = 256
EXPERT_BM = 256
MIB = 1 << 20


def _cparams(sem, vmem_mib=None):
    kw = dict(dimension_semantics=sem)
    if vmem_mib is not None:
        kw["vmem_limit_bytes"] = vmem_mib * MIB
    return pltpu.CompilerParams(**kw)


def _sigmoid(x):
    return 1.0 / (1.0 + jnp.exp(-x))


def _silu(x):
    return x * _sigmoid(x)


def _rms(x, w):
    ms = jnp.mean(x * x, axis=-1, keepdims=True)
    return x * lax.rsqrt(ms + NORM_EPS) * w


def _pack_bf16_pair(lo, hi):
    return pltpu.pack_elementwise([lo, hi], packed_dtype=BF16)


def _unpack_bf16_pair(w):
    lo = pltpu.unpack_elementwise(w, index=0, packed_dtype=BF16, unpacked_dtype=F32)
    hi = pltpu.unpack_elementwise(w, index=1, packed_dtype=BF16, unpacked_dtype=F32)
    return lo, hi


def _store_row_tiles(ref, x):
    m = x.shape[0]
    for s in range(ROW_TILES):
        lo = x[:, 2 * LANES * s:2 * LANES * s + LANES]
        hi = x[:, 2 * LANES * s + LANES:2 * LANES * (s + 1)]
        ref[pl.ds(s, m, SUBLANES), :] = _pack_bf16_pair(lo, hi)


def _load_row_tiles(ref, m):
    parts = []
    for s in range(ROW_TILES):
        parts.extend(_unpack_bf16_pair(ref[pl.ds(s, m, SUBLANES), :]))
    return parts


def _rope128(x, cos, sin):
    lane = lax.broadcasted_iota(I32, x.shape, 1)
    first = (lane & 31) < 16
    rot = jnp.where(first, -pltpu.roll(x, LANES - 16, 1), pltpu.roll(x, 16, 1))
    return x * cos + rot * sin


def _mod_kernel(c_ref, w_ref, b_ref, o_ref):
    s = _silu(c_ref[...]).astype(BF16)
    o_ref[...] = jnp.dot(s, w_ref[...].astype(BF16), preferred_element_type=F32) + b_ref[...]


def _modulation(cond, w_mod, b_mod):
    n = w_mod.shape[1]
    tn = 1024
    return pl.pallas_call(
        _mod_kernel,
        grid=(n // tn,),
        in_specs=[pl.BlockSpec((MOD_ROWS, D_MODEL), lambda j: (0, 0)),
                  pl.BlockSpec((D_MODEL, tn), lambda j: (0, j)),
                  pl.BlockSpec((1, tn), lambda j: (0, j))],
        out_specs=pl.BlockSpec((MOD_ROWS, tn), lambda j: (0, j)),
        out_shape=jax.ShapeDtypeStruct((MOD_ROWS, n), F32),
        compiler_params=_cparams(("arbitrary",), 40),
        name="modulation",
    )(cond, w_mod, b_mod)


def _mod_spec(which, tm, n_ctx_tok, dec_seq):
    def index(i, *_):
        tok = i * tm
        row = jnp.where(tok < n_ctx_tok, 0, 1 + (tok - n_ctx_tok) // dec_seq)
        return (row, which, 0, 0)
    return pl.BlockSpec((None, None, 1, D_MODEL), index)


def _path_specs(tm, width, n_ctx_tok, axis=0):
    n_ctx_tiles = n_ctx_tok // tm

    def ctx(*idx):
        return (jnp.minimum(idx[axis], n_ctx_tiles - 1), 0)

    def lat(*idx):
        return (jnp.maximum(idx[axis] - n_ctx_tiles, 0), 0)

    return pl.BlockSpec((tm, width), ctx), pl.BlockSpec((tm, width), lat)


def _path_pick(tile, tm, n_ctx_tok, ctx_ref, lat_ref):
    return jnp.where(tile * tm < n_ctx_tok, ctx_ref[...], lat_ref[...])


def _normmod_kernel(xc_ref, xl_ref, w_ref, sc_ref, sh_ref, o_ref, *, n_ctx_tok):
    x = _path_pick(pl.program_id(0), o_ref.shape[0], n_ctx_tok, xc_ref, xl_ref)
    y = _rms(x, w_ref[...])
    o_ref[...] = (y * (1.0 + sc_ref[...]) + sh_ref[...]).astype(BF16)


def _norm_modulate(x_ctx, x_lat, w, mod4, n_ctx_tok, dec_seq):
    t = x_ctx.shape[0] + x_lat.shape[0]
    tm = TM_NORM
    return pl.pallas_call(
        functools.partial(_normmod_kernel, n_ctx_tok=n_ctx_tok),
        grid=(t // tm,),
        in_specs=[*_path_specs(tm, D_MODEL, n_ctx_tok),
                  pl.BlockSpec((1, D_MODEL), lambda i: (0, 0)),
                  _mod_spec(1, tm, n_ctx_tok, dec_seq),
                  _mod_spec(0, tm, n_ctx_tok, dec_seq)],
        out_specs=pl.BlockSpec((tm, D_MODEL), lambda i: (i, 0)),
        out_shape=jax.ShapeDtypeStruct((t, D_MODEL), BF16),
        compiler_params=_cparams(("parallel",)),
        name="norm_modulate",
    )(x_ctx, x_lat, w, mod4, mod4)


def _mm_kernel(a_ref, w_ref, o_ref, *, act):
    acc = jnp.dot(a_ref[...].astype(BF16), w_ref[...], preferred_element_type=F32)
    if act == "sigmoid":
        acc = _sigmoid(acc)
    o_ref[...] = acc.astype(o_ref.dtype)


def _matmul(a, w, *, tm, tn, act=None, name):
    m, k = a.shape
    n = w.shape[1]
    tm = min(tm, m)
    assert m % tm == 0 and n % tn == 0
    return pl.pallas_call(
        functools.partial(_mm_kernel, act=act),
        grid=(n // tn, m // tm),
        in_specs=[pl.BlockSpec((tm, k), lambda j, i: (i, 0)),
                  pl.BlockSpec((k, tn), lambda j, i: (0, j))],
        out_specs=pl.BlockSpec((tm, tn), lambda j, i: (i, j)),
        out_shape=jax.ShapeDtypeStruct((m, n), BF16),
        compiler_params=_cparams(("parallel", "parallel"), 48),
        name=name,
    )(a, w)


def _lowrank_kernel(h_ref, w_ref, qnw_ref, kvnw_ref, cos_ref, sin_ref,
                    qn_ref, ckv_ref, ckvb_ref, kpe_ref, kper_ref):
    acc = jnp.dot(h_ref[...], w_ref[...], preferred_element_type=F32)
    qn_ref[...] = _rms(acc[:, :MLA_Q_RANK], qnw_ref[...]).astype(BF16)
    ckv = _rms(acc[:, MLA_Q_RANK:MLA_Q_RANK + MLA_KV_RANK], kvnw_ref[...])
    ckv_ref[...] = ckv
    ckvb_ref[...] = ckv.astype(BF16)
    kp = acc[:, MLA_Q_RANK + MLA_KV_RANK:]
    kpe_ref[...] = kp
    kper_ref[...] = _rope128(kp, cos_ref[...], sin_ref[...]).astype(BF16)


def _rope_spec(tm, n_ctx_tok, dec_seq):
    per_seq = dec_seq // tm

    def index(i, *_):
        tok = i * tm
        return (jnp.where(tok < n_ctx_tok, 0, 1 + ((tok - n_ctx_tok) // tm) % per_seq), 0)
    return pl.BlockSpec((tm, LANES), index)


def _lowrank(h, w_low, qnw, kvnw, cos_t, sin_t, n_ctx_tok, dec_seq):
    t = h.shape[0]
    tm = TM_NORM
    n = w_low.shape[1]
    row = lambda i: (i, 0)
    const = lambda i: (0, 0)
    return pl.pallas_call(
        _lowrank_kernel,
        grid=(t // tm,),
        in_specs=[pl.BlockSpec((tm, D_MODEL), row),
                  pl.BlockSpec((D_MODEL, n), const),
                  pl.BlockSpec((1, MLA_Q_RANK), const),
                  pl.BlockSpec((1, MLA_KV_RANK), const),
                  _rope_spec(tm, n_ctx_tok, dec_seq),
                  _rope_spec(tm, n_ctx_tok, dec_seq)],
        out_specs=[pl.BlockSpec((tm, MLA_Q_RANK), row),
                   pl.BlockSpec((tm, MLA_KV_RANK), row),
                   pl.BlockSpec((tm, MLA_KV_RANK), row),
                   pl.BlockSpec((tm, LANES), row),
                   pl.BlockSpec((tm, LANES), row)],
        out_shape=[jax.ShapeDtypeStruct((t, MLA_Q_RANK), BF16),
                   jax.ShapeDtypeStruct((t, MLA_KV_RANK), F32),
                   jax.ShapeDtypeStruct((t, MLA_KV_RANK), BF16),
                   jax.ShapeDtypeStruct((t, LANES), F32),
                   jax.ShapeDtypeStruct((t, LANES), BF16)],
        compiler_params=_cparams(("parallel",), 40),
        name="lowrank_proj",
    )(h, w_low, qnw, kvnw, cos_t, sin_t)


Q_PRESCALE = (MLA_NOPE + MLA_ROPE) ** -0.5 * 1.4426950408889634


def _qb_kernel(qn_ref, w_ref, cos_ref, sin_ref, o_ref):
    qn = qn_ref[...]
    cos = cos_ref[...]
    sin = sin_ref[...]
    for h in range(MLA_HEADS):
        lo = h * Q_HEAD_W
        acc = jnp.dot(qn, w_ref[:, lo:lo + Q_HEAD_W], preferred_element_type=F32) * Q_PRESCALE
        o_ref[:, lo:lo + MLA_NOPE] = acc[:, :MLA_NOPE].astype(BF16)
        o_ref[:, lo + MLA_NOPE:lo + Q_HEAD_W] = _rope128(acc[:, MLA_NOPE:], cos, sin).astype(BF16)


def _q_up(qn, w_qb, cos_t, sin_t, n_ctx_tok, dec_seq):
    t = qn.shape[0]
    tm = TM_NORM
    n = w_qb.shape[1]
    return pl.pallas_call(
        _qb_kernel,
        grid=(t // tm,),
        in_specs=[pl.BlockSpec((tm, MLA_Q_RANK), lambda i: (i, 0)),
                  pl.BlockSpec((MLA_Q_RANK, n), lambda i: (0, 0)),
                  _rope_spec(tm, n_ctx_tok, dec_seq),
                  _rope_spec(tm, n_ctx_tok, dec_seq)],
        out_specs=pl.BlockSpec((tm, n), lambda i: (i, 0)),
        out_shape=jax.ShapeDtypeStruct((t, n), BF16),
        compiler_params=_cparams(("parallel",), 40),
        name="q_up_rope",
    )(qn, w_qb, cos_t, sin_t)


def _ret_kernel(logit_ref, q_ref, k_ref, v_ref, g_ref, gnw_ref, *rest, n_chunks, has_s0, emit_state):
    rest = list(rest)
    s0_ref = rest.pop(0) if has_s0 else None
    o_ref = rest.pop(0)
    sfin_ref = rest.pop(0) if emit_state else None
    sf_scr, sb_scr = rest
    c_len = RET_CHUNK
    head = pl.program_id(1)

    def log_gamma(direction):
        l = jnp.full((c_len, 1), logit_ref[direction, head], F32)
        return jnp.minimum(l, 0.0) - jnp.log1p(jnp.exp(-jnp.abs(l)))

    lgf_c = log_gamma(0)
    lgb_c = log_gamma(1)
    ii = lax.broadcasted_iota(I32, (c_len, c_len), 0).astype(F32)
    jj = lax.broadcasted_iota(I32, (c_len, c_len), 1).astype(F32)
    k_scale = RET_DK ** -0.5
    decay = (jnp.where(ii >= jj, jnp.exp(jnp.maximum(ii - jj, 0.0) * lgf_c), 0.0)
             + jnp.where(jj >= ii, jnp.exp(jnp.maximum(jj - ii, 0.0) * lgb_c), 0.0)) * k_scale
    pos = lax.broadcasted_iota(I32, (c_len, 1), 0).astype(F32)
    qdec_f = jnp.exp((pos + 1.0) * lgf_c)
    qdec_b = jnp.exp((c_len - pos) * lgb_c)
    kdec_f = jnp.exp((c_len - 1.0 - pos) * lgf_c) * k_scale
    kdec_b = jnp.exp(pos * lgb_c) * k_scale
    cdec_f = jnp.exp(c_len * lgf_c)
    cdec_b = jnp.exp(c_len * lgb_c)
    tn_dims = (((0,), (0,)), ((), ()))
    nt_dims = (((1,), (1,)), ((), ()))

    def chunk(c):
        return pl.ds(c * c_len, c_len)

    for c in range(n_chunks):
        kc = k_ref[chunk(c), :].astype(F32)
        vc = v_ref[chunk(c), :]
        sf_scr[c] = lax.dot_general((kc * kdec_f).astype(BF16), vc, tn_dims, preferred_element_type=F32)
        sb_scr[c] = lax.dot_general((kc * kdec_b).astype(BF16), vc, tn_dims, preferred_element_type=F32)

    s = s0_ref[0] if has_s0 else jnp.zeros((RET_DK, RET_DV), F32)
    for c in range(n_chunks):
        kv = sf_scr[c]
        sf_scr[c] = s
        s = cdec_f * s + kv
    if emit_state:
        sfin_ref[0] = s
    s = s0_ref[1] if has_s0 else jnp.zeros((RET_DK, RET_DV), F32)
    for c in reversed(range(n_chunks)):
        kv = sb_scr[c]
        sb_scr[c] = s
        s = cdec_b * s + kv
    if emit_state:
        sfin_ref[1] = s

    gnw = gnw_ref[...]
    for c in range(n_chunks):
        qc = q_ref[chunk(c), :]
        kc = k_ref[chunk(c), :]
        vc = v_ref[chunk(c), :]
        sc = lax.dot_general(qc, kc, nt_dims, preferred_element_type=F32) * decay
        qf = qc.astype(F32)
        qcat = jnp.concatenate([(qf * qdec_f).astype(BF16), (qf * qdec_b).astype(BF16)], axis=1)
        scat = jnp.concatenate([sf_scr[c], sb_scr[c]], axis=0).astype(BF16)
        o = (jnp.dot(sc.astype(BF16), vc, preferred_element_type=F32)
             + jnp.dot(qcat, scat, preferred_element_type=F32))
        mu = jnp.mean(o, axis=-1, keepdims=True)
        var = jnp.mean(jnp.square(o - mu), axis=-1, keepdims=True)
        on = (o - mu) * lax.rsqrt(var + NORM_EPS) * gnw
        o_ref[chunk(c), :] = (_silu(g_ref[chunk(c), :].astype(F32)) * on).astype(BF16)


def _retention(qkvg, logits, gn_w, s0, *, n_seq, seq, row_block0, emit_state):
    n_chunks = seq // RET_CHUNK
    has_s0 = s0 is not None
    k_off = RET_QK_W // RET_DK
    v_off = 2 * RET_QK_W // RET_DV
    g_off = (2 * RET_QK_W + RET_V_W) // RET_DV
    in_specs = [pl.BlockSpec(memory_space=pltpu.SMEM),
                pl.BlockSpec((seq, RET_DK), lambda b, h: (row_block0 + b, h)),
                pl.BlockSpec((seq, RET_DK), lambda b, h: (row_block0 + b, k_off + h)),
                pl.BlockSpec((seq, RET_DV), lambda b, h: (row_block0 + b, v_off + h)),
                pl.BlockSpec((seq, RET_DV), lambda b, h: (row_block0 + b, g_off + h)),
                pl.BlockSpec((1, RET_DV), lambda b, h: (0, h))]
    args = [logits, qkvg, qkvg, qkvg, qkvg, gn_w]
    state_spec = pl.BlockSpec((None, 2, None, RET_DK, RET_DV), lambda b, h: (b, 0, h, 0, 0))
    if has_s0:
        in_specs.append(state_spec)
        args.append(s0)
    out_specs = [pl.BlockSpec((seq, RET_DV), lambda b, h: (b, h))]
    out_shape = [jax.ShapeDtypeStruct((n_seq * seq, RET_V_W), BF16)]
    if emit_state:
        out_specs.append(state_spec)
        out_shape.append(jax.ShapeDtypeStruct((n_seq, 2, RET_HEADS, RET_DK, RET_DV), F32))
    res = pl.pallas_call(
        functools.partial(_ret_kernel, n_chunks=n_chunks, has_s0=has_s0, emit_state=emit_state),
        grid=(n_seq, RET_HEADS),
        in_specs=in_specs,
        out_specs=out_specs,
        out_shape=out_shape,
        scratch_shapes=[pltpu.VMEM((n_chunks, RET_DK, RET_DV), F32),
                        pltpu.VMEM((n_chunks, RET_DK, RET_DV), F32)],
        compiler_params=_cparams(("parallel", "parallel")),
        name="retention_seq%d" % seq,
    )(*args)
    return res


V_EXT_W = MLA_V + LANES


def _ones_column(rows):
    lane = lax.broadcasted_iota(I32, (rows, LANES), 1)
    return jnp.where(lane == 0, 1.0, 0.0).astype(BF16)


def _softmax_pv(q, kcat, v_ext):
    s = lax.dot_general(q, kcat, (((1,), (1,)), ((), ())), preferred_element_type=F32)
    e = jnp.exp2(s - jnp.max(s, axis=-1, keepdims=True))
    o = jnp.dot(e.astype(BF16), v_ext, preferred_element_type=F32)
    return o[:, :MLA_V] / o[:, MLA_V:MLA_V + 1]


def _attn_ctx_kernel(q_ref, kv_ref, kpe_ref, o_ref):
    kpe = kpe_ref[...]
    ones = _ones_column(kpe.shape[0])
    kvw = MLA_NOPE + MLA_V
    for h in range(MLA_HEADS):
        kv = kv_ref[:, h * kvw:(h + 1) * kvw]
        kcat = jnp.concatenate([kv[:, :MLA_NOPE], kpe], axis=1)
        v_ext = jnp.concatenate([kv[:, MLA_NOPE:], ones], axis=1)
        o = _softmax_pv(q_ref[:, h * Q_HEAD_W:(h + 1) * Q_HEAD_W], kcat, v_ext)
        o_ref[:, h * MLA_V:(h + 1) * MLA_V] = o.astype(o_ref.dtype)


def _attention_ctx(q_all, kv_all, kpe_all, n_seq, seq):
    return pl.pallas_call(
        _attn_ctx_kernel,
        grid=(n_seq,),
        in_specs=[pl.BlockSpec((seq, MLA_HEADS * Q_HEAD_W), lambda b: (b, 0)),
                  pl.BlockSpec((seq, MLA_HEADS * (MLA_NOPE + MLA_V)), lambda b: (b, 0)),
                  pl.BlockSpec((seq, LANES), lambda b: (b, 0))],
        out_specs=pl.BlockSpec((seq, MLA_HEADS * MLA_V), lambda b: (b, 0)),
        out_shape=jax.ShapeDtypeStruct((n_seq * seq, MLA_HEADS * MLA_V), BF16),
        compiler_params=_cparams(("parallel",), 40),
        name="attention_ctx",
    )(q_all, kv_all, kpe_all)


def _attn_lat_kernel(q_ref, kvc_ref, kvl_ref, kpc_ref, kpl_ref, o_ref, kcat, vcat, *, past, seq):
    @pl.when(pl.program_id(2) == 0)
    def _():
        kcat[0:past, 0:MLA_NOPE] = kvc_ref[:, :MLA_NOPE]
        kcat[0:past, MLA_NOPE:] = kpc_ref[...]
        kcat[past:past + seq, 0:MLA_NOPE] = kvl_ref[:, :MLA_NOPE]
        kcat[past:past + seq, MLA_NOPE:] = kpl_ref[...]
        vcat[0:past, 0:MLA_V] = kvc_ref[:, MLA_NOPE:]
        vcat[past:past + seq, 0:MLA_V] = kvl_ref[:, MLA_NOPE:]
        vcat[:, MLA_V:] = _ones_column(past + seq)

    o_ref[...] = _softmax_pv(q_ref[...], kcat[...], vcat[...]).astype(o_ref.dtype)


def _attention_lat(q_all, kv_ctx, kv_all, kpe_ctx, kpe_all, n_seq, seq, past, n_ctx_tok):
    tq = 512
    nq = seq // tq
    q0 = n_ctx_tok // tq
    s0 = n_ctx_tok // seq
    kvw = MLA_NOPE + MLA_V
    return pl.pallas_call(
        functools.partial(_attn_lat_kernel, past=past, seq=seq),
        grid=(n_seq, MLA_HEADS, nq),
        in_specs=[pl.BlockSpec((tq, Q_HEAD_W), lambda b, h, i: (q0 + b * nq + i, h)),
                  pl.BlockSpec((past, kvw), lambda b, h, i: (b, h)),
                  pl.BlockSpec((seq, kvw), lambda b, h, i: (s0 + b, h)),
                  pl.BlockSpec((past, LANES), lambda b, h, i: (b, 0)),
                  pl.BlockSpec((seq, LANES), lambda b, h, i: (s0 + b, 0))],
        out_specs=pl.BlockSpec((tq, MLA_V), lambda b, h, i: (b * nq + i, h)),
        out_shape=jax.ShapeDtypeStruct((n_seq * seq, MLA_HEADS * MLA_V), BF16),
        scratch_shapes=[pltpu.VMEM((past + seq, MLA_NOPE + LANES), BF16),
                        pltpu.VMEM((past + seq, V_EXT_W), BF16)],
        compiler_params=_cparams(("parallel", "parallel", "arbitrary"), 40),
        name="attention_lat",
    )(q_all, kv_ctx, kv_all, kpe_ctx, kpe_all)


def _merge_kernel(rc_ref, rl_ref, ac_ref, al_ref, wr_ref, wm_ref, gr_ref, gm_ref, o_ref, *, n_ctx_tok):
    tile = pl.program_id(1)
    tm = o_ref.shape[0]
    r = _path_pick(tile, tm, n_ctx_tok, rc_ref, rl_ref)
    a = _path_pick(tile, tm, n_ctx_tok, ac_ref, al_ref)
    ro = jnp.dot(r, wr_ref[...], preferred_element_type=F32)
    mo = jnp.dot(a, wm_ref[...], preferred_element_type=F32)
    o_ref[...] = (gr_ref[...].astype(F32) * ro + gm_ref[...].astype(F32) * mo).astype(BF16)


def _merge(r_ctx, r_lat, a_ctx, a_lat, w_ret_o, w_mla_o, gates, n_ctx_tok):
    t = gates.shape[0]
    tm, tn = 512, 1024
    nj = D_MODEL // tn
    return pl.pallas_call(
        functools.partial(_merge_kernel, n_ctx_tok=n_ctx_tok),
        grid=(nj, t // tm),
        in_specs=[*_path_specs(tm, RET_V_W, n_ctx_tok, axis=1),
                  *_path_specs(tm, MLA_HEADS * MLA_V, n_ctx_tok, axis=1),
                  pl.BlockSpec((RET_V_W, tn), lambda j, i: (0, j)),
                  pl.BlockSpec((MLA_HEADS * MLA_V, tn), lambda j, i: (0, j)),
                  pl.BlockSpec((tm, tn), lambda j, i: (i, j)),
                  pl.BlockSpec((tm, tn), lambda j, i: (i, nj + j))],
        out_specs=pl.BlockSpec((tm, tn), lambda j, i: (i, j)),
        out_shape=jax.ShapeDtypeStruct((t, D_MODEL), BF16),
        compiler_params=_cparams(("parallel", "parallel"), 48),
        name="branch_merge",
    )(r_ctx, r_lat, a_ctx, a_lat, w_ret_o, w_mla_o, gates, gates)


def _outproj_kernel(m_ref, w_ref, xc_ref, xl_ref, g1_ref, nw_ref, sc_ref, sh_ref,
                    x1_ref, hhi_ref, hlo_ref, hp_ref, *, n_ctx_tok):
    mix = jnp.dot(m_ref[...], w_ref[...], preferred_element_type=F32)
    x = _path_pick(pl.program_id(0), x1_ref.shape[0], n_ctx_tok, xc_ref, xl_ref)
    x1 = x + g1_ref[...] * mix
    x1_ref[...] = x1
    h2 = _rms(x1, nw_ref[...]) * (1.0 + sc_ref[...]) + sh_ref[...]
    hi = h2.astype(BF16)
    hif = hi.astype(F32)
    hhi_ref[...] = hi
    hlo_ref[...] = (h2 - hif).astype(BF16)
    _store_row_tiles(hp_ref, hif)


def _out_proj(merged, w_out, x_ctx, x_lat, norm_w, mod4, n_ctx_tok, dec_seq):
    t = merged.shape[0]
    tm = TM_NORM
    row = lambda i: (i, 0)
    const = lambda i: (0, 0)
    return pl.pallas_call(
        functools.partial(_outproj_kernel, n_ctx_tok=n_ctx_tok),
        grid=(t // tm,),
        in_specs=[pl.BlockSpec((tm, D_MODEL), row),
                  pl.BlockSpec((D_MODEL, D_MODEL), const),
                  *_path_specs(tm, D_MODEL, n_ctx_tok),
                  _mod_spec(2, tm, n_ctx_tok, dec_seq),
                  pl.BlockSpec((1, D_MODEL), const),
                  _mod_spec(4, tm, n_ctx_tok, dec_seq),
                  _mod_spec(3, tm, n_ctx_tok, dec_seq)],
        out_specs=[pl.BlockSpec((tm, D_MODEL), row),
                   pl.BlockSpec((tm, D_MODEL), row),
                   pl.BlockSpec((tm, D_MODEL), row),
                   pl.BlockSpec((tm * ROW_TILES, LANES), row)],
        out_shape=[jax.ShapeDtypeStruct((t, D_MODEL), F32),
                   jax.ShapeDtypeStruct((t, D_MODEL), BF16),
                   jax.ShapeDtypeStruct((t, D_MODEL), BF16),
                   jax.ShapeDtypeStruct((t * ROW_TILES, LANES), U32)],
        compiler_params=_cparams(("parallel",), 48),
        name="out_proj_norm",
    )(merged, w_out, x_ctx, x_lat, mod4, norm_w, mod4, mod4)


def _router_kernel(hhi_ref, hlo_ref, wrt_ref, bias_ref, eidx_ref, wts_ref, rank_ref, cnt_ref, carry):
    tm = hhi_ref.shape[0]
    n_e = N_EXPERTS

    @pl.when(pl.program_id(0) == 0)
    def _():
        carry[...] = jnp.zeros_like(carry)

    wr = wrt_ref[...]
    wr_hi = wr.astype(BF16)
    wr_lo = (wr - wr_hi.astype(F32)).astype(BF16)
    hh = hhi_ref[...]
    hl = hlo_ref[...]
    nt_dims = (((1,), (1,)), ((), ()))
    logits = (lax.dot_general(wr_hi, hh, nt_dims, preferred_element_type=F32)
              + lax.dot_general(wr_hi, hl, nt_dims, preferred_element_type=F32)
              + lax.dot_general(wr_lo, hh, nt_dims, preferred_element_type=F32))
    scores = _sigmoid(logits)
    choice = scores + bias_ref[...]
    neg = -jnp.inf
    row = lax.broadcasted_iota(I32, (n_e, tm), 0).astype(F32)
    grow = lax.broadcasted_iota(I32, (GROUP_SIZE, tm), 0).astype(F32)

    gs = []
    for g in range(N_GROUPS):
        blk = choice[g * GROUP_SIZE:(g + 1) * GROUP_SIZE]
        m1 = jnp.max(blk, axis=0, keepdims=True)
        i1 = jnp.min(jnp.where(blk == m1, grow, float(GROUP_SIZE)), axis=0, keepdims=True)
        m2 = jnp.max(jnp.where(grow == i1, neg, blk), axis=0, keepdims=True)
        gs.append(m1 + m2)
    gsc = jnp.concatenate(gs, axis=0)
    gid = lax.broadcasted_iota(I32, (N_GROUPS, tm), 0).astype(F32)
    gsel = jnp.zeros((N_GROUPS, tm), F32)
    for _ in range(TOPK_GROUPS):
        m = jnp.max(gsc, axis=0, keepdims=True)
        idx = jnp.min(jnp.where(gsc == m, gid, float(N_GROUPS)), axis=0, keepdims=True)
        hit = gid == idx
        gsel = jnp.where(hit, 1.0, gsel)
        gsc = jnp.where(hit, neg, gsc)
    masked = jnp.concatenate(
        [jnp.where(gsel[g:g + 1] > 0.0, choice[g * GROUP_SIZE:(g + 1) * GROUP_SIZE], neg)
         for g in range(N_GROUPS)], axis=0)

    idxs, ws = [], []
    onehot = jnp.zeros((n_e, tm), F32)
    for _ in range(TOP_K):
        m = jnp.max(masked, axis=0, keepdims=True)
        idx = jnp.min(jnp.where(masked == m, row, float(n_e)), axis=0, keepdims=True)
        hit = row == idx
        ws.append(jnp.sum(jnp.where(hit, scores, 0.0), axis=0, keepdims=True))
        masked = jnp.where(hit, neg, masked)
        onehot = jnp.where(hit, 1.0, onehot)
        idxs.append(idx)
    w = jnp.concatenate(ws, axis=0)
    wts_ref[...] = w / jnp.sum(w, axis=0, keepdims=True) * ROUTED_SCALE
    eidx_ref[...] = jnp.concatenate(idxs, axis=0).astype(I32)

    t_src = lax.broadcasted_iota(I32, (tm, tm), 0)
    t_dst = lax.broadcasted_iota(I32, (tm, tm), 1)
    before = jnp.where(t_src < t_dst, 1.0, 0.0).astype(BF16)
    prior = jnp.dot(onehot.astype(BF16), before, preferred_element_type=F32) + carry[:, :1]
    rank_ref[...] = jnp.concatenate(
        [jnp.sum(jnp.where(row == idx, prior, 0.0), axis=0, keepdims=True) for idx in idxs],
        axis=0).astype(I32)
    carry[...] = carry[...] + jnp.sum(onehot, axis=1, keepdims=True)
    cnt_ref[...] = carry[...]


def _router(h_hi, h_lo, wr_t, bias_col):
    t = h_hi.shape[0]
    tm = 512
    tok = lambda i: (0, i)
    const = lambda i: (0, 0)
    return pl.pallas_call(
        _router_kernel,
        grid=(t // tm,),
        in_specs=[pl.BlockSpec((tm, D_MODEL), lambda i: (i, 0)),
                  pl.BlockSpec((tm, D_MODEL), lambda i: (i, 0)),
                  pl.BlockSpec((N_EXPERTS, D_MODEL), const),
                  pl.BlockSpec((N_EXPERTS, 1), const)],
        out_specs=[pl.BlockSpec((TOP_K, tm), tok),
                   pl.BlockSpec((TOP_K, tm), tok),
                   pl.BlockSpec((TOP_K, tm), tok),
                   pl.BlockSpec((N_EXPERTS, LANES), const)],
        out_shape=[jax.ShapeDtypeStruct((TOP_K, t), I32),
                   jax.ShapeDtypeStruct((TOP_K, t), F32),
                   jax.ShapeDtypeStruct((TOP_K, t), I32),
                   jax.ShapeDtypeStruct((N_EXPERTS, LANES), F32)],
        scratch_shapes=[pltpu.VMEM((N_EXPERTS, LANES), F32)],
        compiler_params=_cparams(("arbitrary",), 40),
        name="router_topk",
    )(h_hi, h_lo, wr_t, bias_col)


def _dest_kernel(eidx_ref, rank_ref, pstart_ref, dest_ref):
    tm = eidx_ref.shape[1]
    row = lax.broadcasted_iota(I32, (N_EXPERTS, tm), 0)
    pstart = pstart_ref[...]
    eidx = eidx_ref[...]
    base = jnp.concatenate(
        [jnp.sum(jnp.where(row == eidx[k:k + 1], pstart, 0.0), axis=0, keepdims=True)
         for k in range(TOP_K)], axis=0)
    dest_ref[...] = (base.astype(I32) + rank_ref[...]) * ROW_TILES


def _dest_rows(eidx, rank, pstart_col):
    t = eidx.shape[1]
    tm = 1024
    tok = lambda i: (0, i)
    return pl.pallas_call(
        _dest_kernel,
        grid=(t // tm,),
        in_specs=[pl.BlockSpec((TOP_K, tm), tok),
                  pl.BlockSpec((TOP_K, tm), tok),
                  pl.BlockSpec((N_EXPERTS, 1), lambda i: (0, 0))],
        out_specs=pl.BlockSpec((TOP_K, tm), tok),
        out_shape=jax.ShapeDtypeStruct((TOP_K, t), I32),
        compiler_params=_cparams(("parallel",)),
        name="dest_rows",
    )(eidx, rank, pstart_col)


def _dispatch_kernel(fill_ref, meta_ref, dest_ref, hp_ref, xs_ref, zbuf, sem, fill_sem):
    tm = hp_ref.shape[0]
    block_rows = zbuf.shape[0]
    n_blocks = xs_ref.shape[0] // block_rows

    def zero_fills(start):
        def run(cp):
            cp.start() if start else cp.wait()

        def per_expert(e, c):
            row = fill_ref[0, e]
            left = fill_ref[1, e]
            p = EXPERT_BM // 2
            while p >= 1:
                take = left & p

                @pl.when(take != 0)
                def _(row=row, p=p):
                    dst = xs_ref.at[pl.ds(pl.multiple_of(row * ROW_TILES, ROW_TILES), p * ROW_TILES)]
                    run(pltpu.make_async_copy(zbuf.at[pl.ds(0, p * ROW_TILES)], dst, fill_sem))

                row = row + take
                p //= 2
            return c

        def per_tail_block(b, c):
            dst = xs_ref.at[pl.ds(pl.multiple_of(b * block_rows, block_rows), block_rows)]
            run(pltpu.make_async_copy(zbuf, dst, fill_sem))
            return c

        lax.fori_loop(0, N_EXPERTS, per_expert, 0)
        lax.fori_loop(meta_ref[0], n_blocks, per_tail_block, 0)

    @pl.when(pl.program_id(0) == 0)
    def _():
        zbuf[...] = jnp.zeros_like(zbuf)
        zero_fills(True)

    def issue(t, c):
        src = hp_ref.at[pl.ds(pl.multiple_of(t * ROW_TILES, ROW_TILES), ROW_TILES)]
        for k in range(TOP_K):
            r = pl.multiple_of(dest_ref[t * TOP_K + k], ROW_TILES)
            pltpu.make_async_copy(src, xs_ref.at[pl.ds(r, ROW_TILES)], sem).start(priority=k % 2)
        return c

    lax.fori_loop(0, tm // ROW_TILES, issue, 0)
    for _ in range(TOP_K):
        pltpu.make_async_copy(hp_ref, xs_ref.at[pl.ds(0, tm)], sem).wait()

    @pl.when(pl.program_id(0) == 0)
    def _():
        zero_fills(False)


def _dispatch(fill, meta, dest, hp, n_blocks):
    tm = TM_NORM * ROW_TILES
    block_rows = EXPERT_BM * ROW_TILES
    return pl.pallas_call(
        _dispatch_kernel,
        grid=(hp.shape[0] // tm,),
        in_specs=[pl.BlockSpec(memory_space=pltpu.SMEM),
                  pl.BlockSpec(memory_space=pltpu.SMEM),
                  pl.BlockSpec((TM_NORM * TOP_K,), lambda i: (i,), memory_space=pltpu.SMEM),
                  pl.BlockSpec((tm, LANES), lambda i: (i, 0))],
        out_specs=pl.BlockSpec(memory_space=pl.ANY),
        out_shape=jax.ShapeDtypeStruct((n_blocks * block_rows, LANES), U32),
        scratch_shapes=[pltpu.VMEM((block_rows, LANES), U32),
                        pltpu.SemaphoreType.DMA(()),
                        pltpu.SemaphoreType.DMA(())],
        compiler_params=_cparams(("arbitrary",)),
        name="moe_dispatch",
    )(fill, meta, dest, hp)


N_WSLOTS = 2


def _expert_kernel(uidx_ref, ue_ref, meta_ref, xs_ref, wi_hbm, wo_hbm, ys_ref,
                   wi_st, wo_st, wi_s, wo_s, sem):
    b = pl.program_id(0)
    n_used = meta_ref[0]
    n_ue = meta_ref[1]

    def weight_copies(j, slot):
        e = ue_ref[j]
        return (pltpu.make_async_copy(wi_hbm.at[e], wi_st.at[slot], sem.at[0, slot]),
                pltpu.make_async_copy(wo_hbm.at[e], wo_st.at[slot], sem.at[1, slot]))

    @pl.when(b < n_used)
    def _():
        j = uidx_ref[b]

        @pl.when(b == 0)
        def _():
            for cp in weight_copies(0, 0):
                cp.start()

            @pl.when(n_ue > 1)
            def _():
                for cp in weight_copies(1, 1):
                    cp.start()

        @pl.when((b == 0) | (j != uidx_ref[jnp.maximum(b - 1, 0)]))
        def _():
            slot = j % N_WSLOTS
            for cp in weight_copies(j, slot):
                cp.wait()
            rows = 256

            def cast_in(i, c):
                r = pl.multiple_of(i * rows, rows)
                wi_s[pl.ds(r, rows), :] = wi_st[slot, pl.ds(r, rows), :].astype(BF16)
                return c

            def cast_out(i, c):
                r = pl.multiple_of(i * rows, rows)
                wo_s[pl.ds(r, rows), :] = wo_st[slot, pl.ds(r, rows), :].astype(BF16)
                return c

            lax.fori_loop(0, D_MODEL // rows, cast_in, 0)
            lax.fori_loop(0, D_EXPERT // rows, cast_out, 0)

            @pl.when(j + N_WSLOTS < n_ue)
            def _():
                for cp in weight_copies(j + N_WSLOTS, slot):
                    cp.start()

        bm = EXPERT_BM
        x = jnp.concatenate([p.astype(BF16) for p in _load_row_tiles(xs_ref, bm)], axis=1)
        gu = jnp.dot(x, wi_s[...], preferred_element_type=F32)
        act = (_silu(gu[:, :D_EXPERT]) * gu[:, D_EXPERT:]).astype(BF16)
        _store_row_tiles(ys_ref, jnp.dot(act, wo_s[...], preferred_element_type=F32))


def _experts(uidx, ue, meta, xs, w_exp_in, w_exp_out):
    rows = EXPERT_BM * ROW_TILES
    nb = xs.shape[0] // rows
    live = lambda b, uidx, ue, meta: (jnp.minimum(b, meta[0] - 1), 0)
    grid_spec = pltpu.PrefetchScalarGridSpec(
        num_scalar_prefetch=3,
        grid=(nb,),
        in_specs=[pl.BlockSpec((rows, LANES), live),
                  pl.BlockSpec(memory_space=pl.ANY),
                  pl.BlockSpec(memory_space=pl.ANY)],
        out_specs=pl.BlockSpec((rows, LANES), live),
        scratch_shapes=[pltpu.VMEM((N_WSLOTS, D_MODEL, 2 * D_EXPERT), F32),
                        pltpu.VMEM((N_WSLOTS, D_EXPERT, D_MODEL), F32),
                        pltpu.VMEM((D_MODEL, 2 * D_EXPERT), BF16),
                        pltpu.VMEM((D_EXPERT, D_MODEL), BF16),
                        pltpu.SemaphoreType.DMA((2, N_WSLOTS))],
    )
    return pl.pallas_call(
        _expert_kernel,
        grid_spec=grid_spec,
        out_shape=jax.ShapeDtypeStruct(xs.shape, U32),
        input_output_aliases={3: 0},
        compiler_params=_cparams(("arbitrary",), 52),
        name="routed_experts",
    )(uidx, ue, meta, xs, w_exp_in, w_exp_out)


def _combine_kernel(dest_ref, dnext_ref, wt_ref, x1_ref, hhi_ref, g2_ref, wsi_ref, wso_ref, nf_ref, ys_ref,
                    yc_ref, yl_ref, rows, sem, *, n_ctx_tok):
    tm = x1_ref.shape[0]
    i = pl.program_id(0)
    n = pl.num_programs(0)

    def gather(idx_ref, slot):
        def issue(t, c):
            for k in range(TOP_K):
                r = pl.multiple_of(idx_ref[t * TOP_K + k], ROW_TILES)
                dst = rows.at[slot, k, pl.ds(pl.multiple_of(t * ROW_TILES, ROW_TILES), ROW_TILES)]
                pltpu.make_async_copy(ys_ref.at[pl.ds(r, ROW_TILES)], dst, sem.at[slot]).start(priority=k % 2)
            return c

        lax.fori_loop(0, tm, issue, 0)

    @pl.when(i == 0)
    def _():
        gather(dest_ref, 0)

    @pl.when(i + 1 < n)
    def _():
        gather(dnext_ref, (i + 1) % 2)

    su = jnp.dot(hhi_ref[...], wsi_ref[...], preferred_element_type=F32)
    sact = (_silu(su[:, :D_SHARED]) * su[:, D_SHARED:]).astype(BF16)
    shared = jnp.dot(sact, wso_ref[...], preferred_element_type=F32)

    slot = i % 2
    for k in range(TOP_K):
        pltpu.make_async_copy(ys_ref.at[pl.ds(0, tm * ROW_TILES)], rows.at[slot, k], sem.at[slot]).wait()

    wt = wt_ref[...]
    acc = None
    for k in range(TOP_K):
        wk = wt[:, k:k + 1]
        parts = [wk * p for p in _load_row_tiles(rows.at[slot, k], tm)]
        acc = parts if acc is None else [a + p for a, p in zip(acc, parts)]
    moe = jnp.concatenate(acc, axis=1) + shared
    x2 = x1_ref[...] + g2_ref[...] * moe
    y = _rms(x2, nf_ref[...])
    is_ctx = i * tm < n_ctx_tok

    @pl.when(is_ctx)
    def _():
        yc_ref[...] = y

    @pl.when(jnp.logical_not(is_ctx))
    def _():
        yl_ref[...] = y


def _combine(dest, wts_t, x1, h_hi, mod4, w_sh_in, w_sh_out, norm_final, ys, n_ctx_tok, dec_seq):
    t = x1.shape[0]
    tm = TM_NORM
    n = t // tm
    row = lambda i: (i, 0)
    const = lambda i: (0, 0)
    return pl.pallas_call(
        functools.partial(_combine_kernel, n_ctx_tok=n_ctx_tok),
        grid=(n,),
        in_specs=[pl.BlockSpec((tm * TOP_K,), lambda i: (i,), memory_space=pltpu.SMEM),
                  pl.BlockSpec((tm * TOP_K,), lambda i: (jnp.minimum(i + 1, n - 1),), memory_space=pltpu.SMEM),
                  pl.BlockSpec((tm, TOP_K), row),
                  pl.BlockSpec((tm, D_MODEL), row),
                  pl.BlockSpec((tm, D_MODEL), row),
                  _mod_spec(5, tm, n_ctx_tok, dec_seq),
                  pl.BlockSpec((D_MODEL, 2 * D_SHARED), const),
                  pl.BlockSpec((D_SHARED, D_MODEL), const),
                  pl.BlockSpec((1, D_MODEL), const),
                  pl.BlockSpec(memory_space=pl.ANY)],
        out_specs=list(_path_specs(tm, D_MODEL, n_ctx_tok)),
        out_shape=[jax.ShapeDtypeStruct((n_ctx_tok, D_MODEL), F32),
                   jax.ShapeDtypeStruct((t - n_ctx_tok, D_MODEL), F32)],
        scratch_shapes=[pltpu.VMEM((2, TOP_K, tm * ROW_TILES, LANES), U32), pltpu.SemaphoreType.DMA((2,))],
        compiler_params=_cparams(("arbitrary",), 56),
        name="moe_combine",
    )(dest, dest, wts_t, x1, h_hi, mod4, w_sh_in, w_sh_out, norm_final, ys)


def _rope_tables(tm, dec_seq):
    rows = dec_seq // GRID_W
    row = jnp.broadcast_to(jnp.arange(rows, dtype=F32)[:, None], (rows, GRID_W)).reshape(-1)
    col = jnp.broadcast_to(jnp.arange(GRID_W, dtype=F32)[None, :], (rows, GRID_W)).reshape(-1)
    axis_dim = MLA_ROPE // 2
    freqs = jnp.power(ROPE_BASE, -jnp.arange(0, axis_dim, 2, dtype=F32) / axis_dim)
    ang_r = row[:, None] * freqs
    ang_c = col[:, None] * freqs
    ang = jnp.concatenate([ang_r, ang_r, ang_c, ang_c], axis=-1)
    pad = LANES - MLA_ROPE
    cos = jnp.concatenate([jnp.cos(ang), jnp.ones((dec_seq, pad), F32)], axis=1)
    sin = jnp.concatenate([jnp.sin(ang), jnp.zeros((dec_seq, pad), F32)], axis=1)
    cos = jnp.concatenate([jnp.ones((tm, LANES), F32), cos], axis=0)
    sin = jnp.concatenate([jnp.zeros((tm, LANES), F32), sin], axis=0)
    return cos, sin


def _block_plan(counts, n_blocks):
    bm = EXPERT_BM
    nblk = (counts + bm - 1) // bm
    bend = jnp.cumsum(nblk)
    bstart = bend - nblk
    present = nblk > 0
    pcum = jnp.cumsum(present.astype(I32))
    bidx = jnp.arange(n_blocks, dtype=I32)
    uidx = jnp.sum((present[None, :] & (bstart[None, :] <= bidx[:, None])).astype(I32), axis=1) - 1
    order = jnp.arange(N_EXPERTS, dtype=I32)
    ue = jnp.minimum(jnp.sum((pcum[None, :] <= order[:, None]).astype(I32), axis=1), N_EXPERTS - 1)
    meta = jnp.stack([bend[-1], pcum[-1]]).astype(I32)
    pstart = bstart * bm
    fill = jnp.stack([pstart + counts, nblk * bm - counts]).astype(I32)
    return pstart, uidx.astype(I32), ue.astype(I32), meta, fill


def kernel(x_prompt, x_sample, c, cache_mla_ckv, cache_mla_krope, state_ret, c_ctx, w_mod, b_mod, norm_mix, norm_ffn, w_in, ret_decay_logit, ret_gn_w, w_ret_o, mla_q_norm, w_q_b, mla_kv_norm, w_kv_b, w_mla_o, w_out, w_router, router_bias, w_exp_in, w_exp_out, w_shared_in, w_shared_out, norm_final):
    n_ctx, seq, d = x_prompt.shape
    n_lat, dec_seq, _ = x_sample.shape
    past = cache_mla_ckv.shape[2]
    n_ctx_tok = n_ctx * seq
    n_lat_tok = n_lat * dec_seq
    t = n_ctx_tok + n_lat_tok
    assert d == D_MODEL and w_mod.shape[0] == 1 and n_lat + 1 <= MOD_ROWS
    assert seq % TM_NORM == 0 and dec_seq % TM_NORM == 0 and n_ctx_tok % dec_seq == 0

    x_ctx = x_prompt.reshape(n_ctx_tok, d)
    x_lat = x_sample.reshape(n_lat_tok, d)

    cond = jnp.concatenate([c_ctx[None, :], c, jnp.zeros((MOD_ROWS - 1 - n_lat, d), F32)], axis=0)
    mod = _modulation(cond, w_mod.reshape(d, N_MOD * d), b_mod.reshape(1, N_MOD * d))
    mod4 = mod.reshape(MOD_ROWS, N_MOD, 1, d)

    w_in2 = w_in.reshape(d, -1)
    o_g = 2 * RET_QK_W + RET_V_W
    o_qa = o_g + RET_V_W
    o_kpe = o_qa + MLA_Q_RANK + MLA_KV_RANK
    o_gr = o_kpe + MLA_ROPE
    w_qkvg = w_in2[:, :o_qa].astype(BF16)
    w_low = jnp.pad(w_in2[:, o_qa:o_gr], ((0, 0), (0, LANES - MLA_ROPE))).astype(BF16)
    w_gates = w_in2[:, o_gr:].astype(BF16)
    wq3 = w_q_b.reshape(MLA_Q_RANK, MLA_HEADS, MLA_NOPE + MLA_ROPE)
    w_qb = jnp.pad(wq3, ((0, 0), (0, 0), (0, Q_HEAD_W - MLA_NOPE - MLA_ROPE))
                   ).reshape(MLA_Q_RANK, MLA_HEADS * Q_HEAD_W).astype(BF16)
    w_kvb = w_kv_b.reshape(MLA_KV_RANK, -1).astype(BF16)
    cos_t, sin_t = _rope_tables(TM_NORM, dec_seq)

    h1 = _norm_modulate(x_ctx, x_lat, norm_mix.reshape(1, d), mod4, n_ctx_tok, dec_seq)
    qkvg = _matmul(h1, w_qkvg, tm=1024, tn=1024, name="proj_qkvg")
    gates = _matmul(h1, w_gates, tm=1024, tn=1024, act="sigmoid", name="proj_gates")
    qn, ckv, ckv_b, kpe_raw, kpe_rot = _lowrank(
        h1, w_low, mla_q_norm.reshape(1, -1), mla_kv_norm.reshape(1, -1), cos_t, sin_t, n_ctx_tok, dec_seq)

    logits = ret_decay_logit.reshape(2, RET_HEADS)
    gn_w = ret_gn_w.reshape(1, RET_V_W)
    r_ctx, ret_state = _retention(qkvg, logits, gn_w, None, n_seq=n_ctx, seq=seq, row_block0=0,
                                  emit_state=True)
    (r_lat,) = _retention(qkvg, logits, gn_w, state_ret.reshape(n_lat, 2, RET_HEADS, RET_DK, RET_DV),
                          n_seq=n_lat, seq=dec_seq, row_block0=n_ctx_tok // dec_seq, emit_state=False)

    q_all = _q_up(qn, w_qb, cos_t, sin_t, n_ctx_tok, dec_seq)
    kv_all = _matmul(ckv_b, w_kvb, tm=1024, tn=1024, name="kv_up")
    kv_ctx = _matmul(cache_mla_ckv.reshape(n_lat * past, MLA_KV_RANK), w_kvb, tm=1024, tn=1024,
                     name="kv_up_cache")
    kpe_ctx = jnp.pad(cache_mla_krope.reshape(n_lat * past, MLA_ROPE),
                      ((0, 0), (0, LANES - MLA_ROPE))).astype(BF16)
    a_ctx = _attention_ctx(q_all, kv_all, kpe_rot, n_ctx, seq)
    a_lat = _attention_lat(q_all, kv_ctx, kv_all, kpe_ctx, kpe_rot, n_lat, dec_seq, past, n_ctx_tok)

    merged = _merge(r_ctx, r_lat, a_ctx, a_lat, w_ret_o.reshape(RET_V_W, d).astype(BF16),
                    w_mla_o.reshape(MLA_HEADS * MLA_V, d).astype(BF16), gates, n_ctx_tok)
    x1, h_hi, h_lo, hp = _out_proj(merged, w_out.reshape(d, d).astype(BF16), x_ctx, x_lat,
                                   norm_ffn.reshape(1, d), mod4, n_ctx_tok, dec_seq)

    eidx, wts, rank, cnt = _router(h_hi, h_lo, w_router.reshape(d, N_EXPERTS).T,
                                   router_bias.reshape(N_EXPERTS, 1))
    n_blocks = t * TOP_K // EXPERT_BM + N_EXPERTS
    pstart, uidx, ue, meta, fill = _block_plan(cnt[:, 0].astype(I32), n_blocks)
    dest = _dest_rows(eidx, rank, pstart.astype(F32).reshape(N_EXPERTS, 1)).T.reshape(-1)
    xs = _dispatch(fill, meta, dest, hp, n_blocks)
    ys = _experts(uidx, ue, meta, xs, w_exp_in.reshape(N_EXPERTS, d, 2 * D_EXPERT),
                  w_exp_out.reshape(N_EXPERTS, D_EXPERT, d))
    y_ctx, y_lat = _combine(dest, wts.T, x1, h_hi, mod4, w_shared_in.reshape(d, 2 * D_SHARED).astype(BF16),
                            w_shared_out.reshape(D_SHARED, d).astype(BF16), norm_final.reshape(1, d), ys,
                            n_ctx_tok, dec_seq)

    y_prompt = y_ctx.reshape(n_ctx, seq, d)
    y_sample = y_lat.reshape(n_lat, dec_seq, d)
    new_ckv = ckv[:n_ctx_tok].reshape(n_ctx, 1, seq, MLA_KV_RANK)
    new_krope = kpe_raw[:n_ctx_tok, :MLA_ROPE].reshape(n_ctx, 1, seq, MLA_ROPE)
    new_state = ret_state.reshape(n_ctx, 1, 2, RET_HEADS, RET_DK, RET_DV)
    return (y_prompt, y_sample, new_ckv, new_krope, new_state)
```

```python
import functools

import jax
import jax.numpy as jnp
from jax import lax
from jax.experimental import pallas as pl
from jax.experimental.pallas import tpu as pltpu

F32 = jnp.float32
BF16 = jnp.bfloat16
U32 = jnp.uint32
I32 = jnp.int32

D_MODEL = 2048
GRID_W = 64
NORM_EPS = 1e-6
RET_HEADS = 8
RET_DK = 128
RET_DV = 256
RET_CHUNK = 128
RET_QK_W = RET_HEADS * RET_DK
RET_V_W = RET_HEADS * RET_DV
MLA_HEADS = 16
MLA_Q_RANK = 512
MLA_KV_RANK = 512
MLA_NOPE = 128
MLA_ROPE = 64
MLA_V = 128
ROPE_BASE = 10000.0
N_EXPERTS = 256
TOP_K = 8
N_GROUPS = 8
TOPK_GROUPS = 4
GROUP_SIZE = N_EXPERTS // N_GROUPS
D_EXPERT = 512
D_SHARED = 512
ROUTED_SCALE = 2.5

LANES = 128
SUBLANES = 8
ROW_TILES = D_MODEL // (2 * LANES)
assert ROW_TILES == SUBLANES
Q_HEAD_W = 256
N_MOD = 6
MOD_ROWS = 16
TM_NORM = 256
EXPERT_BM = 256
MIB = 1 << 20


def _cparams(sem, vmem_mib=None):
    kw = dict(dimension_semantics=sem)
    if vmem_mib is not None:
        kw["vmem_limit_bytes"] = vmem_mib * MIB
    return pltpu.CompilerParams(**kw)


def _sigmoid(x):
    return 1.0 / (1.0 + jnp.exp(-x))


def _silu(x):
    return x * _sigmoid(x)


def _rms(x, w):
    ms = jnp.mean(x * x, axis=-1, keepdims=True)
    return x * lax.rsqrt(ms + NORM_EPS) * w


def _pack_bf16_pair(lo, hi):
    return pltpu.pack_elementwise([lo, hi], packed_dtype=BF16)


def _unpack_bf16_pair(w):
    lo = pltpu.unpack_elementwise(w, index=0, packed_dtype=BF16, unpacked_dtype=F32)
    hi = pltpu.unpack_elementwise(w, index=1, packed_dtype=BF16, unpacked_dtype=F32)
    return lo, hi


def _store_row_tiles(ref, x):
    m = x.shape[0]
    for s in range(ROW_TILES):
        lo = x[:, 2 * LANES * s:2 * LANES * s + LANES]
        hi = x[:, 2 * LANES * s + LANES:2 * LANES * (s + 1)]
        ref[pl.ds(s, m, SUBLANES), :] = _pack_bf16_pair(lo, hi)


def _load_row_tiles(ref, m):
    parts = []
    for s in range(ROW_TILES):
        parts.extend(_unpack_bf16_pair(ref[pl.ds(s, m, SUBLANES), :]))
    return parts


def _rope128(x, cos, sin):
    lane = lax.broadcasted_iota(I32, x.shape, 1)
    first = (lane & 31) < 16
    rot = jnp.where(first, -pltpu.roll(x, LANES - 16, 1), pltpu.roll(x, 16, 1))
    return x * cos + rot * sin


def _mod_kernel(c_ref, w_ref, b_ref, o_ref):
    s = _silu(c_ref[...]).astype(BF16)
    o_ref[...] = jnp.dot(s, w_ref[...].astype(BF16), preferred_element_type=F32) + b_ref[...]


def _modulation(cond, w_mod, b_mod):
    n = w_mod.shape[1]
    tn = 1024
    return pl.pallas_call(
        _mod_kernel,
        grid=(n // tn,),
        in_specs=[pl.BlockSpec((MOD_ROWS, D_MODEL), lambda j: (0, 0)),
                  pl.BlockSpec((D_MODEL, tn), lambda j: (0, j)),
                  pl.BlockSpec((1, tn), lambda j: (0, j))],
        out_specs=pl.BlockSpec((MOD_ROWS, tn), lambda j: (0, j)),
        out_shape=jax.ShapeDtypeStruct((MOD_ROWS, n), F32),
        compiler_params=_cparams(("arbitrary",), 40),
        name="modulation",
    )(cond, w_mod, b_mod)


def _mod_spec(which, tm, n_ctx_tok, dec_seq):
    def index(i, *_):
        tok = i * tm
        row = jnp.where(tok < n_ctx_tok, 0, 1 + (tok - n_ctx_tok) // dec_seq)
        return (row, which, 0, 0)
    return pl.BlockSpec((None, None, 1, D_MODEL), index)


def _path_specs(tm, width, n_ctx_tok, axis=0):
    n_ctx_tiles = n_ctx_tok // tm

    def ctx(*idx):
        return (jnp.minimum(idx[axis], n_ctx_tiles - 1), 0)

    def lat(*idx):
        return (jnp.maximum(idx[axis] - n_ctx_tiles, 0), 0)

    return pl.BlockSpec((tm, width), ctx), pl.BlockSpec((tm, width), lat)


def _path_pick(tile, tm, n_ctx_tok, ctx_ref, lat_ref):
    return jnp.where(tile * tm < n_ctx_tok, ctx_ref[...], lat_ref[...])


def _normmod_kernel(xc_ref, xl_ref, w_ref, sc_ref, sh_ref, o_ref, *, n_ctx_tok):
    x = _path_pick(pl.program_id(0), o_ref.shape[0], n_ctx_tok, xc_ref, xl_ref)
    y = _rms(x, w_ref[...])
    o_ref[...] = (y * (1.0 + sc_ref[...]) + sh_ref[...]).astype(BF16)


def _norm_modulate(x_ctx, x_lat, w, mod4, n_ctx_tok, dec_seq):
    t = x_ctx.shape[0] + x_lat.shape[0]
    tm = TM_NORM
    return pl.pallas_call(
        functools.partial(_normmod_kernel, n_ctx_tok=n_ctx_tok),
        grid=(t // tm,),
        in_specs=[*_path_specs(tm, D_MODEL, n_ctx_tok),
                  pl.BlockSpec((1, D_MODEL), lambda i: (0, 0)),
                  _mod_spec(1, tm, n_ctx_tok, dec_seq),
                  _mod_spec(0, tm, n_ctx_tok, dec_seq)],
        out_specs=pl.BlockSpec((tm, D_MODEL), lambda i: (i, 0)),
        out_shape=jax.ShapeDtypeStruct((t, D_MODEL), BF16),
        compiler_params=_cparams(("parallel",)),
        name="norm_modulate",
    )(x_ctx, x_lat, w, mod4, mod4)


def _mm_kernel(a_ref, w_ref, o_ref, *, act):
    acc = jnp.dot(a_ref[...].astype(BF16), w_ref[...], preferred_element_type=F32)
    if act == "sigmoid":
        acc = _sigmoid(acc)
    o_ref[...] = acc.astype(o_ref.dtype)


def _matmul(a, w, *, tm, tn, act=None, name):
    m, k = a.shape
    n = w.shape[1]
    tm = min(tm, m)
    assert m % tm == 0 and n % tn == 0
    return pl.pallas_call(
        functools.partial(_mm_kernel, act=act),
        grid=(n // tn, m // tm),
        in_specs=[pl.BlockSpec((tm, k), lambda j, i: (i, 0)),
                  pl.BlockSpec((k, tn), lambda j, i: (0, j))],
        out_specs=pl.BlockSpec((tm, tn), lambda j, i: (i, j)),
        out_shape=jax.ShapeDtypeStruct((m, n), BF16),
        compiler_params=_cparams(("parallel", "parallel"), 48),
        name=name,
    )(a, w)


def _lowrank_kernel(h_ref, w_ref, qnw_ref, kvnw_ref, cos_ref, sin_ref,
                    qn_ref, ckv_ref, ckvb_ref, kpe_ref, kper_ref):
    acc = jnp.dot(h_ref[...], w_ref[...], preferred_element_type=F32)
    qn_ref[...] = _rms(acc[:, :MLA_Q_RANK], qnw_ref[...]).astype(BF16)
    ckv = _rms(acc[:, MLA_Q_RANK:MLA_Q_RANK + MLA_KV_RANK], kvnw_ref[...])
    ckv_ref[...] = ckv
    ckvb_ref[...] = ckv.astype(BF16)
    kp = acc[:, MLA_Q_RANK + MLA_KV_RANK:]
    kpe_ref[...] = kp
    kper_ref[...] = _rope128(kp, cos_ref[...], sin_ref[...]).astype(BF16)


def _rope_spec(tm, n_ctx_tok, dec_seq):
    per_seq = dec_seq // tm

    def index(i, *_):
        tok = i * tm
        return (jnp.where(tok < n_ctx_tok, 0, 1 + ((tok - n_ctx_tok) // tm) % per_seq), 0)
    return pl.BlockSpec((tm, LANES), index)


def _lowrank(h, w_low, qnw, kvnw, cos_t, sin_t, n_ctx_tok, dec_seq):
    t = h.shape[0]
    tm = TM_NORM
    n = w_low.shape[1]
    row = lambda i: (i, 0)
    const = lambda i: (0, 0)
    return pl.pallas_call(
        _lowrank_kernel,
        grid=(t // tm,),
        in_specs=[pl.BlockSpec((tm, D_MODEL), row),
                  pl.BlockSpec((D_MODEL, n), const),
                  pl.BlockSpec((1, MLA_Q_RANK), const),
                  pl.BlockSpec((1, MLA_KV_RANK), const),
                  _rope_spec(tm, n_ctx_tok, dec_seq),
                  _rope_spec(tm, n_ctx_tok, dec_seq)],
        out_specs=[pl.BlockSpec((tm, MLA_Q_RANK), row),
                   pl.BlockSpec((tm, MLA_KV_RANK), row),
                   pl.BlockSpec((tm, MLA_KV_RANK), row),
                   pl.BlockSpec((tm, LANES), row),
                   pl.BlockSpec((tm, LANES), row)],
        out_shape=[jax.ShapeDtypeStruct((t, MLA_Q_RANK), BF16),
                   jax.ShapeDtypeStruct((t, MLA_KV_RANK), F32),
                   jax.ShapeDtypeStruct((t, MLA_KV_RANK), BF16),
                   jax.ShapeDtypeStruct((t, LANES), F32),
                   jax.ShapeDtypeStruct((t, LANES), BF16)],
        compiler_params=_cparams(("parallel",), 40),
        name="lowrank_proj",
    )(h, w_low, qnw, kvnw, cos_t, sin_t)


Q_PRESCALE = (MLA_NOPE + MLA_ROPE) ** -0.5 * 1.4426950408889634


def _qb_kernel(qn_ref, w_ref, cos_ref, sin_ref, o_ref):
    qn = qn_ref[...]
    cos = cos_ref[...]
    sin = sin_ref[...]
    for h in range(MLA_HEADS):
        lo = h * Q_HEAD_W
        acc = jnp.dot(qn, w_ref[:, lo:lo + Q_HEAD_W], preferred_element_type=F32) * Q_PRESCALE
        o_ref[:, lo:lo + MLA_NOPE] = acc[:, :MLA_NOPE].astype(BF16)
        o_ref[:, lo + MLA_NOPE:lo + Q_HEAD_W] = _rope128(acc[:, MLA_NOPE:], cos, sin).astype(BF16)


def _q_up(qn, w_qb, cos_t, sin_t, n_ctx_tok, dec_seq):
    t = qn.shape[0]
    tm = TM_NORM
    n = w_qb.shape[1]
    return pl.pallas_call(
        _qb_kernel,
        grid=(t // tm,),
        in_specs=[pl.BlockSpec((tm, MLA_Q_RANK), lambda i: (i, 0)),
                  pl.BlockSpec((MLA_Q_RANK, n), lambda i: (0, 0)),
                  _rope_spec(tm, n_ctx_tok, dec_seq),
                  _rope_spec(tm, n_ctx_tok, dec_seq)],
        out_specs=pl.BlockSpec((tm, n), lambda i: (i, 0)),
        out_shape=jax.ShapeDtypeStruct((t, n), BF16),
        compiler_params=_cparams(("parallel",), 40),
        name="q_up_rope",
    )(qn, w_qb, cos_t, sin_t)


def _ret_kernel(logit_ref, q_ref, k_ref, v_ref, g_ref, gnw_ref, *rest, n_chunks, has_s0, emit_state):
    rest = list(rest)
    s0_ref = rest.pop(0) if has_s0 else None
    o_ref = rest.pop(0)
    sfin_ref = rest.pop(0) if emit_state else None
    sf_scr, sb_scr = rest
    c_len = RET_CHUNK
    head = pl.program_id(1)

    def log_gamma(direction):
        l = jnp.full((c_len, 1), logit_ref[direction, head], F32)
        return jnp.minimum(l, 0.0) - jnp.log1p(jnp.exp(-jnp.abs(l)))

    lgf_c = log_gamma(0)
    lgb_c = log_gamma(1)
    ii = lax.broadcasted_iota(I32, (c_len, c_len), 0).astype(F32)
    jj = lax.broadcasted_iota(I32, (c_len, c_len), 1).astype(F32)
    k_scale = RET_DK ** -0.5
    decay = (jnp.where(ii >= jj, jnp.exp(jnp.maximum(ii - jj, 0.0) * lgf_c), 0.0)
             + jnp.where(jj >= ii, jnp.exp(jnp.maximum(jj - ii, 0.0) * lgb_c), 0.0)) * k_scale
    pos = lax.broadcasted_iota(I32, (c_len, 1), 0).astype(F32)
    qdec_f = jnp.exp((pos + 1.0) * lgf_c)
    qdec_b = jnp.exp((c_len - pos) * lgb_c)
    kdec_f = jnp.exp((c_len - 1.0 - pos) * lgf_c) * k_scale
    kdec_b = jnp.exp(pos * lgb_c) * k_scale
    cdec_f = jnp.exp(c_len * lgf_c)
    cdec_b = jnp.exp(c_len * lgb_c)
    tn_dims = (((0,), (0,)), ((), ()))
    nt_dims = (((1,), (1,)), ((), ()))

    def chunk(c):
        return pl.ds(c * c_len, c_len)

    for c in range(n_chunks):
        kc = k_ref[chunk(c), :].astype(F32)
        vc = v_ref[chunk(c), :]
        sf_scr[c] = lax.dot_general((kc * kdec_f).astype(BF16), vc, tn_dims, preferred_element_type=F32)
        sb_scr[c] = lax.dot_general((kc * kdec_b).astype(BF16), vc, tn_dims, preferred_element_type=F32)

    s = s0_ref[0] if has_s0 else jnp.zeros((RET_DK, RET_DV), F32)
    for c in range(n_chunks):
        kv = sf_scr[c]
        sf_scr[c] = s
        s = cdec_f * s + kv
    if emit_state:
        sfin_ref[0] = s
    s = s0_ref[1] if has_s0 else jnp.zeros((RET_DK, RET_DV), F32)
    for c in reversed(range(n_chunks)):
        kv = sb_scr[c]
        sb_scr[c] = s
        s = cdec_b * s + kv
    if emit_state:
        sfin_ref[1] = s

    gnw = gnw_ref[...]
    for c in range(n_chunks):
        qc = q_ref[chunk(c), :]
        kc = k_ref[chunk(c), :]
        vc = v_ref[chunk(c), :]
        sc = lax.dot_general(qc, kc, nt_dims, preferred_element_type=F32) * decay
        qf = qc.astype(F32)
        qcat = jnp.concatenate([(qf * qdec_f).astype(BF16), (qf * qdec_b).astype(BF16)], axis=1)
        scat = jnp.concatenate([sf_scr[c], sb_scr[c]], axis=0).astype(BF16)
        o = (jnp.dot(sc.astype(BF16), vc, preferred_element_type=F32)
             + jnp.dot(qcat, scat, preferred_element_type=F32))
        mu = jnp.mean(o, axis=-1, keepdims=True)
        var = jnp.mean(jnp.square(o - mu), axis=-1, keepdims=True)
        on = (o - mu) * lax.rsqrt(var + NORM_EPS) * gnw
        o_ref[chunk(c), :] = (_silu(g_ref[chunk(c), :].astype(F32)) * on).astype(BF16)


def _retention(qkvg, logits, gn_w, s0, *, n_seq, seq, row_block0, emit_state):
    n_chunks = seq // RET_CHUNK
    has_s0 = s0 is not None
    k_off = RET_QK_W // RET_DK
    v_off = 2 * RET_QK_W // RET_DV
    g_off = (2 * RET_QK_W + RET_V_W) // RET_DV
    in_specs = [pl.BlockSpec(memory_space=pltpu.SMEM),
                pl.BlockSpec((seq, RET_DK), lambda b, h: (row_block0 + b, h)),
                pl.BlockSpec((seq, RET_DK), lambda b, h: (row_block0 + b, k_off + h)),
                pl.BlockSpec((seq, RET_DV), lambda b, h: (row_block0 + b, v_off + h)),
                pl.BlockSpec((seq, RET_DV), lambda b, h: (row_block0 + b, g_off + h)),
                pl.BlockSpec((1, RET_DV), lambda b, h: (0, h))]
    args = [logits, qkvg, qkvg, qkvg, qkvg, gn_w]
    state_spec = pl.BlockSpec((None, 2, None, RET_DK, RET_DV), lambda b, h: (b, 0, h, 0, 0))
    if has_s0:
        in_specs.append(state_spec)
        args.append(s0)
    out_specs = [pl.BlockSpec((seq, RET_DV), lambda b, h: (b, h))]
    out_shape = [jax.ShapeDtypeStruct((n_seq * seq, RET_V_W), BF16)]
    if emit_state:
        out_specs.append(state_spec)
        out_shape.append(jax.ShapeDtypeStruct((n_seq, 2, RET_HEADS, RET_DK, RET_DV), F32))
    res = pl.pallas_call(
        functools.partial(_ret_kernel, n_chunks=n_chunks, has_s0=has_s0, emit_state=emit_state),
        grid=(n_seq, RET_HEADS),
        in_specs=in_specs,
        out_specs=out_specs,
        out_shape=out_shape,
        scratch_shapes=[pltpu.VMEM((n_chunks, RET_DK, RET_DV), F32),
                        pltpu.VMEM((n_chunks, RET_DK, RET_DV), F32)],
        compiler_params=_cparams(("parallel", "parallel")),
        name="retention_seq%d" % seq,
    )(*args)
    return res


V_EXT_W = MLA_V + LANES


def _ones_column(rows):
    lane = lax.broadcasted_iota(I32, (rows, LANES), 1)
    return jnp.where(lane == 0, 1.0, 0.0).astype(BF16)


def _softmax_pv(q, kcat, v_ext):
    s = lax.dot_general(q, kcat, (((1,), (1,)), ((), ())), preferred_element_type=F32)
    e = jnp.exp2(s - jnp.max(s, axis=-1, keepdims=True))
    o = jnp.dot(e.astype(BF16), v_ext, preferred_element_type=F32)
    return o[:, :MLA_V] / o[:, MLA_V:MLA_V + 1]


def _attn_ctx_kernel(q_ref, kv_ref, kpe_ref, o_ref):
    kpe = kpe_ref[...]
    ones = _ones_column(kpe.shape[0])
    kvw = MLA_NOPE + MLA_V
    for h in range(MLA_HEADS):
        kv = kv_ref[:, h * kvw:(h + 1) * kvw]
        kcat = jnp.concatenate([kv[:, :MLA_NOPE], kpe], axis=1)
        v_ext = jnp.concatenate([kv[:, MLA_NOPE:], ones], axis=1)
        o = _softmax_pv(q_ref[:, h * Q_HEAD_W:(h + 1) * Q_HEAD_W], kcat, v_ext)
        o_ref[:, h * MLA_V:(h + 1) * MLA_V] = o.astype(o_ref.dtype)


def _attention_ctx(q_all, kv_all, kpe_all, n_seq, seq):
    return pl.pallas_call(
        _attn_ctx_kernel,
        grid=(n_seq,),
        in_specs=[pl.BlockSpec((seq, MLA_HEADS * Q_HEAD_W), lambda b: (b, 0)),
                  pl.BlockSpec((seq, MLA_HEADS * (MLA_NOPE + MLA_V)), lambda b: (b, 0)),
                  pl.BlockSpec((seq, LANES), lambda b: (b, 0))],
        out_specs=pl.BlockSpec((seq, MLA_HEADS * MLA_V), lambda b: (b, 0)),
        out_shape=jax.ShapeDtypeStruct((n_seq * seq, MLA_HEADS * MLA_V), BF16),
        compiler_params=_cparams(("parallel",), 40),
        name="attention_ctx",
    )(q_all, kv_all, kpe_all)


def _attn_lat_kernel(q_ref, kvc_ref, kvl_ref, kpc_ref, kpl_ref, o_ref, kcat, vcat, *, past, seq):
    @pl.when(pl.program_id(2) == 0)
    def _():
        kcat[0:past, 0:MLA_NOPE] = kvc_ref[:, :MLA_NOPE]
        kcat[0:past, MLA_NOPE:] = kpc_ref[...]
        kcat[past:past + seq, 0:MLA_NOPE] = kvl_ref[:, :MLA_NOPE]
        kcat[past:past + seq, MLA_NOPE:] = kpl_ref[...]
        vcat[0:past, 0:MLA_V] = kvc_ref[:, MLA_NOPE:]
        vcat[past:past + seq, 0:MLA_V] = kvl_ref[:, MLA_NOPE:]
        vcat[:, MLA_V:] = _ones_column(past + seq)

    sub = 256
    for r in range(q_ref.shape[0] // sub):
        rows = slice(r * sub, (r + 1) * sub)
        o_ref[rows, :] = _softmax_pv(q_ref[rows, :], kcat[...], vcat[...]).astype(o_ref.dtype)


def _attention_lat(q_all, kv_ctx, kv_all, kpe_ctx, kpe_all, n_seq, seq, past, n_ctx_tok):
    tq = 1024
    nq = seq // tq
    q0 = n_ctx_tok // tq
    s0 = n_ctx_tok // seq
    kvw = MLA_NOPE + MLA_V
    return pl.pallas_call(
        functools.partial(_attn_lat_kernel, past=past, seq=seq),
        grid=(n_seq, MLA_HEADS, nq),
        in_specs=[pl.BlockSpec((tq, Q_HEAD_W), lambda b, h, i: (q0 + b * nq + i, h)),
                  pl.BlockSpec((past, kvw), lambda b, h, i: (b, h)),
                  pl.BlockSpec((seq, kvw), lambda b, h, i: (s0 + b, h)),
                  pl.BlockSpec((past, LANES), lambda b, h, i: (b, 0)),
                  pl.BlockSpec((seq, LANES), lambda b, h, i: (s0 + b, 0))],
        out_specs=pl.BlockSpec((tq, MLA_V), lambda b, h, i: (b * nq + i, h)),
        out_shape=jax.ShapeDtypeStruct((n_seq * seq, MLA_HEADS * MLA_V), BF16),
        scratch_shapes=[pltpu.VMEM((past + seq, MLA_NOPE + LANES), BF16),
                        pltpu.VMEM((past + seq, V_EXT_W), BF16)],
        compiler_params=_cparams(("parallel", "parallel", "arbitrary"), 40),
        name="attention_lat",
    )(q_all, kv_ctx, kv_all, kpe_ctx, kpe_all)


def _merge_kernel(rc_ref, rl_ref, ac_ref, al_ref, wr_ref, wm_ref, gr_ref, gm_ref, o_ref, *, n_ctx_tok):
    tile = pl.program_id(1)
    tm = o_ref.shape[0]
    r = _path_pick(tile, tm, n_ctx_tok, rc_ref, rl_ref)
    a = _path_pick(tile, tm, n_ctx_tok, ac_ref, al_ref)
    ro = jnp.dot(r, wr_ref[...], preferred_element_type=F32)
    mo = jnp.dot(a, wm_ref[...], preferred_element_type=F32)
    o_ref[...] = (gr_ref[...].astype(F32) * ro + gm_ref[...].astype(F32) * mo).astype(BF16)


def _merge(r_ctx, r_lat, a_ctx, a_lat, w_ret_o, w_mla_o, gates, n_ctx_tok):
    t = gates.shape[0]
    tm, tn = 512, 1024
    nj = D_MODEL // tn
    return pl.pallas_call(
        functools.partial(_merge_kernel, n_ctx_tok=n_ctx_tok),
        grid=(nj, t // tm),
        in_specs=[*_path_specs(tm, RET_V_W, n_ctx_tok, axis=1),
                  *_path_specs(tm, MLA_HEADS * MLA_V, n_ctx_tok, axis=1),
                  pl.BlockSpec((RET_V_W, tn), lambda j, i: (0, j)),
                  pl.BlockSpec((MLA_HEADS * MLA_V, tn), lambda j, i: (0, j)),
                  pl.BlockSpec((tm, tn), lambda j, i: (i, j)),
                  pl.BlockSpec((tm, tn), lambda j, i: (i, nj + j))],
        out_specs=pl.BlockSpec((tm, tn), lambda j, i: (i, j)),
        out_shape=jax.ShapeDtypeStruct((t, D_MODEL), BF16),
        compiler_params=_cparams(("parallel", "parallel"), 48),
        name="branch_merge",
    )(r_ctx, r_lat, a_ctx, a_lat, w_ret_o, w_mla_o, gates, gates)


def _outproj_kernel(m_ref, w_ref, xc_ref, xl_ref, g1_ref, nw_ref, sc_ref, sh_ref,
                    x1_ref, hhi_ref, hlo_ref, hp_ref, *, n_ctx_tok):
    mix = jnp.dot(m_ref[...], w_ref[...], preferred_element_type=F32)
    x = _path_pick(pl.program_id(0), x1_ref.shape[0], n_ctx_tok, xc_ref, xl_ref)
    x1 = x + g1_ref[...] * mix
    x1_ref[...] = x1
    h2 = _rms(x1, nw_ref[...]) * (1.0 + sc_ref[...]) + sh_ref[...]
    hi = h2.astype(BF16)
    hif = hi.astype(F32)
    hhi_ref[...] = hi
    hlo_ref[...] = (h2 - hif).astype(BF16)
    _store_row_tiles(hp_ref, hif)


def _out_proj(merged, w_out, x_ctx, x_lat, norm_w, mod4, n_ctx_tok, dec_seq):
    t = merged.shape[0]
    tm = TM_NORM
    row = lambda i: (i, 0)
    const = lambda i: (0, 0)
    return pl.pallas_call(
        functools.partial(_outproj_kernel, n_ctx_tok=n_ctx_tok),
        grid=(t // tm,),
        in_specs=[pl.BlockSpec((tm, D_MODEL), row),
                  pl.BlockSpec((D_MODEL, D_MODEL), const),
                  *_path_specs(tm, D_MODEL, n_ctx_tok),
                  _mod_spec(2, tm, n_ctx_tok, dec_seq),
                  pl.BlockSpec((1, D_MODEL), const),
                  _mod_spec(4, tm, n_ctx_tok, dec_seq),
                  _mod_spec(3, tm, n_ctx_tok, dec_seq)],
        out_specs=[pl.BlockSpec((tm, D_MODEL), row),
                   pl.BlockSpec((tm, D_MODEL), row),
                   pl.BlockSpec((tm, D_MODEL), row),
                   pl.BlockSpec((tm * ROW_TILES, LANES), row)],
        out_shape=[jax.ShapeDtypeStruct((t, D_MODEL), F32),
                   jax.ShapeDtypeStruct((t, D_MODEL), BF16),
                   jax.ShapeDtypeStruct((t, D_MODEL), BF16),
                   jax.ShapeDtypeStruct((t * ROW_TILES, LANES), U32)],
        compiler_params=_cparams(("parallel",), 48),
        name="out_proj_norm",
    )(merged, w_out, x_ctx, x_lat, mod4, norm_w, mod4, mod4)


def _router_kernel(hhi_ref, hlo_ref, wrt_ref, bias_ref, eidx_ref, wts_ref, rank_ref, cnt_ref, carry):
    tm = hhi_ref.shape[0]
    n_e = N_EXPERTS

    @pl.when(pl.program_id(0) == 0)
    def _():
        carry[...] = jnp.zeros_like(carry)

    wr = wrt_ref[...]
    wr_hi = wr.astype(BF16)
    wr_lo = (wr - wr_hi.astype(F32)).astype(BF16)
    hh = hhi_ref[...]
    hl = hlo_ref[...]
    nt_dims = (((1,), (1,)), ((), ()))
    logits = (lax.dot_general(wr_hi, hh, nt_dims, preferred_element_type=F32)
              + lax.dot_general(wr_hi, hl, nt_dims, preferred_element_type=F32)
              + lax.dot_general(wr_lo, hh, nt_dims, preferred_element_type=F32))
    scores = _sigmoid(logits)
    choice = scores + bias_ref[...]
    neg = -jnp.inf
    row = lax.broadcasted_iota(I32, (n_e, tm), 0).astype(F32)
    grow = lax.broadcasted_iota(I32, (GROUP_SIZE, tm), 0).astype(F32)

    gs = []
    for g in range(N_GROUPS):
        blk = choice[g * GROUP_SIZE:(g + 1) * GROUP_SIZE]
        m1 = jnp.max(blk, axis=0, keepdims=True)
        i1 = jnp.min(jnp.where(blk == m1, grow, float(GROUP_SIZE)), axis=0, keepdims=True)
        m2 = jnp.max(jnp.where(grow == i1, neg, blk), axis=0, keepdims=True)
        gs.append(m1 + m2)
    gsc = jnp.concatenate(gs, axis=0)
    gid = lax.broadcasted_iota(I32, (N_GROUPS, tm), 0).astype(F32)
    gsel = jnp.zeros((N_GROUPS, tm), F32)
    for _ in range(TOPK_GROUPS):
        m = jnp.max(gsc, axis=0, keepdims=True)
        idx = jnp.min(jnp.where(gsc == m, gid, float(N_GROUPS)), axis=0, keepdims=True)
        hit = gid == idx
        gsel = jnp.where(hit, 1.0, gsel)
        gsc = jnp.where(hit, neg, gsc)
    masked = jnp.concatenate(
        [jnp.where(gsel[g:g + 1] > 0.0, choice[g * GROUP_SIZE:(g + 1) * GROUP_SIZE], neg)
         for g in range(N_GROUPS)], axis=0)

    idxs, ws = [], []
    onehot = jnp.zeros((n_e, tm), F32)
    for _ in range(TOP_K):
        m = jnp.max(masked, axis=0, keepdims=True)
        idx = jnp.min(jnp.where(masked == m, row, float(n_e)), axis=0, keepdims=True)
        hit = row == idx
        ws.append(jnp.sum(jnp.where(hit, scores, 0.0), axis=0, keepdims=True))
        masked = jnp.where(hit, neg, masked)
        onehot = jnp.where(hit, 1.0, onehot)
        idxs.append(idx)
    w = jnp.concatenate(ws, axis=0)
    wts_ref[...] = w / jnp.sum(w, axis=0, keepdims=True) * ROUTED_SCALE
    eidx_ref[...] = jnp.concatenate(idxs, axis=0).astype(I32)

    t_src = lax.broadcasted_iota(I32, (tm, tm), 0)
    t_dst = lax.broadcasted_iota(I32, (tm, tm), 1)
    before = jnp.where(t_src < t_dst, 1.0, 0.0).astype(BF16)
    prior = jnp.dot(onehot.astype(BF16), before, preferred_element_type=F32) + carry[:, :1]
    rank_ref[...] = jnp.concatenate(
        [jnp.sum(jnp.where(row == idx, prior, 0.0), axis=0, keepdims=True) for idx in idxs],
        axis=0).astype(I32)
    carry[...] = carry[...] + jnp.sum(onehot, axis=1, keepdims=True)
    cnt_ref[...] = carry[...]


def _router(h_hi, h_lo, wr_t, bias_col):
    t = h_hi.shape[0]
    tm = 512
    tok = lambda i: (0, i)
    const = lambda i: (0, 0)
    return pl.pallas_call(
        _router_kernel,
        grid=(t // tm,),
        in_specs=[pl.BlockSpec((tm, D_MODEL), lambda i: (i, 0)),
                  pl.BlockSpec((tm, D_MODEL), lambda i: (i, 0)),
                  pl.BlockSpec((N_EXPERTS, D_MODEL), const),
                  pl.BlockSpec((N_EXPERTS, 1), const)],
        out_specs=[pl.BlockSpec((TOP_K, tm), tok),
                   pl.BlockSpec((TOP_K, tm), tok),
                   pl.BlockSpec((TOP_K, tm), tok),
                   pl.BlockSpec((N_EXPERTS, LANES), const)],
        out_shape=[jax.ShapeDtypeStruct((TOP_K, t), I32),
                   jax.ShapeDtypeStruct((TOP_K, t), F32),
                   jax.ShapeDtypeStruct((TOP_K, t), I32),
                   jax.ShapeDtypeStruct((N_EXPERTS, LANES), F32)],
        scratch_shapes=[pltpu.VMEM((N_EXPERTS, LANES), F32)],
        compiler_params=_cparams(("arbitrary",), 40),
        name="router_topk",
    )(h_hi, h_lo, wr_t, bias_col)


def _dest_kernel(eidx_ref, rank_ref, pstart_ref, dest_ref):
    tm = eidx_ref.shape[1]
    row = lax.broadcasted_iota(I32, (N_EXPERTS, tm), 0)
    pstart = pstart_ref[...]
    eidx = eidx_ref[...]
    base = jnp.concatenate(
        [jnp.sum(jnp.where(row == eidx[k:k + 1], pstart, 0.0), axis=0, keepdims=True)
         for k in range(TOP_K)], axis=0)
    dest_ref[...] = (base.astype(I32) + rank_ref[...]) * ROW_TILES


def _dest_rows(eidx, rank, pstart_col):
    t = eidx.shape[1]
    tm = 1024
    tok = lambda i: (0, i)
    return pl.pallas_call(
        _dest_kernel,
        grid=(t // tm,),
        in_specs=[pl.BlockSpec((TOP_K, tm), tok),
                  pl.BlockSpec((TOP_K, tm), tok),
                  pl.BlockSpec((N_EXPERTS, 1), lambda i: (0, 0))],
        out_specs=pl.BlockSpec((TOP_K, tm), tok),
        out_shape=jax.ShapeDtypeStruct((TOP_K, t), I32),
        compiler_params=_cparams(("parallel",)),
        name="dest_rows",
    )(eidx, rank, pstart_col)


def _dispatch_kernel(fill_ref, meta_ref, dest_ref, hp_ref, xs_ref, zbuf, sem, fill_sem):
    tm = hp_ref.shape[0]
    block_rows = zbuf.shape[0]
    n_blocks = xs_ref.shape[0] // block_rows

    def zero_fills(start):
        def run(cp):
            cp.start() if start else cp.wait()

        def per_expert(e, c):
            row = fill_ref[0, e]
            left = fill_ref[1, e]
            p = EXPERT_BM // 2
            while p >= 1:
                take = left & p

                @pl.when(take != 0)
                def _(row=row, p=p):
                    dst = xs_ref.at[pl.ds(pl.multiple_of(row * ROW_TILES, ROW_TILES), p * ROW_TILES)]
                    run(pltpu.make_async_copy(zbuf.at[pl.ds(0, p * ROW_TILES)], dst, fill_sem))

                row = row + take
                p //= 2
            return c

        def per_tail_block(b, c):
            dst = xs_ref.at[pl.ds(pl.multiple_of(b * block_rows, block_rows), block_rows)]
            run(pltpu.make_async_copy(zbuf, dst, fill_sem))
            return c

        lax.fori_loop(0, N_EXPERTS, per_expert, 0)
        lax.fori_loop(meta_ref[0], n_blocks, per_tail_block, 0)

    @pl.when(pl.program_id(0) == 0)
    def _():
        zbuf[...] = jnp.zeros_like(zbuf)
        zero_fills(True)

    def issue(t, c):
        src = hp_ref.at[pl.ds(pl.multiple_of(t * ROW_TILES, ROW_TILES), ROW_TILES)]
        for k in range(TOP_K):
            r = pl.multiple_of(dest_ref[t * TOP_K + k], ROW_TILES)
            pltpu.make_async_copy(src, xs_ref.at[pl.ds(r, ROW_TILES)], sem).start(priority=k % 2)
        return c

    lax.fori_loop(0, tm // ROW_TILES, issue, 0)
    for _ in range(TOP_K):
        pltpu.make_async_copy(hp_ref, xs_ref.at[pl.ds(0, tm)], sem).wait()

    @pl.when(pl.program_id(0) == 0)
    def _():
        zero_fills(False)


def _dispatch(fill, meta, dest, hp, n_blocks):
    tm = TM_NORM * ROW_TILES
    block_rows = EXPERT_BM * ROW_TILES
    return pl.pallas_call(
        _dispatch_kernel,
        grid=(hp.shape[0] // tm,),
        in_specs=[pl.BlockSpec(memory_space=pltpu.SMEM),
                  pl.BlockSpec(memory_space=pltpu.SMEM),
                  pl.BlockSpec((TM_NORM * TOP_K,), lambda i: (i,), memory_space=pltpu.SMEM),
                  pl.BlockSpec((tm, LANES), lambda i: (i, 0))],
        out_specs=pl.BlockSpec(memory_space=pl.ANY),
        out_shape=jax.ShapeDtypeStruct((n_blocks * block_rows, LANES), U32),
        scratch_shapes=[pltpu.VMEM((block_rows, LANES), U32),
                        pltpu.SemaphoreType.DMA(()),
                        pltpu.SemaphoreType.DMA(())],
        compiler_params=_cparams(("arbitrary",)),
        name="moe_dispatch",
    )(fill, meta, dest, hp)


N_WSLOTS = 2


def _expert_kernel(uidx_ref, ue_ref, meta_ref, xs_ref, wi_hbm, wo_hbm, ys_ref,
                   wi_st, wo_st, wi_s, wo_s, sem):
    b = pl.program_id(0)
    n_used = meta_ref[0]
    n_ue = meta_ref[1]

    def weight_copies(j, slot):
        e = ue_ref[j]
        copies = []
        for m, (hbm, st) in enumerate(((wi_hbm, wi_st), (wo_hbm, wo_st))):
            half = hbm.shape[1] // 2
            for q in range(2):
                rows = pl.ds(q * half, half)
                copies.append((pltpu.make_async_copy(hbm.at[e, rows], st.at[slot, rows], sem.at[2 * m + q, slot]), q))
        return copies

    @pl.when(b < n_used)
    def _():
        j = uidx_ref[b]

        @pl.when(b == 0)
        def _():
            for cp, prio in weight_copies(0, 0):
                cp.start(priority=prio)

            @pl.when(n_ue > 1)
            def _():
                for cp, prio in weight_copies(1, 1):
                    cp.start(priority=prio)

        @pl.when((b == 0) | (j != uidx_ref[jnp.maximum(b - 1, 0)]))
        def _():
            slot = j % N_WSLOTS
            for cp, _ in weight_copies(j, slot):
                cp.wait()
            rows = 256

            def cast_in(i, c):
                r = pl.multiple_of(i * rows, rows)
                wi_s[pl.ds(r, rows), :] = wi_st[slot, pl.ds(r, rows), :].astype(BF16)
                return c

            def cast_out(i, c):
                r = pl.multiple_of(i * rows, rows)
                wo_s[pl.ds(r, rows), :] = wo_st[slot, pl.ds(r, rows), :].astype(BF16)
                return c

            lax.fori_loop(0, D_MODEL // rows, cast_in, 0)
            lax.fori_loop(0, D_EXPERT // rows, cast_out, 0)

            @pl.when(j + N_WSLOTS < n_ue)
            def _():
                for cp, prio in weight_copies(j + N_WSLOTS, slot):
                    cp.start(priority=prio)

        bm = EXPERT_BM
        x = jnp.concatenate([p.astype(BF16) for p in _load_row_tiles(xs_ref, bm)], axis=1)
        gu = jnp.dot(x, wi_s[...], preferred_element_type=F32)
        act = (_silu(gu[:, :D_EXPERT]) * gu[:, D_EXPERT:]).astype(BF16)
        _store_row_tiles(ys_ref, jnp.dot(act, wo_s[...], preferred_element_type=F32))


def _experts(uidx, ue, meta, xs, w_exp_in, w_exp_out):
    rows = EXPERT_BM * ROW_TILES
    nb = xs.shape[0] // rows
    live = lambda b, uidx, ue, meta: (jnp.minimum(b, meta[0] - 1), 0)
    grid_spec = pltpu.PrefetchScalarGridSpec(
        num_scalar_prefetch=3,
        grid=(nb,),
        in_specs=[pl.BlockSpec((rows, LANES), live),
                  pl.BlockSpec(memory_space=pl.ANY),
                  pl.BlockSpec(memory_space=pl.ANY)],
        out_specs=pl.BlockSpec((rows, LANES), live),
        scratch_shapes=[pltpu.VMEM((N_WSLOTS, D_MODEL, 2 * D_EXPERT), F32),
                        pltpu.VMEM((N_WSLOTS, D_EXPERT, D_MODEL), F32),
                        pltpu.VMEM((D_MODEL, 2 * D_EXPERT), BF16),
                        pltpu.VMEM((D_EXPERT, D_MODEL), BF16),
                        pltpu.SemaphoreType.DMA((4, N_WSLOTS))],
    )
    return pl.pallas_call(
        _expert_kernel,
        grid_spec=grid_spec,
        out_shape=jax.ShapeDtypeStruct(xs.shape, U32),
        input_output_aliases={3: 0},
        compiler_params=_cparams(("arbitrary",), 52),
        name="routed_experts",
    )(uidx, ue, meta, xs, w_exp_in, w_exp_out)


def _combine_kernel(dest_ref, dnext_ref, wt_ref, x1_ref, hhi_ref, g2_ref, wsi_ref, wso_ref, nf_ref, ys_ref,
                    yc_ref, yl_ref, rows, sem, *, n_ctx_tok):
    tm = x1_ref.shape[0]
    i = pl.program_id(0)
    n = pl.num_programs(0)

    def gather(idx_ref, slot):
        def issue(t, c):
            for k in range(TOP_K):
                r = pl.multiple_of(idx_ref[t * TOP_K + k], ROW_TILES)
                dst = rows.at[slot, k, pl.ds(pl.multiple_of(t * ROW_TILES, ROW_TILES), ROW_TILES)]
                pltpu.make_async_copy(ys_ref.at[pl.ds(r, ROW_TILES)], dst, sem.at[slot]).start(priority=k % 2)
            return c

        lax.fori_loop(0, tm, issue, 0)

    @pl.when(i == 0)
    def _():
        gather(dest_ref, 0)

    @pl.when(i + 1 < n)
    def _():
        gather(dnext_ref, (i + 1) % 2)

    su = jnp.dot(hhi_ref[...], wsi_ref[...], preferred_element_type=F32)
    sact = (_silu(su[:, :D_SHARED]) * su[:, D_SHARED:]).astype(BF16)
    shared = jnp.dot(sact, wso_ref[...], preferred_element_type=F32)

    slot = i % 2
    for k in range(TOP_K):
        pltpu.make_async_copy(ys_ref.at[pl.ds(0, tm * ROW_TILES)], rows.at[slot, k], sem.at[slot]).wait()

    wt = wt_ref[...]
    acc = None
    for k in range(TOP_K):
        wk = wt[:, k:k + 1]
        parts = [wk * p for p in _load_row_tiles(rows.at[slot, k], tm)]
        acc = parts if acc is None else [a + p for a, p in zip(acc, parts)]
    moe = jnp.concatenate(acc, axis=1) + shared
    x2 = x1_ref[...] + g2_ref[...] * moe
    y = _rms(x2, nf_ref[...])
    is_ctx = i * tm < n_ctx_tok

    @pl.when(is_ctx)
    def _():
        yc_ref[...] = y

    @pl.when(jnp.logical_not(is_ctx))
    def _():
        yl_ref[...] = y


def _combine(dest, wts_t, x1, h_hi, mod4, w_sh_in, w_sh_out, norm_final, ys, n_ctx_tok, dec_seq):
    t = x1.shape[0]
    tm = TM_NORM
    n = t // tm
    row = lambda i: (i, 0)
    const = lambda i: (0, 0)
    return pl.pallas_call(
        functools.partial(_combine_kernel, n_ctx_tok=n_ctx_tok),
        grid=(n,),
        in_specs=[pl.BlockSpec((tm * TOP_K,), lambda i: (i,), memory_space=pltpu.SMEM),
                  pl.BlockSpec((tm * TOP_K,), lambda i: (jnp.minimum(i + 1, n - 1),), memory_space=pltpu.SMEM),
                  pl.BlockSpec((tm, TOP_K), row),
                  pl.BlockSpec((tm, D_MODEL), row),
                  pl.BlockSpec((tm, D_MODEL), row),
                  _mod_spec(5, tm, n_ctx_tok, dec_seq),
                  pl.BlockSpec((D_MODEL, 2 * D_SHARED), const),
                  pl.BlockSpec((D_SHARED, D_MODEL), const),
                  pl.BlockSpec((1, D_MODEL), const),
                  pl.BlockSpec(memory_space=pl.ANY)],
        out_specs=list(_path_specs(tm, D_MODEL, n_ctx_tok)),
        out_shape=[jax.ShapeDtypeStruct((n_ctx_tok, D_MODEL), F32),
                   jax.ShapeDtypeStruct((t - n_ctx_tok, D_MODEL), F32)],
        scratch_shapes=[pltpu.VMEM((2, TOP_K, tm * ROW_TILES, LANES), U32), pltpu.SemaphoreType.DMA((2,))],
        compiler_params=_cparams(("arbitrary",), 56),
        name="moe_combine",
    )(dest, dest, wts_t, x1, h_hi, mod4, w_sh_in, w_sh_out, norm_final, ys)


def _rope_tables(tm, dec_seq):
    rows = dec_seq // GRID_W
    row = jnp.broadcast_to(jnp.arange(rows, dtype=F32)[:, None], (rows, GRID_W)).reshape(-1)
    col = jnp.broadcast_to(jnp.arange(GRID_W, dtype=F32)[None, :], (rows, GRID_W)).reshape(-1)
    axis_dim = MLA_ROPE // 2
    freqs = jnp.power(ROPE_BASE, -jnp.arange(0, axis_dim, 2, dtype=F32) / axis_dim)
    ang_r = row[:, None] * freqs
    ang_c = col[:, None] * freqs
    ang = jnp.concatenate([ang_r, ang_r, ang_c, ang_c], axis=-1)
    pad = LANES - MLA_ROPE
    cos = jnp.concatenate([jnp.cos(ang), jnp.ones((dec_seq, pad), F32)], axis=1)
    sin = jnp.concatenate([jnp.sin(ang), jnp.zeros((dec_seq, pad), F32)], axis=1)
    cos = jnp.concatenate([jnp.ones((tm, LANES), F32), cos], axis=0)
    sin = jnp.concatenate([jnp.zeros((tm, LANES), F32), sin], axis=0)
    return cos, sin


def _block_plan(counts, n_blocks):
    bm = EXPERT_BM
    nblk = (counts + bm - 1) // bm
    bend = jnp.cumsum(nblk)
    bstart = bend - nblk
    present = nblk > 0
    pcum = jnp.cumsum(present.astype(I32))
    bidx = jnp.arange(n_blocks, dtype=I32)
    uidx = jnp.sum((present[None, :] & (bstart[None, :] <= bidx[:, None])).astype(I32), axis=1) - 1
    order = jnp.arange(N_EXPERTS, dtype=I32)
    ue = jnp.minimum(jnp.sum((pcum[None, :] <= order[:, None]).astype(I32), axis=1), N_EXPERTS - 1)
    meta = jnp.stack([bend[-1], pcum[-1]]).astype(I32)
    pstart = bstart * bm
    fill = jnp.stack([pstart + counts, nblk * bm - counts]).astype(I32)
    return pstart, uidx.astype(I32), ue.astype(I32), meta, fill


def kernel(x_prompt, x_sample, c, cache_mla_ckv, cache_mla_krope, state_ret, c_ctx, w_mod, b_mod, norm_mix, norm_ffn, w_in, ret_decay_logit, ret_gn_w, w_ret_o, mla_q_norm, w_q_b, mla_kv_norm, w_kv_b, w_mla_o, w_out, w_router, router_bias, w_exp_in, w_exp_out, w_shared_in, w_shared_out, norm_final):
    n_ctx, seq, d = x_prompt.shape
    n_lat, dec_seq, _ = x_sample.shape
    past = cache_mla_ckv.shape[2]
    n_ctx_tok = n_ctx * seq
    n_lat_tok = n_lat * dec_seq
    t = n_ctx_tok + n_lat_tok
    assert d == D_MODEL and w_mod.shape[0] == 1 and n_lat + 1 <= MOD_ROWS
    assert seq % TM_NORM == 0 and dec_seq % TM_NORM == 0 and n_ctx_tok % dec_seq == 0

    x_ctx = x_prompt.reshape(n_ctx_tok, d)
    x_lat = x_sample.reshape(n_lat_tok, d)

    cond = jnp.concatenate([c_ctx[None, :], c, jnp.zeros((MOD_ROWS - 1 - n_lat, d), F32)], axis=0)
    mod = _modulation(cond, w_mod.reshape(d, N_MOD * d), b_mod.reshape(1, N_MOD * d))
    mod4 = mod.reshape(MOD_ROWS, N_MOD, 1, d)

    w_in2 = w_in.reshape(d, -1)
    o_g = 2 * RET_QK_W + RET_V_W
    o_qa = o_g + RET_V_W
    o_kpe = o_qa + MLA_Q_RANK + MLA_KV_RANK
    o_gr = o_kpe + MLA_ROPE
    w_qkvg = w_in2[:, :o_qa].astype(BF16)
    w_low = jnp.pad(w_in2[:, o_qa:o_gr], ((0, 0), (0, LANES - MLA_ROPE))).astype(BF16)
    w_gates = w_in2[:, o_gr:].astype(BF16)
    wq3 = w_q_b.reshape(MLA_Q_RANK, MLA_HEADS, MLA_NOPE + MLA_ROPE)
    w_qb = jnp.pad(wq3, ((0, 0), (0, 0), (0, Q_HEAD_W - MLA_NOPE - MLA_ROPE))
                   ).reshape(MLA_Q_RANK, MLA_HEADS * Q_HEAD_W).astype(BF16)
    w_kvb = w_kv_b.reshape(MLA_KV_RANK, -1).astype(BF16)
    cos_t, sin_t = _rope_tables(TM_NORM, dec_seq)

    h1 = _norm_modulate(x_ctx, x_lat, norm_mix.reshape(1, d), mod4, n_ctx_tok, dec_seq)
    qkvg = _matmul(h1, w_qkvg, tm=1024, tn=1024, name="proj_qkvg")
    gates = _matmul(h1, w_gates, tm=1024, tn=1024, act="sigmoid", name="proj_gates")
    qn, ckv, ckv_b, kpe_raw, kpe_rot = _lowrank(
        h1, w_low, mla_q_norm.reshape(1, -1), mla_kv_norm.reshape(1, -1), cos_t, sin_t, n_ctx_tok, dec_seq)

    logits = ret_decay_logit.reshape(2, RET_HEADS)
    gn_w = ret_gn_w.reshape(1, RET_V_W)
    r_ctx, ret_state = _retention(qkvg, logits, gn_w, None, n_seq=n_ctx, seq=seq, row_block0=0,
                                  emit_state=True)
    (r_lat,) = _retention(qkvg, logits, gn_w, state_ret.reshape(n_lat, 2, RET_HEADS, RET_DK, RET_DV),
                          n_seq=n_lat, seq=dec_seq, row_block0=n_ctx_tok // dec_seq, emit_state=False)

    q_all = _q_up(qn, w_qb, cos_t, sin_t, n_ctx_tok, dec_seq)
    kv_all = _matmul(ckv_b, w_kvb, tm=1024, tn=1024, name="kv_up")
    kv_ctx = _matmul(cache_mla_ckv.reshape(n_lat * past, MLA_KV_RANK), w_kvb, tm=1024, tn=1024,
                     name="kv_up_cache")
    kpe_ctx = jnp.pad(cache_mla_krope.reshape(n_lat * past, MLA_ROPE),
                      ((0, 0), (0, LANES - MLA_ROPE))).astype(BF16)
    a_ctx = _attention_ctx(q_all, kv_all, kpe_rot, n_ctx, seq)
    a_lat = _attention_lat(q_all, kv_ctx, kv_all, kpe_ctx, kpe_rot, n_lat, dec_seq, past, n_ctx_tok)

    merged = _merge(r_ctx, r_lat, a_ctx, a_lat, w_ret_o.reshape(RET_V_W, d).astype(BF16),
                    w_mla_o.reshape(MLA_HEADS * MLA_V, d).astype(BF16), gates, n_ctx_tok)
    x1, h_hi, h_lo, hp = _out_proj(merged, w_out.reshape(d, d).astype(BF16), x_ctx, x_lat,
                                   norm_ffn.reshape(1, d), mod4, n_ctx_tok, dec_seq)

    eidx, wts, rank, cnt = _router(h_hi, h_lo, w_router.reshape(d, N_EXPERTS).T,
                                   router_bias.reshape(N_EXPERTS, 1))
    n_blocks = t * TOP_K // EXPERT_BM + N_EXPERTS
    pstart, uidx, ue, meta, fill = _block_plan(cnt[:, 0].astype(I32), n_blocks)
    dest = _dest_rows(eidx, rank, pstart.astype(F32).reshape(N_EXPERTS, 1)).T.reshape(-1)
    xs = _dispatch(fill, meta, dest, hp, n_blocks)
    ys = _experts(uidx, ue, meta, xs, w_exp_in.reshape(N_EXPERTS, d, 2 * D_EXPERT),
                  w_exp_out.reshape(N_EXPERTS, D_EXPERT, d))
    y_ctx, y_lat = _combine(dest, wts.T, x1, h_hi, mod4, w_shared_in.reshape(d, 2 * D_SHARED).astype(BF16),
                            w_shared_out.reshape(D_SHARED, d).astype(BF16), norm_final.reshape(1, d), ys,
                            n_ctx_tok, dec_seq)

    y_prompt = y_ctx.reshape(n_ctx, seq, d)
    y_sample = y_lat.reshape(n_lat, dec_seq, d)
    new_ckv = ckv[:n_ctx_tok].reshape(n_ctx, 1, seq, MLA_KV_RANK)
    new_krope = kpe_raw[:n_ctx_tok, :MLA_ROPE].reshape(n_ctx, 1, seq, MLA_ROPE)
    new_state = ret_state.reshape(n_ctx, 1, 2, RET_HEADS, RET_DK, RET_DV)
    return (y_prompt, y_sample, new_ckv, new_krope, new_state)
```

```python
import functools

import jax
import jax.numpy as jnp
from jax import lax
from jax.experimental import pallas as pl
from jax.experimental.pallas import tpu as pltpu

F32 = jnp.float32
BF16 = jnp.bfloat16
U32 = jnp.uint32
I32 = jnp.int32

D_MODEL = 2048
GRID_W = 64
NORM_EPS = 1e-6
RET_HEADS = 8
RET_DK = 128
RET_DV = 256
RET_CHUNK = 128
RET_QK_W = RET_HEADS * RET_DK
RET_V_W = RET_HEADS * RET_DV
MLA_HEADS = 16
MLA_Q_RANK = 512
MLA_KV_RANK = 512
MLA_NOPE = 128
MLA_ROPE = 64
MLA_V = 128
ROPE_BASE = 10000.0
N_EXPERTS = 256
TOP_K = 8
N_GROUPS = 8
TOPK_GROUPS = 4
GROUP_SIZE = N_EXPERTS // N_GROUPS
D_EXPERT = 512
D_SHARED = 512
ROUTED_SCALE = 2.5

LANES = 128
SUBLANES = 8
ROW_TILES = D_MODEL // (2 * LANES)
assert ROW_TILES == SUBLANES
Q_HEAD_W = 256
N_MOD = 6
MOD_ROWS = 16
TM_NORM = 256
EXPERT_BM = 256
MIB = 1 << 20


def _cparams(sem, vmem_mib=None):
    kw = dict(dimension_semantics=sem)
    if vmem_mib is not None:
        kw["vmem_limit_bytes"] = vmem_mib * MIB
    return pltpu.CompilerParams(**kw)


def _sigmoid(x):
    return 1.0 / (1.0 + jnp.exp(-x))


def _silu(x):
    return x * _sigmoid(x)


def _rms(x, w):
    ms = jnp.mean(x * x, axis=-1, keepdims=True)
    return x * lax.rsqrt(ms + NORM_EPS) * w


def _pack_bf16_pair(lo, hi):
    return pltpu.pack_elementwise([lo, hi], packed_dtype=BF16)


def _unpack_bf16_pair(w):
    lo = pltpu.unpack_elementwise(w, index=0, packed_dtype=BF16, unpacked_dtype=F32)
    hi = pltpu.unpack_elementwise(w, index=1, packed_dtype=BF16, unpacked_dtype=F32)
    return lo, hi


def _store_row_tiles(ref, x):
    m = x.shape[0]
    for s in range(ROW_TILES):
        lo = x[:, 2 * LANES * s:2 * LANES * s + LANES]
        hi = x[:, 2 * LANES * s + LANES:2 * LANES * (s + 1)]
        ref[pl.ds(s, m, SUBLANES), :] = _pack_bf16_pair(lo, hi)


def _load_row_tiles(ref, m):
    parts = []
    for s in range(ROW_TILES):
        parts.extend(_unpack_bf16_pair(ref[pl.ds(s, m, SUBLANES), :]))
    return parts


def _rope128(x, cos, sin):
    lane = lax.broadcasted_iota(I32, x.shape, 1)
    first = (lane & 31) < 16
    rot = jnp.where(first, -pltpu.roll(x, LANES - 16, 1), pltpu.roll(x, 16, 1))
    return x * cos + rot * sin


def _mod_kernel(c_ref, w_ref, b_ref, o_ref):
    s = _silu(c_ref[...]).astype(BF16)
    o_ref[...] = jnp.dot(s, w_ref[...].astype(BF16), preferred_element_type=F32) + b_ref[...]


def _modulation(cond, w_mod, b_mod):
    n = w_mod.shape[1]
    tn = 1024
    return pl.pallas_call(
        _mod_kernel,
        grid=(n // tn,),
        in_specs=[pl.BlockSpec((MOD_ROWS, D_MODEL), lambda j: (0, 0)),
                  pl.BlockSpec((D_MODEL, tn), lambda j: (0, j)),
                  pl.BlockSpec((1, tn), lambda j: (0, j))],
        out_specs=pl.BlockSpec((MOD_ROWS, tn), lambda j: (0, j)),
        out_shape=jax.ShapeDtypeStruct((MOD_ROWS, n), F32),
        compiler_params=_cparams(("arbitrary",), 40),
        name="modulation",
    )(cond, w_mod, b_mod)


def _mod_spec(which, tm, n_ctx_tok, dec_seq):
    def index(i, *_):
        tok = i * tm
        row = jnp.where(tok < n_ctx_tok, 0, 1 + (tok - n_ctx_tok) // dec_seq)
        return (row, which, 0, 0)
    return pl.BlockSpec((None, None, 1, D_MODEL), index)


def _path_specs(tm, width, n_ctx_tok, axis=0):
    n_ctx_tiles = n_ctx_tok // tm

    def ctx(*idx):
        return (jnp.minimum(idx[axis], n_ctx_tiles - 1), 0)

    def lat(*idx):
        return (jnp.maximum(idx[axis] - n_ctx_tiles, 0), 0)

    return pl.BlockSpec((tm, width), ctx), pl.BlockSpec((tm, width), lat)


def _path_pick(tile, tm, n_ctx_tok, ctx_ref, lat_ref):
    return jnp.where(tile * tm < n_ctx_tok, ctx_ref[...], lat_ref[...])


def _normmod_kernel(xc_ref, xl_ref, w_ref, sc_ref, sh_ref, o_ref, *, n_ctx_tok):
    x = _path_pick(pl.program_id(0), o_ref.shape[0], n_ctx_tok, xc_ref, xl_ref)
    y = _rms(x, w_ref[...])
    o_ref[...] = (y * (1.0 + sc_ref[...]) + sh_ref[...]).astype(BF16)


def _norm_modulate(x_ctx, x_lat, w, mod4, n_ctx_tok, dec_seq):
    t = x_ctx.shape[0] + x_lat.shape[0]
    tm = TM_NORM
    return pl.pallas_call(
        functools.partial(_normmod_kernel, n_ctx_tok=n_ctx_tok),
        grid=(t // tm,),
        in_specs=[*_path_specs(tm, D_MODEL, n_ctx_tok),
                  pl.BlockSpec((1, D_MODEL), lambda i: (0, 0)),
                  _mod_spec(1, tm, n_ctx_tok, dec_seq),
                  _mod_spec(0, tm, n_ctx_tok, dec_seq)],
        out_specs=pl.BlockSpec((tm, D_MODEL), lambda i: (i, 0)),
        out_shape=jax.ShapeDtypeStruct((t, D_MODEL), BF16),
        compiler_params=_cparams(("parallel",)),
        name="norm_modulate",
    )(x_ctx, x_lat, w, mod4, mod4)


def _mm_kernel(a_ref, w_ref, o_ref, *, act):
    acc = jnp.dot(a_ref[...].astype(BF16), w_ref[...], preferred_element_type=F32)
    if act == "sigmoid":
        acc = _sigmoid(acc)
    o_ref[...] = acc.astype(o_ref.dtype)


def _matmul(a, w, *, tm, tn, act=None, name):
    m, k = a.shape
    n = w.shape[1]
    tm = min(tm, m)
    assert m % tm == 0 and n % tn == 0
    return pl.pallas_call(
        functools.partial(_mm_kernel, act=act),
        grid=(n // tn, m // tm),
        in_specs=[pl.BlockSpec((tm, k), lambda j, i: (i, 0)),
                  pl.BlockSpec((k, tn), lambda j, i: (0, j))],
        out_specs=pl.BlockSpec((tm, tn), lambda j, i: (i, j)),
        out_shape=jax.ShapeDtypeStruct((m, n), BF16),
        compiler_params=_cparams(("parallel", "parallel"), 48),
        name=name,
    )(a, w)


def _lowrank_kernel(h_ref, w_ref, qnw_ref, kvnw_ref, cos_ref, sin_ref,
                    qn_ref, ckv_ref, ckvb_ref, kpe_ref, kper_ref):
    acc = jnp.dot(h_ref[...], w_ref[...], preferred_element_type=F32)
    qn_ref[...] = _rms(acc[:, :MLA_Q_RANK], qnw_ref[...]).astype(BF16)
    ckv = _rms(acc[:, MLA_Q_RANK:MLA_Q_RANK + MLA_KV_RANK], kvnw_ref[...])
    ckv_ref[...] = ckv
    ckvb_ref[...] = ckv.astype(BF16)
    kp = acc[:, MLA_Q_RANK + MLA_KV_RANK:]
    kpe_ref[...] = kp
    kper_ref[...] = _rope128(kp, cos_ref[...], sin_ref[...]).astype(BF16)


def _rope_spec(tm, n_ctx_tok, dec_seq):
    per_seq = dec_seq // tm

    def index(i, *_):
        tok = i * tm
        return (jnp.where(tok < n_ctx_tok, 0, 1 + ((tok - n_ctx_tok) // tm) % per_seq), 0)
    return pl.BlockSpec((tm, LANES), index)


def _lowrank(h, w_low, qnw, kvnw, cos_t, sin_t, n_ctx_tok, dec_seq):
    t = h.shape[0]
    tm = TM_NORM
    n = w_low.shape[1]
    row = lambda i: (i, 0)
    const = lambda i: (0, 0)
    return pl.pallas_call(
        _lowrank_kernel,
        grid=(t // tm,),
        in_specs=[pl.BlockSpec((tm, D_MODEL), row),
                  pl.BlockSpec((D_MODEL, n), const),
                  pl.BlockSpec((1, MLA_Q_RANK), const),
                  pl.BlockSpec((1, MLA_KV_RANK), const),
                  _rope_spec(tm, n_ctx_tok, dec_seq),
                  _rope_spec(tm, n_ctx_tok, dec_seq)],
        out_specs=[pl.BlockSpec((tm, MLA_Q_RANK), row),
                   pl.BlockSpec((tm, MLA_KV_RANK), row),
                   pl.BlockSpec((tm, MLA_KV_RANK), row),
                   pl.BlockSpec((tm, LANES), row),
                   pl.BlockSpec((tm, LANES), row)],
        out_shape=[jax.ShapeDtypeStruct((t, MLA_Q_RANK), BF16),
                   jax.ShapeDtypeStruct((t, MLA_KV_RANK), F32),
                   jax.ShapeDtypeStruct((t, MLA_KV_RANK), BF16),
                   jax.ShapeDtypeStruct((t, LANES), F32),
                   jax.ShapeDtypeStruct((t, LANES), BF16)],
        compiler_params=_cparams(("parallel",), 40),
        name="lowrank_proj",
    )(h, w_low, qnw, kvnw, cos_t, sin_t)


Q_PRESCALE = (MLA_NOPE + MLA_ROPE) ** -0.5 * 1.4426950408889634


def _qb_kernel(qn_ref, w_ref, cos_ref, sin_ref, o_ref):
    qn = qn_ref[...]
    cos = cos_ref[...]
    sin = sin_ref[...]
    for h in range(MLA_HEADS):
        lo = h * Q_HEAD_W
        acc = jnp.dot(qn, w_ref[:, lo:lo + Q_HEAD_W], preferred_element_type=F32) * Q_PRESCALE
        o_ref[:, lo:lo + MLA_NOPE] = acc[:, :MLA_NOPE].astype(BF16)
        o_ref[:, lo + MLA_NOPE:lo + Q_HEAD_W] = _rope128(acc[:, MLA_NOPE:], cos, sin).astype(BF16)


def _q_up(qn, w_qb, cos_t, sin_t, n_ctx_tok, dec_seq):
    t = qn.shape[0]
    tm = TM_NORM
    n = w_qb.shape[1]
    return pl.pallas_call(
        _qb_kernel,
        grid=(t // tm,),
        in_specs=[pl.BlockSpec((tm, MLA_Q_RANK), lambda i: (i, 0)),
                  pl.BlockSpec((MLA_Q_RANK, n), lambda i: (0, 0)),
                  _rope_spec(tm, n_ctx_tok, dec_seq),
                  _rope_spec(tm, n_ctx_tok, dec_seq)],
        out_specs=pl.BlockSpec((tm, n), lambda i: (i, 0)),
        out_shape=jax.ShapeDtypeStruct((t, n), BF16),
        compiler_params=_cparams(("parallel",), 40),
        name="q_up_rope",
    )(qn, w_qb, cos_t, sin_t)


def _ret_kernel(logit_ref, q_ref, k_ref, v_ref, g_ref, gnw_ref, *rest, n_chunks, group, has_s0, emit_state):
    rest = list(rest)
    s0_ref = rest.pop(0) if has_s0 else None
    o_ref = rest.pop(0)
    sfin_ref = rest.pop(0) if emit_state else None
    sf_scr, sb_scr = rest
    c_len = RET_CHUNK
    head = pl.program_id(1)

    def log_gamma(direction):
        l = jnp.full((c_len, 1), logit_ref[direction, head], F32)
        return jnp.minimum(l, 0.0) - jnp.log1p(jnp.exp(-jnp.abs(l)))

    lgf_c = log_gamma(0)
    lgb_c = log_gamma(1)
    ii = lax.broadcasted_iota(I32, (c_len, c_len), 0).astype(F32)
    jj = lax.broadcasted_iota(I32, (c_len, c_len), 1).astype(F32)
    k_scale = RET_DK ** -0.5
    decay = (jnp.where(ii >= jj, jnp.exp(jnp.maximum(ii - jj, 0.0) * lgf_c), 0.0)
             + jnp.where(jj >= ii, jnp.exp(jnp.maximum(jj - ii, 0.0) * lgb_c), 0.0)) * k_scale
    pos = lax.broadcasted_iota(I32, (c_len, 1), 0).astype(F32)
    qdec_f = jnp.exp((pos + 1.0) * lgf_c)
    qdec_b = jnp.exp((c_len - pos) * lgb_c)
    kdec_f = jnp.exp((c_len - 1.0 - pos) * lgf_c) * k_scale
    kdec_b = jnp.exp(pos * lgb_c) * k_scale
    cdec_f = jnp.exp(c_len * lgf_c)
    cdec_b = jnp.exp(c_len * lgb_c)
    tn_dims = (((0,), (0,)), ((), ()))
    nt_dims = (((1,), (1,)), ((), ()))

    def chunk(c):
        return pl.ds(c * c_len, c_len)

    for c in range(n_chunks):
        kc = k_ref[chunk(c), :].astype(F32)
        vc = v_ref[chunk(c), :]
        sf_scr[c] = lax.dot_general((kc * kdec_f).astype(BF16), vc, tn_dims, preferred_element_type=F32)
        sb_scr[c] = lax.dot_general((kc * kdec_b).astype(BF16), vc, tn_dims, preferred_element_type=F32)

    per_seq = n_chunks // group
    for g in range(group):
        chunks = range(g * per_seq, (g + 1) * per_seq)
        s = s0_ref[g, 0] if has_s0 else jnp.zeros((RET_DK, RET_DV), F32)
        for c in chunks:
            kv = sf_scr[c]
            sf_scr[c] = s
            s = cdec_f * s + kv
        if emit_state:
            sfin_ref[g, 0] = s
        s = s0_ref[g, 1] if has_s0 else jnp.zeros((RET_DK, RET_DV), F32)
        for c in reversed(chunks):
            kv = sb_scr[c]
            sb_scr[c] = s
            s = cdec_b * s + kv
        if emit_state:
            sfin_ref[g, 1] = s

    gnw = gnw_ref[...]
    for c in range(n_chunks):
        qc = q_ref[chunk(c), :]
        kc = k_ref[chunk(c), :]
        vc = v_ref[chunk(c), :]
        sc = lax.dot_general(qc, kc, nt_dims, preferred_element_type=F32) * decay
        qf = qc.astype(F32)
        qcat = jnp.concatenate([(qf * qdec_f).astype(BF16), (qf * qdec_b).astype(BF16)], axis=1)
        scat = jnp.concatenate([sf_scr[c], sb_scr[c]], axis=0).astype(BF16)
        o = (jnp.dot(sc.astype(BF16), vc, preferred_element_type=F32)
             + jnp.dot(qcat, scat, preferred_element_type=F32))
        mu = jnp.mean(o, axis=-1, keepdims=True)
        var = jnp.mean(jnp.square(o - mu), axis=-1, keepdims=True)
        on = (o - mu) * lax.rsqrt(var + NORM_EPS) * gnw
        o_ref[chunk(c), :] = (_silu(g_ref[chunk(c), :].astype(F32)) * on).astype(BF16)


def _retention(qkvg, logits, gn_w, s0, *, n_seq, seq, group, row_block0, emit_state):
    rows = group * seq
    n_chunks = rows // RET_CHUNK
    has_s0 = s0 is not None
    assert n_seq % group == 0
    k_off = RET_QK_W // RET_DK
    v_off = 2 * RET_QK_W // RET_DV
    g_off = (2 * RET_QK_W + RET_V_W) // RET_DV
    in_specs = [pl.BlockSpec(memory_space=pltpu.SMEM),
                pl.BlockSpec((rows, RET_DK), lambda b, h: (row_block0 + b, h)),
                pl.BlockSpec((rows, RET_DK), lambda b, h: (row_block0 + b, k_off + h)),
                pl.BlockSpec((rows, RET_DV), lambda b, h: (row_block0 + b, v_off + h)),
                pl.BlockSpec((rows, RET_DV), lambda b, h: (row_block0 + b, g_off + h)),
                pl.BlockSpec((1, RET_DV), lambda b, h: (0, h))]
    args = [logits, qkvg, qkvg, qkvg, qkvg, gn_w]
    state_spec = pl.BlockSpec((group, 2, None, RET_DK, RET_DV), lambda b, h: (b, 0, h, 0, 0))
    if has_s0:
        in_specs.append(state_spec)
        args.append(s0)
    out_specs = [pl.BlockSpec((rows, RET_DV), lambda b, h: (b, h))]
    out_shape = [jax.ShapeDtypeStruct((n_seq * seq, RET_V_W), BF16)]
    if emit_state:
        out_specs.append(state_spec)
        out_shape.append(jax.ShapeDtypeStruct((n_seq, 2, RET_HEADS, RET_DK, RET_DV), F32))
    res = pl.pallas_call(
        functools.partial(_ret_kernel, n_chunks=n_chunks, group=group, has_s0=has_s0, emit_state=emit_state),
        grid=(n_seq // group, RET_HEADS),
        in_specs=in_specs,
        out_specs=out_specs,
        out_shape=out_shape,
        scratch_shapes=[pltpu.VMEM((n_chunks, RET_DK, RET_DV), F32),
                        pltpu.VMEM((n_chunks, RET_DK, RET_DV), F32)],
        compiler_params=_cparams(("parallel", "parallel")),
        name="retention_seq%d" % seq,
    )(*args)
    return res


V_EXT_W = MLA_V + LANES


def _ones_column(rows):
    lane = lax.broadcasted_iota(I32, (rows, LANES), 1)
    return jnp.where(lane == 0, 1.0, 0.0).astype(BF16)


def _softmax_pv(q, kcat, v_ext):
    s = lax.dot_general(q, kcat, (((1,), (1,)), ((), ())), preferred_element_type=F32)
    e = jnp.exp2(s - jnp.max(s, axis=-1, keepdims=True))
    o = jnp.dot(e.astype(BF16), v_ext, preferred_element_type=F32)
    return o[:, :MLA_V] / o[:, MLA_V:MLA_V + 1]


def _attn_ctx_kernel(q_ref, kv_ref, kpe_ref, o_ref):
    kpe = kpe_ref[...]
    ones = _ones_column(kpe.shape[0])
    kvw = MLA_NOPE + MLA_V
    for h in range(MLA_HEADS):
        kv = kv_ref[:, h * kvw:(h + 1) * kvw]
        kcat = jnp.concatenate([kv[:, :MLA_NOPE], kpe], axis=1)
        v_ext = jnp.concatenate([kv[:, MLA_NOPE:], ones], axis=1)
        o = _softmax_pv(q_ref[:, h * Q_HEAD_W:(h + 1) * Q_HEAD_W], kcat, v_ext)
        o_ref[:, h * MLA_V:(h + 1) * MLA_V] = o.astype(o_ref.dtype)


def _attention_ctx(q_all, kv_all, kpe_all, n_seq, seq):
    return pl.pallas_call(
        _attn_ctx_kernel,
        grid=(n_seq,),
        in_specs=[pl.BlockSpec((seq, MLA_HEADS * Q_HEAD_W), lambda b: (b, 0)),
                  pl.BlockSpec((seq, MLA_HEADS * (MLA_NOPE + MLA_V)), lambda b: (b, 0)),
                  pl.BlockSpec((seq, LANES), lambda b: (b, 0))],
        out_specs=pl.BlockSpec((seq, MLA_HEADS * MLA_V), lambda b: (b, 0)),
        out_shape=jax.ShapeDtypeStruct((n_seq * seq, MLA_HEADS * MLA_V), BF16),
        compiler_params=_cparams(("parallel",), 40),
        name="attention_ctx",
    )(q_all, kv_all, kpe_all)


def _attn_lat_kernel(q_ref, kvc_ref, kvl_ref, kpc_ref, kpl_ref, o_ref, kcat, vcat, *, past, seq):
    @pl.when(pl.program_id(2) == 0)
    def _():
        kcat[0:past, 0:MLA_NOPE] = kvc_ref[:, :MLA_NOPE]
        kcat[0:past, MLA_NOPE:] = kpc_ref[...]
        kcat[past:past + seq, 0:MLA_NOPE] = kvl_ref[:, :MLA_NOPE]
        kcat[past:past + seq, MLA_NOPE:] = kpl_ref[...]
        vcat[0:past, 0:MLA_V] = kvc_ref[:, MLA_NOPE:]
        vcat[past:past + seq, 0:MLA_V] = kvl_ref[:, MLA_NOPE:]
        vcat[:, MLA_V:] = _ones_column(past + seq)

    sub = 256
    for r in range(q_ref.shape[0] // sub):
        rows = slice(r * sub, (r + 1) * sub)
        o_ref[rows, :] = _softmax_pv(q_ref[rows, :], kcat[...], vcat[...]).astype(o_ref.dtype)


def _attention_lat(q_all, kv_ctx, kv_all, kpe_ctx, kpe_all, n_seq, seq, past, n_ctx_tok):
    tq = 1024
    nq = seq // tq
    q0 = n_ctx_tok // tq
    s0 = n_ctx_tok // seq
    kvw = MLA_NOPE + MLA_V
    return pl.pallas_call(
        functools.partial(_attn_lat_kernel, past=past, seq=seq),
        grid=(n_seq, MLA_HEADS, nq),
        in_specs=[pl.BlockSpec((tq, Q_HEAD_W), lambda b, h, i: (q0 + b * nq + i, h)),
                  pl.BlockSpec((past, kvw), lambda b, h, i: (b, h)),
                  pl.BlockSpec((seq, kvw), lambda b, h, i: (s0 + b, h)),
                  pl.BlockSpec((past, LANES), lambda b, h, i: (b, 0)),
                  pl.BlockSpec((seq, LANES), lambda b, h, i: (s0 + b, 0))],
        out_specs=pl.BlockSpec((tq, MLA_V), lambda b, h, i: (b * nq + i, h)),
        out_shape=jax.ShapeDtypeStruct((n_seq * seq, MLA_HEADS * MLA_V), BF16),
        scratch_shapes=[pltpu.VMEM((past + seq, MLA_NOPE + LANES), BF16),
                        pltpu.VMEM((past + seq, V_EXT_W), BF16)],
        compiler_params=_cparams(("parallel", "parallel", "arbitrary"), 40),
        name="attention_lat",
    )(q_all, kv_ctx, kv_all, kpe_ctx, kpe_all)


def _merge_kernel(rc_ref, rl_ref, ac_ref, al_ref, wr_ref, wm_ref, gr_ref, gm_ref, o_ref, *, n_ctx_tok):
    tile = pl.program_id(1)
    tm = o_ref.shape[0]
    r = _path_pick(tile, tm, n_ctx_tok, rc_ref, rl_ref)
    a = _path_pick(tile, tm, n_ctx_tok, ac_ref, al_ref)
    ro = jnp.dot(r, wr_ref[...], preferred_element_type=F32)
    mo = jnp.dot(a, wm_ref[...], preferred_element_type=F32)
    o_ref[...] = (gr_ref[...].astype(F32) * ro + gm_ref[...].astype(F32) * mo).astype(BF16)


def _merge(r_ctx, r_lat, a_ctx, a_lat, w_ret_o, w_mla_o, gates, n_ctx_tok):
    t = gates.shape[0]
    tm, tn = 512, 1024
    nj = D_MODEL // tn
    return pl.pallas_call(
        functools.partial(_merge_kernel, n_ctx_tok=n_ctx_tok),
        grid=(nj, t // tm),
        in_specs=[*_path_specs(tm, RET_V_W, n_ctx_tok, axis=1),
                  *_path_specs(tm, MLA_HEADS * MLA_V, n_ctx_tok, axis=1),
                  pl.BlockSpec((RET_V_W, tn), lambda j, i: (0, j)),
                  pl.BlockSpec((MLA_HEADS * MLA_V, tn), lambda j, i: (0, j)),
                  pl.BlockSpec((tm, tn), lambda j, i: (i, j)),
                  pl.BlockSpec((tm, tn), lambda j, i: (i, nj + j))],
        out_specs=pl.BlockSpec((tm, tn), lambda j, i: (i, j)),
        out_shape=jax.ShapeDtypeStruct((t, D_MODEL), BF16),
        compiler_params=_cparams(("parallel", "parallel"), 48),
        name="branch_merge",
    )(r_ctx, r_lat, a_ctx, a_lat, w_ret_o, w_mla_o, gates, gates)


def _outproj_kernel(m_ref, w_ref, xc_ref, xl_ref, g1_ref, nw_ref, sc_ref, sh_ref,
                    x1_ref, hhi_ref, hlo_ref, hp_ref, *, n_ctx_tok):
    mix = jnp.dot(m_ref[...], w_ref[...], preferred_element_type=F32)
    x = _path_pick(pl.program_id(0), x1_ref.shape[0], n_ctx_tok, xc_ref, xl_ref)
    x1 = x + g1_ref[...] * mix
    x1_ref[...] = x1
    h2 = _rms(x1, nw_ref[...]) * (1.0 + sc_ref[...]) + sh_ref[...]
    hi = h2.astype(BF16)
    hif = hi.astype(F32)
    hhi_ref[...] = hi
    hlo_ref[...] = (h2 - hif).astype(BF16)
    _store_row_tiles(hp_ref, hif)


def _out_proj(merged, w_out, x_ctx, x_lat, norm_w, mod4, n_ctx_tok, dec_seq):
    t = merged.shape[0]
    tm = TM_NORM
    row = lambda i: (i, 0)
    const = lambda i: (0, 0)
    return pl.pallas_call(
        functools.partial(_outproj_kernel, n_ctx_tok=n_ctx_tok),
        grid=(t // tm,),
        in_specs=[pl.BlockSpec((tm, D_MODEL), row),
                  pl.BlockSpec((D_MODEL, D_MODEL), const),
                  *_path_specs(tm, D_MODEL, n_ctx_tok),
                  _mod_spec(2, tm, n_ctx_tok, dec_seq),
                  pl.BlockSpec((1, D_MODEL), const),
                  _mod_spec(4, tm, n_ctx_tok, dec_seq),
                  _mod_spec(3, tm, n_ctx_tok, dec_seq)],
        out_specs=[pl.BlockSpec((tm, D_MODEL), row),
                   pl.BlockSpec((tm, D_MODEL), row),
                   pl.BlockSpec((tm, D_MODEL), row),
                   pl.BlockSpec((tm * ROW_TILES, LANES), row)],
        out_shape=[jax.ShapeDtypeStruct((t, D_MODEL), F32),
                   jax.ShapeDtypeStruct((t, D_MODEL), BF16),
                   jax.ShapeDtypeStruct((t, D_MODEL), BF16),
                   jax.ShapeDtypeStruct((t * ROW_TILES, LANES), U32)],
        compiler_params=_cparams(("parallel",), 48),
        name="out_proj_norm",
    )(merged, w_out, x_ctx, x_lat, mod4, norm_w, mod4, mod4)


def _router_kernel(hhi_ref, hlo_ref, wrt_ref, bias_ref, eidx_ref, wts_ref, rank_ref, cnt_ref, carry):
    tm = hhi_ref.shape[0]
    n_e = N_EXPERTS

    @pl.when(pl.program_id(0) == 0)
    def _():
        carry[...] = jnp.zeros_like(carry)

    wr = wrt_ref[...]
    wr_hi = wr.astype(BF16)
    wr_lo = (wr - wr_hi.astype(F32)).astype(BF16)
    hh = hhi_ref[...]
    hl = hlo_ref[...]
    nt_dims = (((1,), (1,)), ((), ()))
    logits = (lax.dot_general(wr_hi, hh, nt_dims, preferred_element_type=F32)
              + lax.dot_general(wr_hi, hl, nt_dims, preferred_element_type=F32)
              + lax.dot_general(wr_lo, hh, nt_dims, preferred_element_type=F32))
    scores = _sigmoid(logits)
    choice = scores + bias_ref[...]
    neg = -jnp.inf
    row = lax.broadcasted_iota(I32, (n_e, tm), 0).astype(F32)
    grow = lax.broadcasted_iota(I32, (GROUP_SIZE, tm), 0).astype(F32)

    gs = []
    for g in range(N_GROUPS):
        blk = choice[g * GROUP_SIZE:(g + 1) * GROUP_SIZE]
        m1 = jnp.max(blk, axis=0, keepdims=True)
        i1 = jnp.min(jnp.where(blk == m1, grow, float(GROUP_SIZE)), axis=0, keepdims=True)
        m2 = jnp.max(jnp.where(grow == i1, neg, blk), axis=0, keepdims=True)
        gs.append(m1 + m2)
    gsc = jnp.concatenate(gs, axis=0)
    gid = lax.broadcasted_iota(I32, (N_GROUPS, tm), 0).astype(F32)
    gsel = jnp.zeros((N_GROUPS, tm), F32)
    for _ in range(TOPK_GROUPS):
        m = jnp.max(gsc, axis=0, keepdims=True)
        idx = jnp.min(jnp.where(gsc == m, gid, float(N_GROUPS)), axis=0, keepdims=True)
        hit = gid == idx
        gsel = jnp.where(hit, 1.0, gsel)
        gsc = jnp.where(hit, neg, gsc)
    masked = jnp.concatenate(
        [jnp.where(gsel[g:g + 1] > 0.0, choice[g * GROUP_SIZE:(g + 1) * GROUP_SIZE], neg)
         for g in range(N_GROUPS)], axis=0)

    idxs, ws = [], []
    onehot = jnp.zeros((n_e, tm), F32)
    for _ in range(TOP_K):
        m = jnp.max(masked, axis=0, keepdims=True)
        idx = jnp.min(jnp.where(masked == m, row, float(n_e)), axis=0, keepdims=True)
        hit = row == idx
        ws.append(jnp.sum(jnp.where(hit, scores, 0.0), axis=0, keepdims=True))
        masked = jnp.where(hit, neg, masked)
        onehot = jnp.where(hit, 1.0, onehot)
        idxs.append(idx)
    w = jnp.concatenate(ws, axis=0)
    wts_ref[...] = w / jnp.sum(w, axis=0, keepdims=True) * ROUTED_SCALE
    eidx_ref[...] = jnp.concatenate(idxs, axis=0).astype(I32)

    t_src = lax.broadcasted_iota(I32, (tm, tm), 0)
    t_dst = lax.broadcasted_iota(I32, (tm, tm), 1)
    before = jnp.where(t_src < t_dst, 1.0, 0.0).astype(BF16)
    prior = jnp.dot(onehot.astype(BF16), before, preferred_element_type=F32) + carry[:, :1]
    rank_ref[...] = jnp.concatenate(
        [jnp.sum(jnp.where(row == idx, prior, 0.0), axis=0, keepdims=True) for idx in idxs],
        axis=0).astype(I32)
    carry[...] = carry[...] + jnp.sum(onehot, axis=1, keepdims=True)
    cnt_ref[...] = carry[...]


def _router(h_hi, h_lo, wr_t, bias_col):
    t = h_hi.shape[0]
    tm = 512
    tok = lambda i: (0, i)
    const = lambda i: (0, 0)
    return pl.pallas_call(
        _router_kernel,
        grid=(t // tm,),
        in_specs=[pl.BlockSpec((tm, D_MODEL), lambda i: (i, 0)),
                  pl.BlockSpec((tm, D_MODEL), lambda i: (i, 0)),
                  pl.BlockSpec((N_EXPERTS, D_MODEL), const),
                  pl.BlockSpec((N_EXPERTS, 1), const)],
        out_specs=[pl.BlockSpec((TOP_K, tm), tok),
                   pl.BlockSpec((TOP_K, tm), tok),
                   pl.BlockSpec((TOP_K, tm), tok),
                   pl.BlockSpec((N_EXPERTS, LANES), const)],
        out_shape=[jax.ShapeDtypeStruct((TOP_K, t), I32),
                   jax.ShapeDtypeStruct((TOP_K, t), F32),
                   jax.ShapeDtypeStruct((TOP_K, t), I32),
                   jax.ShapeDtypeStruct((N_EXPERTS, LANES), F32)],
        scratch_shapes=[pltpu.VMEM((N_EXPERTS, LANES), F32)],
        compiler_params=_cparams(("arbitrary",), 40),
        name="router_topk",
    )(h_hi, h_lo, wr_t, bias_col)


def _dest_kernel(eidx_ref, rank_ref, pstart_ref, dest_ref):
    tm = eidx_ref.shape[1]
    row = lax.broadcasted_iota(I32, (N_EXPERTS, tm), 0)
    pstart = pstart_ref[...]
    eidx = eidx_ref[...]
    base = jnp.concatenate(
        [jnp.sum(jnp.where(row == eidx[k:k + 1], pstart, 0.0), axis=0, keepdims=True)
         for k in range(TOP_K)], axis=0)
    dest_ref[...] = (base.astype(I32) + rank_ref[...]) * ROW_TILES


def _dest_rows(eidx, rank, pstart_col):
    t = eidx.shape[1]
    tm = 1024
    tok = lambda i: (0, i)
    return pl.pallas_call(
        _dest_kernel,
        grid=(t // tm,),
        in_specs=[pl.BlockSpec((TOP_K, tm), tok),
                  pl.BlockSpec((TOP_K, tm), tok),
                  pl.BlockSpec((N_EXPERTS, 1), lambda i: (0, 0))],
        out_specs=pl.BlockSpec((TOP_K, tm), tok),
        out_shape=jax.ShapeDtypeStruct((TOP_K, t), I32),
        compiler_params=_cparams(("parallel",)),
        name="dest_rows",
    )(eidx, rank, pstart_col)


def _dispatch_kernel(fill_ref, meta_ref, dest_ref, hp_ref, xs_ref, zbuf, sem, fill_sem):
    tm = hp_ref.shape[0]
    block_rows = zbuf.shape[0]
    n_blocks = xs_ref.shape[0] // block_rows

    def zero_fills(start):
        def run(cp):
            cp.start() if start else cp.wait()

        def per_expert(e, c):
            row = fill_ref[0, e]
            left = fill_ref[1, e]
            p = EXPERT_BM // 2
            while p >= 1:
                take = left & p

                @pl.when(take != 0)
                def _(row=row, p=p):
                    dst = xs_ref.at[pl.ds(pl.multiple_of(row * ROW_TILES, ROW_TILES), p * ROW_TILES)]
                    run(pltpu.make_async_copy(zbuf.at[pl.ds(0, p * ROW_TILES)], dst, fill_sem))

                row = row + take
                p //= 2
            return c

        def per_tail_block(b, c):
            dst = xs_ref.at[pl.ds(pl.multiple_of(b * block_rows, block_rows), block_rows)]
            run(pltpu.make_async_copy(zbuf, dst, fill_sem))
            return c

        lax.fori_loop(0, N_EXPERTS, per_expert, 0)
        lax.fori_loop(meta_ref[0], n_blocks, per_tail_block, 0)

    @pl.when(pl.program_id(0) == 0)
    def _():
        zbuf[...] = jnp.zeros_like(zbuf)
        zero_fills(True)

    def issue(t, c):
        src = hp_ref.at[pl.ds(pl.multiple_of(t * ROW_TILES, ROW_TILES), ROW_TILES)]
        for k in range(TOP_K):
            r = pl.multiple_of(dest_ref[t * TOP_K + k], ROW_TILES)
            pltpu.make_async_copy(src, xs_ref.at[pl.ds(r, ROW_TILES)], sem).start(priority=k % 2)
        return c

    lax.fori_loop(0, tm // ROW_TILES, issue, 0)
    for _ in range(TOP_K):
        pltpu.make_async_copy(hp_ref, xs_ref.at[pl.ds(0, tm)], sem).wait()

    @pl.when(pl.program_id(0) == 0)
    def _():
        zero_fills(False)


def _dispatch(fill, meta, dest, hp, n_blocks):
    tm = TM_NORM * ROW_TILES
    block_rows = EXPERT_BM * ROW_TILES
    return pl.pallas_call(
        _dispatch_kernel,
        grid=(hp.shape[0] // tm,),
        in_specs=[pl.BlockSpec(memory_space=pltpu.SMEM),
                  pl.BlockSpec(memory_space=pltpu.SMEM),
                  pl.BlockSpec((TM_NORM * TOP_K,), lambda i: (i,), memory_space=pltpu.SMEM),
                  pl.BlockSpec((tm, LANES), lambda i: (i, 0))],
        out_specs=pl.BlockSpec(memory_space=pl.ANY),
        out_shape=jax.ShapeDtypeStruct((n_blocks * block_rows, LANES), U32),
        scratch_shapes=[pltpu.VMEM((block_rows, LANES), U32),
                        pltpu.SemaphoreType.DMA(()),
                        pltpu.SemaphoreType.DMA(())],
        compiler_params=_cparams(("arbitrary",)),
        name="moe_dispatch",
    )(fill, meta, dest, hp)


N_WSLOTS = 2


def _expert_kernel(uidx_ref, ue_ref, meta_ref, xs_ref, wi_hbm, wo_hbm, ys_ref,
                   wi_st, wo_st, wi_s, wo_s, sem):
    b = pl.program_id(0)
    n_used = meta_ref[0]
    n_ue = meta_ref[1]

    def weight_copies(j, slot):
        e = ue_ref[j]
        copies = []
        for m, (hbm, st) in enumerate(((wi_hbm, wi_st), (wo_hbm, wo_st))):
            half = hbm.shape[1] // 2
            for q in range(2):
                rows = pl.ds(q * half, half)
                copies.append((pltpu.make_async_copy(hbm.at[e, rows], st.at[slot, rows], sem.at[2 * m + q, slot]), q))
        return copies

    @pl.when(b < n_used)
    def _():
        j = uidx_ref[b]

        @pl.when(b == 0)
        def _():
            for cp, prio in weight_copies(0, 0):
                cp.start(priority=prio)

            @pl.when(n_ue > 1)
            def _():
                for cp, prio in weight_copies(1, 1):
                    cp.start(priority=prio)

        @pl.when((b == 0) | (j != uidx_ref[jnp.maximum(b - 1, 0)]))
        def _():
            slot = j % N_WSLOTS
            for cp, _ in weight_copies(j, slot):
                cp.wait()
            rows = 256

            def cast_in(i, c):
                r = pl.multiple_of(i * rows, rows)
                wi_s[pl.ds(r, rows), :] = wi_st[slot, pl.ds(r, rows), :].astype(BF16)
                return c

            def cast_out(i, c):
                r = pl.multiple_of(i * rows, rows)
                wo_s[pl.ds(r, rows), :] = wo_st[slot, pl.ds(r, rows), :].astype(BF16)
                return c

            lax.fori_loop(0, D_MODEL // rows, cast_in, 0)
            lax.fori_loop(0, D_EXPERT // rows, cast_out, 0)

            @pl.when(j + N_WSLOTS < n_ue)
            def _():
                for cp, prio in weight_copies(j + N_WSLOTS, slot):
                    cp.start(priority=prio)

        bm = EXPERT_BM
        x = jnp.concatenate([p.astype(BF16) for p in _load_row_tiles(xs_ref, bm)], axis=1)
        gu = jnp.dot(x, wi_s[...], preferred_element_type=F32)
        act = (_silu(gu[:, :D_EXPERT]) * gu[:, D_EXPERT:]).astype(BF16)
        _store_row_tiles(ys_ref, jnp.dot(act, wo_s[...], preferred_element_type=F32))


def _experts(uidx, ue, meta, xs, w_exp_in, w_exp_out):
    rows = EXPERT_BM * ROW_TILES
    nb = xs.shape[0] // rows
    live = lambda b, uidx, ue, meta: (jnp.minimum(b, meta[0] - 1), 0)
    grid_spec = pltpu.PrefetchScalarGridSpec(
        num_scalar_prefetch=3,
        grid=(nb,),
        in_specs=[pl.BlockSpec((rows, LANES), live),
                  pl.BlockSpec(memory_space=pl.ANY),
                  pl.BlockSpec(memory_space=pl.ANY)],
        out_specs=pl.BlockSpec((rows, LANES), live),
        scratch_shapes=[pltpu.VMEM((N_WSLOTS, D_MODEL, 2 * D_EXPERT), F32),
                        pltpu.VMEM((N_WSLOTS, D_EXPERT, D_MODEL), F32),
                        pltpu.VMEM((D_MODEL, 2 * D_EXPERT), BF16),
                        pltpu.VMEM((D_EXPERT, D_MODEL), BF16),
                        pltpu.SemaphoreType.DMA((4, N_WSLOTS))],
    )
    return pl.pallas_call(
        _expert_kernel,
        grid_spec=grid_spec,
        out_shape=jax.ShapeDtypeStruct(xs.shape, U32),
        input_output_aliases={3: 0},
        compiler_params=_cparams(("arbitrary",), 52),
        name="routed_experts",
    )(uidx, ue, meta, xs, w_exp_in, w_exp_out)


def _combine_kernel(dest_ref, dnext_ref, wt_ref, x1_ref, hhi_ref, g2_ref, wsi_ref, wso_ref, nf_ref, ys_ref,
                    yc_ref, yl_ref, rows, moe_scr, sem, *, n_ctx_tok):
    tm = x1_ref.shape[0]
    i = pl.program_id(0)
    n = pl.num_programs(0)

    def gather(idx_ref, slot):
        def issue(t, c):
            for k in range(TOP_K):
                r = pl.multiple_of(idx_ref[t * TOP_K + k], ROW_TILES)
                dst = rows.at[slot, k, pl.ds(pl.multiple_of(t * ROW_TILES, ROW_TILES), ROW_TILES)]
                pltpu.make_async_copy(ys_ref.at[pl.ds(r, ROW_TILES)], dst, sem.at[slot]).start(priority=k % 2)
            return c

        lax.fori_loop(0, tm, issue, 0)

    @pl.when(i == 0)
    def _():
        gather(dest_ref, 0)

    @pl.when(i + 1 < n)
    def _():
        gather(dnext_ref, (i + 1) % 2)

    su = jnp.dot(hhi_ref[...], wsi_ref[...], preferred_element_type=F32)
    sact = (_silu(su[:, :D_SHARED]) * su[:, D_SHARED:]).astype(BF16)
    shared = jnp.dot(sact, wso_ref[...], preferred_element_type=F32)

    slot = i % 2
    for k in range(TOP_K):
        pltpu.make_async_copy(ys_ref.at[pl.ds(0, tm * ROW_TILES)], rows.at[slot, k], sem.at[slot]).wait()

    sub = 128
    for r0 in range(0, tm, sub):
        wks = [wt_ref[r0:r0 + sub, k:k + 1] for k in range(TOP_K)]
        for s in range(ROW_TILES):
            acc_lo = acc_hi = None
            for k in range(TOP_K):
                lo, hi = _unpack_bf16_pair(rows[slot, k, pl.ds(r0 * ROW_TILES + s, sub, SUBLANES), :])
                acc_lo = wks[k] * lo if acc_lo is None else acc_lo + wks[k] * lo
                acc_hi = wks[k] * hi if acc_hi is None else acc_hi + wks[k] * hi
            col = 2 * LANES * s
            moe_scr[r0:r0 + sub, col:col + LANES] = acc_lo
            moe_scr[r0:r0 + sub, col + LANES:col + 2 * LANES] = acc_hi
    moe = moe_scr[...] + shared
    x2 = x1_ref[...] + g2_ref[...] * moe
    y = _rms(x2, nf_ref[...])
    is_ctx = i * tm < n_ctx_tok

    @pl.when(is_ctx)
    def _():
        yc_ref[...] = y

    @pl.when(jnp.logical_not(is_ctx))
    def _():
        yl_ref[...] = y


def _combine(dest, wts_t, x1, h_hi, mod4, w_sh_in, w_sh_out, norm_final, ys, n_ctx_tok, dec_seq):
    t = x1.shape[0]
    tm = TM_NORM
    n = t // tm
    row = lambda i: (i, 0)
    const = lambda i: (0, 0)
    return pl.pallas_call(
        functools.partial(_combine_kernel, n_ctx_tok=n_ctx_tok),
        grid=(n,),
        in_specs=[pl.BlockSpec((tm * TOP_K,), lambda i: (i,), memory_space=pltpu.SMEM),
                  pl.BlockSpec((tm * TOP_K,), lambda i: (jnp.minimum(i + 1, n - 1),), memory_space=pltpu.SMEM),
                  pl.BlockSpec((tm, TOP_K), row),
                  pl.BlockSpec((tm, D_MODEL), row),
                  pl.BlockSpec((tm, D_MODEL), row),
                  _mod_spec(5, tm, n_ctx_tok, dec_seq),
                  pl.BlockSpec((D_MODEL, 2 * D_SHARED), const),
                  pl.BlockSpec((D_SHARED, D_MODEL), const),
                  pl.BlockSpec((1, D_MODEL), const),
                  pl.BlockSpec(memory_space=pl.ANY)],
        out_specs=list(_path_specs(tm, D_MODEL, n_ctx_tok)),
        out_shape=[jax.ShapeDtypeStruct((n_ctx_tok, D_MODEL), F32),
                   jax.ShapeDtypeStruct((t - n_ctx_tok, D_MODEL), F32)],
        scratch_shapes=[pltpu.VMEM((2, TOP_K, tm * ROW_TILES, LANES), U32),
                        pltpu.VMEM((tm, D_MODEL), F32),
                        pltpu.SemaphoreType.DMA((2,))],
        compiler_params=_cparams(("arbitrary",), 56),
        name="moe_combine",
    )(dest, dest, wts_t, x1, h_hi, mod4, w_sh_in, w_sh_out, norm_final, ys)


def _rope_tables(tm, dec_seq):
    rows = dec_seq // GRID_W
    row = jnp.broadcast_to(jnp.arange(rows, dtype=F32)[:, None], (rows, GRID_W)).reshape(-1)
    col = jnp.broadcast_to(jnp.arange(GRID_W, dtype=F32)[None, :], (rows, GRID_W)).reshape(-1)
    axis_dim = MLA_ROPE // 2
    freqs = jnp.power(ROPE_BASE, -jnp.arange(0, axis_dim, 2, dtype=F32) / axis_dim)
    ang_r = row[:, None] * freqs
    ang_c = col[:, None] * freqs
    ang = jnp.concatenate([ang_r, ang_r, ang_c, ang_c], axis=-1)
    pad = LANES - MLA_ROPE
    cos = jnp.concatenate([jnp.cos(ang), jnp.ones((dec_seq, pad), F32)], axis=1)
    sin = jnp.concatenate([jnp.sin(ang), jnp.zeros((dec_seq, pad), F32)], axis=1)
    cos = jnp.concatenate([jnp.ones((tm, LANES), F32), cos], axis=0)
    sin = jnp.concatenate([jnp.zeros((tm, LANES), F32), sin], axis=0)
    return cos, sin


def _block_plan(counts, n_blocks):
    bm = EXPERT_BM
    nblk = (counts + bm - 1) // bm
    bend = jnp.cumsum(nblk)
    bstart = bend - nblk
    present = nblk > 0
    pcum = jnp.cumsum(present.astype(I32))
    bidx = jnp.arange(n_blocks, dtype=I32)
    uidx = jnp.sum((present[None, :] & (bstart[None, :] <= bidx[:, None])).astype(I32), axis=1) - 1
    order = jnp.arange(N_EXPERTS, dtype=I32)
    ue = jnp.minimum(jnp.sum((pcum[None, :] <= order[:, None]).astype(I32), axis=1), N_EXPERTS - 1)
    meta = jnp.stack([bend[-1], pcum[-1]]).astype(I32)
    pstart = bstart * bm
    fill = jnp.stack([pstart + counts, nblk * bm - counts]).astype(I32)
    return pstart, uidx.astype(I32), ue.astype(I32), meta, fill


def kernel(x_prompt, x_sample, c, cache_mla_ckv, cache_mla_krope, state_ret, c_ctx, w_mod, b_mod, norm_mix, norm_ffn, w_in, ret_decay_logit, ret_gn_w, w_ret_o, mla_q_norm, w_q_b, mla_kv_norm, w_kv_b, w_mla_o, w_out, w_router, router_bias, w_exp_in, w_exp_out, w_shared_in, w_shared_out, norm_final):
    n_ctx, seq, d = x_prompt.shape
    n_lat, dec_seq, _ = x_sample.shape
    past = cache_mla_ckv.shape[2]
    n_ctx_tok = n_ctx * seq
    n_lat_tok = n_lat * dec_seq
    t = n_ctx_tok + n_lat_tok
    assert d == D_MODEL and w_mod.shape[0] == 1 and n_lat + 1 <= MOD_ROWS
    assert seq % TM_NORM == 0 and dec_seq % TM_NORM == 0 and n_ctx_tok % dec_seq == 0

    x_ctx = x_prompt.reshape(n_ctx_tok, d)
    x_lat = x_sample.reshape(n_lat_tok, d)

    cond = jnp.concatenate([c_ctx[None, :], c, jnp.zeros((MOD_ROWS - 1 - n_lat, d), F32)], axis=0)
    mod = _modulation(cond, w_mod.reshape(d, N_MOD * d), b_mod.reshape(1, N_MOD * d))
    mod4 = mod.reshape(MOD_ROWS, N_MOD, 1, d)

    w_in2 = w_in.reshape(d, -1)
    o_g = 2 * RET_QK_W + RET_V_W
    o_qa = o_g + RET_V_W
    o_kpe = o_qa + MLA_Q_RANK + MLA_KV_RANK
    o_gr = o_kpe + MLA_ROPE
    w_qkvg = w_in2[:, :o_qa].astype(BF16)
    w_low = jnp.pad(w_in2[:, o_qa:o_gr], ((0, 0), (0, LANES - MLA_ROPE))).astype(BF16)
    w_gates = w_in2[:, o_gr:].astype(BF16)
    wq3 = w_q_b.reshape(MLA_Q_RANK, MLA_HEADS, MLA_NOPE + MLA_ROPE)
    w_qb = jnp.pad(wq3, ((0, 0), (0, 0), (0, Q_HEAD_W - MLA_NOPE - MLA_ROPE))
                   ).reshape(MLA_Q_RANK, MLA_HEADS * Q_HEAD_W).astype(BF16)
    w_kvb = w_kv_b.reshape(MLA_KV_RANK, -1).astype(BF16)
    cos_t, sin_t = _rope_tables(TM_NORM, dec_seq)

    h1 = _norm_modulate(x_ctx, x_lat, norm_mix.reshape(1, d), mod4, n_ctx_tok, dec_seq)
    qkvg = _matmul(h1, w_qkvg, tm=2048, tn=1024, name="proj_qkvg")
    gates = _matmul(h1, w_gates, tm=2048, tn=1024, act="sigmoid", name="proj_gates")
    qn, ckv, ckv_b, kpe_raw, kpe_rot = _lowrank(
        h1, w_low, mla_q_norm.reshape(1, -1), mla_kv_norm.reshape(1, -1), cos_t, sin_t, n_ctx_tok, dec_seq)

    logits = ret_decay_logit.reshape(2, RET_HEADS)
    gn_w = ret_gn_w.reshape(1, RET_V_W)
    ctx_group = dec_seq // seq
    r_ctx, ret_state = _retention(qkvg, logits, gn_w, None, n_seq=n_ctx, seq=seq, group=ctx_group,
                                  row_block0=0, emit_state=True)
    (r_lat,) = _retention(qkvg, logits, gn_w, state_ret.reshape(n_lat, 2, RET_HEADS, RET_DK, RET_DV),
                          n_seq=n_lat, seq=dec_seq, group=1, row_block0=n_ctx_tok // dec_seq,
                          emit_state=False)

    q_all = _q_up(qn, w_qb, cos_t, sin_t, n_ctx_tok, dec_seq)
    kv_all = _matmul(ckv_b, w_kvb, tm=1024, tn=1024, name="kv_up")
    kv_ctx = _matmul(cache_mla_ckv.reshape(n_lat * past, MLA_KV_RANK), w_kvb, tm=1024, tn=1024,
                     name="kv_up_cache")
    kpe_ctx = jnp.pad(cache_mla_krope.reshape(n_lat * past, MLA_ROPE),
                      ((0, 0), (0, LANES - MLA_ROPE))).astype(BF16)
    a_ctx = _attention_ctx(q_all, kv_all, kpe_rot, n_ctx, seq)
    a_lat = _attention_lat(q_all, kv_ctx, kv_all, kpe_ctx, kpe_rot, n_lat, dec_seq, past, n_ctx_tok)

    merged = _merge(r_ctx, r_lat, a_ctx, a_lat, w_ret_o.reshape(RET_V_W, d).astype(BF16),
                    w_mla_o.reshape(MLA_HEADS * MLA_V, d).astype(BF16), gates, n_ctx_tok)
    x1, h_hi, h_lo, hp = _out_proj(merged, w_out.reshape(d, d).astype(BF16), x_ctx, x_lat,
                                   norm_ffn.reshape(1, d), mod4, n_ctx_tok, dec_seq)

    eidx, wts, rank, cnt = _router(h_hi, h_lo, w_router.reshape(d, N_EXPERTS).T,
                                   router_bias.reshape(N_EXPERTS, 1))
    n_blocks = t * TOP_K // EXPERT_BM + N_EXPERTS
    pstart, uidx, ue, meta, fill = _block_plan(cnt[:, 0].astype(I32), n_blocks)
    dest = _dest_rows(eidx, rank, pstart.astype(F32).reshape(N_EXPERTS, 1)).T.reshape(-1)
    xs = _dispatch(fill, meta, dest, hp, n_blocks)
    ys = _experts(uidx, ue, meta, xs, w_exp_in.reshape(N_EXPERTS, d, 2 * D_EXPERT),
                  w_exp_out.reshape(N_EXPERTS, D_EXPERT, d))
    y_ctx, y_lat = _combine(dest, wts.T, x1, h_hi, mod4, w_shared_in.reshape(d, 2 * D_SHARED).astype(BF16),
                            w_shared_out.reshape(D_SHARED, d).astype(BF16), norm_final.reshape(1, d), ys,
                            n_ctx_tok, dec_seq)

    y_prompt = y_ctx.reshape(n_ctx, seq, d)
    y_sample = y_lat.reshape(n_lat, dec_seq, d)
    new_ckv = ckv[:n_ctx_tok].reshape(n_ctx, 1, seq, MLA_KV_RANK)
    new_krope = kpe_raw[:n_ctx_tok, :MLA_ROPE].reshape(n_ctx, 1, seq, MLA_ROPE)
    new_state = ret_state.reshape(n_ctx, 1, 2, RET_HEADS, RET_DK, RET_DV)
    return (y_prompt, y_sample, new_ckv, new_krope, new_state)
```

```python
import functools

import jax
import jax.numpy as jnp
from jax import lax
from jax.experimental import pallas as pl
from jax.experimental.pallas import tpu as pltpu

F32 = jnp.float32
BF16 = jnp.bfloat16
U32 = jnp.uint32
I32 = jnp.int32

D_MODEL = 2048
GRID_W = 64
NORM_EPS = 1e-6
RET_HEADS = 8
RET_DK = 128
RET_DV = 256
RET_CHUNK = 128
RET_QK_W = RET_HEADS * RET_DK
RET_V_W = RET_HEADS * RET_DV
MLA_HEADS = 16
MLA_Q_RANK = 512
MLA_KV_RANK = 512
MLA_NOPE = 128
MLA_ROPE = 64
MLA_V = 128
ROPE_BASE = 10000.0
N_EXPERTS = 256
TOP_K = 8
N_GROUPS = 8
TOPK_GROUPS = 4
GROUP_SIZE = N_EXPERTS // N_GROUPS
D_EXPERT = 512
D_SHARED = 512
ROUTED_SCALE = 2.5

LANES = 128
SUBLANES = 8
ROW_TILES = D_MODEL // (2 * LANES)
assert ROW_TILES == SUBLANES
Q_HEAD_W = 256
N_MOD = 6
MOD_ROWS = 16
TM_NORM = 256
EXPERT_BM = 256
MIB = 1 << 20


def _cparams(sem, vmem_mib=None):
    kw = dict(dimension_semantics=sem)
    if vmem_mib is not None:
        kw["vmem_limit_bytes"] = vmem_mib * MIB
    return pltpu.CompilerParams(**kw)


def _sigmoid(x):
    return 1.0 / (1.0 + jnp.exp(-x))


def _silu(x):
    return x * _sigmoid(x)


def _rms(x, w):
    ms = jnp.mean(x * x, axis=-1, keepdims=True)
    return x * lax.rsqrt(ms + NORM_EPS) * w


def _pack_bf16_pair(lo, hi):
    return pltpu.pack_elementwise([lo, hi], packed_dtype=BF16)


def _unpack_bf16_pair(w):
    lo = pltpu.unpack_elementwise(w, index=0, packed_dtype=BF16, unpacked_dtype=F32)
    hi = pltpu.unpack_elementwise(w, index=1, packed_dtype=BF16, unpacked_dtype=F32)
    return lo, hi


def _store_row_tiles(ref, x):
    m = x.shape[0]
    for s in range(ROW_TILES):
        lo = x[:, 2 * LANES * s:2 * LANES * s + LANES]
        hi = x[:, 2 * LANES * s + LANES:2 * LANES * (s + 1)]
        ref[pl.ds(s, m, SUBLANES), :] = _pack_bf16_pair(lo, hi)


def _load_row_tiles(ref, m):
    parts = []
    for s in range(ROW_TILES):
        parts.extend(_unpack_bf16_pair(ref[pl.ds(s, m, SUBLANES), :]))
    return parts


def _rope128(x, cos, sin):
    lane = lax.broadcasted_iota(I32, x.shape, 1)
    first = (lane & 31) < 16
    rot = jnp.where(first, -pltpu.roll(x, LANES - 16, 1), pltpu.roll(x, 16, 1))
    return x * cos + rot * sin


def _mod_kernel(c_ref, w_ref, b_ref, o_ref):
    s = _silu(c_ref[...]).astype(BF16)
    o_ref[...] = jnp.dot(s, w_ref[...].astype(BF16), preferred_element_type=F32) + b_ref[...]


def _modulation(cond, w_mod, b_mod):
    n = w_mod.shape[1]
    tn = 1024
    return pl.pallas_call(
        _mod_kernel,
        grid=(n // tn,),
        in_specs=[pl.BlockSpec((MOD_ROWS, D_MODEL), lambda j: (0, 0)),
                  pl.BlockSpec((D_MODEL, tn), lambda j: (0, j)),
                  pl.BlockSpec((1, tn), lambda j: (0, j))],
        out_specs=pl.BlockSpec((MOD_ROWS, tn), lambda j: (0, j)),
        out_shape=jax.ShapeDtypeStruct((MOD_ROWS, n), F32),
        compiler_params=_cparams(("arbitrary",), 40),
        name="modulation",
    )(cond, w_mod, b_mod)


def _mod_spec(which, tm, n_ctx_tok, dec_seq):
    def index(i, *_):
        tok = i * tm
        row = jnp.where(tok < n_ctx_tok, 0, 1 + (tok - n_ctx_tok) // dec_seq)
        return (row, which, 0, 0)
    return pl.BlockSpec((None, None, 1, D_MODEL), index)


def _path_specs(tm, width, n_ctx_tok, axis=0):
    n_ctx_tiles = n_ctx_tok // tm

    def ctx(*idx):
        return (jnp.minimum(idx[axis], n_ctx_tiles - 1), 0)

    def lat(*idx):
        return (jnp.maximum(idx[axis] - n_ctx_tiles, 0), 0)

    return pl.BlockSpec((tm, width), ctx), pl.BlockSpec((tm, width), lat)


def _path_pick(tile, tm, n_ctx_tok, ctx_ref, lat_ref):
    return jnp.where(tile * tm < n_ctx_tok, ctx_ref[...], lat_ref[...])


def _normmod_kernel(xc_ref, xl_ref, w_ref, sc_ref, sh_ref, o_ref, *, n_ctx_tok):
    x = _path_pick(pl.program_id(0), o_ref.shape[0], n_ctx_tok, xc_ref, xl_ref)
    y = _rms(x, w_ref[...])
    o_ref[...] = (y * (1.0 + sc_ref[...]) + sh_ref[...]).astype(BF16)


def _norm_modulate(x_ctx, x_lat, w, mod4, n_ctx_tok, dec_seq):
    t = x_ctx.shape[0] + x_lat.shape[0]
    tm = TM_NORM
    return pl.pallas_call(
        functools.partial(_normmod_kernel, n_ctx_tok=n_ctx_tok),
        grid=(t // tm,),
        in_specs=[*_path_specs(tm, D_MODEL, n_ctx_tok),
                  pl.BlockSpec((1, D_MODEL), lambda i: (0, 0)),
                  _mod_spec(1, tm, n_ctx_tok, dec_seq),
                  _mod_spec(0, tm, n_ctx_tok, dec_seq)],
        out_specs=pl.BlockSpec((tm, D_MODEL), lambda i: (i, 0)),
        out_shape=jax.ShapeDtypeStruct((t, D_MODEL), BF16),
        compiler_params=_cparams(("parallel",)),
        name="norm_modulate",
    )(x_ctx, x_lat, w, mod4, mod4)


def _mm_kernel(a_ref, w_ref, o_ref, *w_bf16, act):
    if w_bf16:
        (w_s,) = w_bf16
        rows = 256

        @pl.when(pl.program_id(1) == 0)
        def _():
            def cast(i, c):
                r = pl.multiple_of(i * rows, rows)
                w_s[pl.ds(r, rows), :] = w_ref[pl.ds(r, rows), :].astype(BF16)
                return c

            lax.fori_loop(0, w_ref.shape[0] // rows, cast, 0)

        w = w_s[...]
    else:
        w = w_ref[...]
    acc = jnp.dot(a_ref[...].astype(BF16), w, preferred_element_type=F32)
    if act == "sigmoid":
        acc = _sigmoid(acc)
    o_ref[...] = acc.astype(o_ref.dtype)


def _matmul(a, w, *, tm, tn, n=None, act=None, name):
    m, k = a.shape
    n = w.shape[1] if n is None else n
    tm = min(tm, m)
    assert m % tm == 0 and n % tn == 0 and k % 256 == 0
    cast_w = w.dtype != BF16
    return pl.pallas_call(
        functools.partial(_mm_kernel, act=act),
        grid=(n // tn, m // tm),
        in_specs=[pl.BlockSpec((tm, k), lambda j, i: (i, 0)),
                  pl.BlockSpec((k, tn), lambda j, i: (0, j))],
        out_specs=pl.BlockSpec((tm, tn), lambda j, i: (i, j)),
        out_shape=jax.ShapeDtypeStruct((m, n), BF16),
        scratch_shapes=[pltpu.VMEM((k, tn), BF16)] if cast_w else [],
        compiler_params=_cparams(("parallel", "arbitrary"), 48),
        name=name,
    )(a, w)


def _lowrank_kernel(h_ref, w_ref, qnw_ref, kvnw_ref, cos_ref, sin_ref,
                    qn_ref, ckv_ref, ckvb_ref, kpe_ref, kper_ref):
    acc = jnp.dot(h_ref[...], w_ref[...], preferred_element_type=F32)
    qn_ref[...] = _rms(acc[:, :MLA_Q_RANK], qnw_ref[...]).astype(BF16)
    ckv = _rms(acc[:, MLA_Q_RANK:MLA_Q_RANK + MLA_KV_RANK], kvnw_ref[...])
    ckv_ref[...] = ckv
    ckvb_ref[...] = ckv.astype(BF16)
    kp = acc[:, MLA_Q_RANK + MLA_KV_RANK:]
    kpe_ref[...] = kp
    kper_ref[...] = _rope128(kp, cos_ref[...], sin_ref[...]).astype(BF16)


def _rope_spec(tm, n_ctx_tok, dec_seq):
    per_seq = dec_seq // tm

    def index(i, *_):
        tok = i * tm
        return (jnp.where(tok < n_ctx_tok, 0, 1 + ((tok - n_ctx_tok) // tm) % per_seq), 0)
    return pl.BlockSpec((tm, LANES), index)


def _lowrank(h, w_low, qnw, kvnw, cos_t, sin_t, n_ctx_tok, dec_seq):
    t = h.shape[0]
    tm = TM_NORM
    n = w_low.shape[1]
    row = lambda i: (i, 0)
    const = lambda i: (0, 0)
    return pl.pallas_call(
        _lowrank_kernel,
        grid=(t // tm,),
        in_specs=[pl.BlockSpec((tm, D_MODEL), row),
                  pl.BlockSpec((D_MODEL, n), const),
                  pl.BlockSpec((1, MLA_Q_RANK), const),
                  pl.BlockSpec((1, MLA_KV_RANK), const),
                  _rope_spec(tm, n_ctx_tok, dec_seq),
                  _rope_spec(tm, n_ctx_tok, dec_seq)],
        out_specs=[pl.BlockSpec((tm, MLA_Q_RANK), row),
                   pl.BlockSpec((tm, MLA_KV_RANK), row),
                   pl.BlockSpec((tm, MLA_KV_RANK), row),
                   pl.BlockSpec((tm, LANES), row),
                   pl.BlockSpec((tm, LANES), row)],
        out_shape=[jax.ShapeDtypeStruct((t, MLA_Q_RANK), BF16),
                   jax.ShapeDtypeStruct((t, MLA_KV_RANK), F32),
                   jax.ShapeDtypeStruct((t, MLA_KV_RANK), BF16),
                   jax.ShapeDtypeStruct((t, LANES), F32),
                   jax.ShapeDtypeStruct((t, LANES), BF16)],
        compiler_params=_cparams(("parallel",), 40),
        name="lowrank_proj",
    )(h, w_low, qnw, kvnw, cos_t, sin_t)


Q_PRESCALE = (MLA_NOPE + MLA_ROPE) ** -0.5 * 1.4426950408889634


def _qb_kernel(qn_ref, w_ref, cos_ref, sin_ref, o_ref):
    qn = qn_ref[...]
    cos = cos_ref[...]
    sin = sin_ref[...]
    for h in range(MLA_HEADS):
        lo = h * Q_HEAD_W
        acc = jnp.dot(qn, w_ref[:, lo:lo + Q_HEAD_W], preferred_element_type=F32) * Q_PRESCALE
        o_ref[:, lo:lo + MLA_NOPE] = acc[:, :MLA_NOPE].astype(BF16)
        o_ref[:, lo + MLA_NOPE:lo + Q_HEAD_W] = _rope128(acc[:, MLA_NOPE:], cos, sin).astype(BF16)


def _q_up(qn, w_qb, cos_t, sin_t, n_ctx_tok, dec_seq):
    t = qn.shape[0]
    tm = TM_NORM
    n = w_qb.shape[1]
    return pl.pallas_call(
        _qb_kernel,
        grid=(t // tm,),
        in_specs=[pl.BlockSpec((tm, MLA_Q_RANK), lambda i: (i, 0)),
                  pl.BlockSpec((MLA_Q_RANK, n), lambda i: (0, 0)),
                  _rope_spec(tm, n_ctx_tok, dec_seq),
                  _rope_spec(tm, n_ctx_tok, dec_seq)],
        out_specs=pl.BlockSpec((tm, n), lambda i: (i, 0)),
        out_shape=jax.ShapeDtypeStruct((t, n), BF16),
        compiler_params=_cparams(("parallel",), 40),
        name="q_up_rope",
    )(qn, w_qb, cos_t, sin_t)


def _ret_kernel(logit_ref, q_ref, k_ref, v_ref, g_ref, gnw_ref, *rest, n_chunks, group, has_s0, emit_state):
    rest = list(rest)
    s0_ref = rest.pop(0) if has_s0 else None
    o_ref = rest.pop(0)
    sfin_ref = rest.pop(0) if emit_state else None
    sf_scr, sb_scr = rest
    c_len = RET_CHUNK
    head = pl.program_id(1)

    def log_gamma(direction):
        l = jnp.full((c_len, 1), logit_ref[direction, head], F32)
        return jnp.minimum(l, 0.0) - jnp.log1p(jnp.exp(-jnp.abs(l)))

    lgf_c = log_gamma(0)
    lgb_c = log_gamma(1)
    ii = lax.broadcasted_iota(I32, (c_len, c_len), 0).astype(F32)
    jj = lax.broadcasted_iota(I32, (c_len, c_len), 1).astype(F32)
    k_scale = RET_DK ** -0.5
    decay = (jnp.where(ii >= jj, jnp.exp(jnp.maximum(ii - jj, 0.0) * lgf_c), 0.0)
             + jnp.where(jj >= ii, jnp.exp(jnp.maximum(jj - ii, 0.0) * lgb_c), 0.0)) * k_scale
    pos = lax.broadcasted_iota(I32, (c_len, 1), 0).astype(F32)
    qdec_f = jnp.exp((pos + 1.0) * lgf_c)
    qdec_b = jnp.exp((c_len - pos) * lgb_c)
    kdec_f = jnp.exp((c_len - 1.0 - pos) * lgf_c) * k_scale
    kdec_b = jnp.exp(pos * lgb_c) * k_scale
    cdec_f = jnp.exp(c_len * lgf_c)
    cdec_b = jnp.exp(c_len * lgb_c)
    tn_dims = (((0,), (0,)), ((), ()))
    nt_dims = (((1,), (1,)), ((), ()))

    def chunk(c):
        return pl.ds(c * c_len, c_len)

    for c in range(n_chunks):
        kc = k_ref[chunk(c), :].astype(F32)
        vc = v_ref[chunk(c), :]
        sf_scr[c] = lax.dot_general((kc * kdec_f).astype(BF16), vc, tn_dims, preferred_element_type=F32)
        sb_scr[c] = lax.dot_general((kc * kdec_b).astype(BF16), vc, tn_dims, preferred_element_type=F32)

    per_seq = n_chunks // group
    for g in range(group):
        chunks = range(g * per_seq, (g + 1) * per_seq)
        s = s0_ref[g, 0] if has_s0 else jnp.zeros((RET_DK, RET_DV), F32)
        for c in chunks:
            kv = sf_scr[c]
            sf_scr[c] = s
            s = cdec_f * s + kv
        if emit_state:
            sfin_ref[g, 0] = s
        s = s0_ref[g, 1] if has_s0 else jnp.zeros((RET_DK, RET_DV), F32)
        for c in reversed(chunks):
            kv = sb_scr[c]
            sb_scr[c] = s
            s = cdec_b * s + kv
        if emit_state:
            sfin_ref[g, 1] = s

    gnw = gnw_ref[...]
    for c in range(n_chunks):
        qc = q_ref[chunk(c), :]
        kc = k_ref[chunk(c), :]
        vc = v_ref[chunk(c), :]
        sc = lax.dot_general(qc, kc, nt_dims, preferred_element_type=F32) * decay
        qf = qc.astype(F32)
        qcat = jnp.concatenate([(qf * qdec_f).astype(BF16), (qf * qdec_b).astype(BF16)], axis=1)
        scat = jnp.concatenate([sf_scr[c], sb_scr[c]], axis=0).astype(BF16)
        o = (jnp.dot(sc.astype(BF16), vc, preferred_element_type=F32)
             + jnp.dot(qcat, scat, preferred_element_type=F32))
        mu = jnp.mean(o, axis=-1, keepdims=True)
        var = jnp.mean(jnp.square(o - mu), axis=-1, keepdims=True)
        on = (o - mu) * lax.rsqrt(var + NORM_EPS) * gnw
        o_ref[chunk(c), :] = (_silu(g_ref[chunk(c), :].astype(F32)) * on).astype(BF16)


def _retention(qkvg, logits, gn_w, s0, *, n_seq, seq, group, row_block0, emit_state):
    rows = group * seq
    n_chunks = rows // RET_CHUNK
    has_s0 = s0 is not None
    assert n_seq % group == 0
    k_off = RET_QK_W // RET_DK
    v_off = 2 * RET_QK_W // RET_DV
    g_off = (2 * RET_QK_W + RET_V_W) // RET_DV
    in_specs = [pl.BlockSpec(memory_space=pltpu.SMEM),
                pl.BlockSpec((rows, RET_DK), lambda b, h: (row_block0 + b, h)),
                pl.BlockSpec((rows, RET_DK), lambda b, h: (row_block0 + b, k_off + h)),
                pl.BlockSpec((rows, RET_DV), lambda b, h: (row_block0 + b, v_off + h)),
                pl.BlockSpec((rows, RET_DV), lambda b, h: (row_block0 + b, g_off + h)),
                pl.BlockSpec((1, RET_DV), lambda b, h: (0, h))]
    args = [logits, qkvg, qkvg, qkvg, qkvg, gn_w]
    state_spec = pl.BlockSpec((group, 2, None, RET_DK, RET_DV), lambda b, h: (b, 0, h, 0, 0))
    if has_s0:
        in_specs.append(state_spec)
        args.append(s0)
    out_specs = [pl.BlockSpec((rows, RET_DV), lambda b, h: (b, h))]
    out_shape = [jax.ShapeDtypeStruct((n_seq * seq, RET_V_W), BF16)]
    if emit_state:
        out_specs.append(state_spec)
        out_shape.append(jax.ShapeDtypeStruct((n_seq, 2, RET_HEADS, RET_DK, RET_DV), F32))
    res = pl.pallas_call(
        functools.partial(_ret_kernel, n_chunks=n_chunks, group=group, has_s0=has_s0, emit_state=emit_state),
        grid=(n_seq // group, RET_HEADS),
        in_specs=in_specs,
        out_specs=out_specs,
        out_shape=out_shape,
        scratch_shapes=[pltpu.VMEM((n_chunks, RET_DK, RET_DV), F32),
                        pltpu.VMEM((n_chunks, RET_DK, RET_DV), F32)],
        compiler_params=_cparams(("parallel", "parallel")),
        name="retention_seq%d" % seq,
    )(*args)
    return res


V_EXT_W = MLA_V + LANES


def _ones_column(rows):
    lane = lax.broadcasted_iota(I32, (rows, LANES), 1)
    return jnp.where(lane == 0, 1.0, 0.0).astype(BF16)


def _softmax_pv(q, kcat, v_ext):
    s = lax.dot_general(q, kcat, (((1,), (1,)), ((), ())), preferred_element_type=F32)
    e = jnp.exp2(s - jnp.max(s, axis=-1, keepdims=True))
    o = jnp.dot(e.astype(BF16), v_ext, preferred_element_type=F32)
    return o[:, :MLA_V] / o[:, MLA_V:MLA_V + 1]


def _attn_ctx_kernel(q_ref, kv_ref, kpe_ref, o_ref):
    kpe = kpe_ref[...]
    ones = _ones_column(kpe.shape[0])
    kvw = MLA_NOPE + MLA_V
    for h in range(MLA_HEADS):
        kv = kv_ref[:, h * kvw:(h + 1) * kvw]
        kcat = jnp.concatenate([kv[:, :MLA_NOPE], kpe], axis=1)
        v_ext = jnp.concatenate([kv[:, MLA_NOPE:], ones], axis=1)
        o = _softmax_pv(q_ref[:, h * Q_HEAD_W:(h + 1) * Q_HEAD_W], kcat, v_ext)
        o_ref[:, h * MLA_V:(h + 1) * MLA_V] = o.astype(o_ref.dtype)


def _attention_ctx(q_all, kv_all, kpe_all, n_seq, seq):
    return pl.pallas_call(
        _attn_ctx_kernel,
        grid=(n_seq,),
        in_specs=[pl.BlockSpec((seq, MLA_HEADS * Q_HEAD_W), lambda b: (b, 0)),
                  pl.BlockSpec((seq, MLA_HEADS * (MLA_NOPE + MLA_V)), lambda b: (b, 0)),
                  pl.BlockSpec((seq, LANES), lambda b: (b, 0))],
        out_specs=pl.BlockSpec((seq, MLA_HEADS * MLA_V), lambda b: (b, 0)),
        out_shape=jax.ShapeDtypeStruct((n_seq * seq, MLA_HEADS * MLA_V), BF16),
        compiler_params=_cparams(("parallel",), 40),
        name="attention_ctx",
    )(q_all, kv_all, kpe_all)


def _attn_lat_kernel(q_ref, kvc_ref, kvl_ref, kpc_ref, kpl_ref, o_ref, kcat, vcat, *, past, seq):
    @pl.when(pl.program_id(2) == 0)
    def _():
        kcat[0:past, 0:MLA_NOPE] = kvc_ref[:, :MLA_NOPE]
        kcat[0:past, MLA_NOPE:] = kpc_ref[...]
        kcat[past:past + seq, 0:MLA_NOPE] = kvl_ref[:, :MLA_NOPE]
        kcat[past:past + seq, MLA_NOPE:] = kpl_ref[...]
        vcat[0:past, 0:MLA_V] = kvc_ref[:, MLA_NOPE:]
        vcat[past:past + seq, 0:MLA_V] = kvl_ref[:, MLA_NOPE:]
        vcat[:, MLA_V:] = _ones_column(past + seq)

    sub = 256
    for r in range(q_ref.shape[0] // sub):
        rows = slice(r * sub, (r + 1) * sub)
        o_ref[rows, :] = _softmax_pv(q_ref[rows, :], kcat[...], vcat[...]).astype(o_ref.dtype)


def _attention_lat(q_all, kv_ctx, kv_all, kpe_ctx, kpe_all, n_seq, seq, past, n_ctx_tok):
    tq = 1024
    nq = seq // tq
    q0 = n_ctx_tok // tq
    s0 = n_ctx_tok // seq
    kvw = MLA_NOPE + MLA_V
    return pl.pallas_call(
        functools.partial(_attn_lat_kernel, past=past, seq=seq),
        grid=(n_seq, MLA_HEADS, nq),
        in_specs=[pl.BlockSpec((tq, Q_HEAD_W), lambda b, h, i: (q0 + b * nq + i, h)),
                  pl.BlockSpec((past, kvw), lambda b, h, i: (b, h)),
                  pl.BlockSpec((seq, kvw), lambda b, h, i: (s0 + b, h)),
                  pl.BlockSpec((past, LANES), lambda b, h, i: (b, 0)),
                  pl.BlockSpec((seq, LANES), lambda b, h, i: (s0 + b, 0))],
        out_specs=pl.BlockSpec((tq, MLA_V), lambda b, h, i: (b * nq + i, h)),
        out_shape=jax.ShapeDtypeStruct((n_seq * seq, MLA_HEADS * MLA_V), BF16),
        scratch_shapes=[pltpu.VMEM((past + seq, MLA_NOPE + LANES), BF16),
                        pltpu.VMEM((past + seq, V_EXT_W), BF16)],
        compiler_params=_cparams(("parallel", "parallel", "arbitrary"), 40),
        name="attention_lat",
    )(q_all, kv_ctx, kv_all, kpe_ctx, kpe_all)


def _merge_kernel(rc_ref, rl_ref, ac_ref, al_ref, wr_ref, wm_ref, gr_ref, gm_ref, o_ref, *, n_ctx_tok):
    tile = pl.program_id(1)
    tm = o_ref.shape[0]
    r = _path_pick(tile, tm, n_ctx_tok, rc_ref, rl_ref)
    a = _path_pick(tile, tm, n_ctx_tok, ac_ref, al_ref)
    ro = jnp.dot(r, wr_ref[...], preferred_element_type=F32)
    mo = jnp.dot(a, wm_ref[...], preferred_element_type=F32)
    o_ref[...] = (gr_ref[...].astype(F32) * ro + gm_ref[...].astype(F32) * mo).astype(BF16)


def _merge(r_ctx, r_lat, a_ctx, a_lat, w_ret_o, w_mla_o, gates, n_ctx_tok):
    t = gates.shape[0]
    tm, tn = 512, 1024
    nj = D_MODEL // tn
    return pl.pallas_call(
        functools.partial(_merge_kernel, n_ctx_tok=n_ctx_tok),
        grid=(nj, t // tm),
        in_specs=[*_path_specs(tm, RET_V_W, n_ctx_tok, axis=1),
                  *_path_specs(tm, MLA_HEADS * MLA_V, n_ctx_tok, axis=1),
                  pl.BlockSpec((RET_V_W, tn), lambda j, i: (0, j)),
                  pl.BlockSpec((MLA_HEADS * MLA_V, tn), lambda j, i: (0, j)),
                  pl.BlockSpec((tm, tn), lambda j, i: (i, j)),
                  pl.BlockSpec((tm, tn), lambda j, i: (i, nj + j))],
        out_specs=pl.BlockSpec((tm, tn), lambda j, i: (i, j)),
        out_shape=jax.ShapeDtypeStruct((t, D_MODEL), BF16),
        compiler_params=_cparams(("parallel", "parallel"), 48),
        name="branch_merge",
    )(r_ctx, r_lat, a_ctx, a_lat, w_ret_o, w_mla_o, gates, gates)


def _outproj_kernel(m_ref, w_ref, xc_ref, xl_ref, g1_ref, nw_ref, sc_ref, sh_ref,
                    x1_ref, hhi_ref, hlo_ref, hp_ref, *, n_ctx_tok):
    mix = jnp.dot(m_ref[...], w_ref[...], preferred_element_type=F32)
    x = _path_pick(pl.program_id(0), x1_ref.shape[0], n_ctx_tok, xc_ref, xl_ref)
    x1 = x + g1_ref[...] * mix
    x1_ref[...] = x1
    h2 = _rms(x1, nw_ref[...]) * (1.0 + sc_ref[...]) + sh_ref[...]
    hi = h2.astype(BF16)
    hif = hi.astype(F32)
    hhi_ref[...] = hi
    hlo_ref[...] = (h2 - hif).astype(BF16)
    _store_row_tiles(hp_ref, hif)


def _out_proj(merged, w_out, x_ctx, x_lat, norm_w, mod4, n_ctx_tok, dec_seq):
    t = merged.shape[0]
    tm = TM_NORM
    row = lambda i: (i, 0)
    const = lambda i: (0, 0)
    return pl.pallas_call(
        functools.partial(_outproj_kernel, n_ctx_tok=n_ctx_tok),
        grid=(t // tm,),
        in_specs=[pl.BlockSpec((tm, D_MODEL), row),
                  pl.BlockSpec((D_MODEL, D_MODEL), const),
                  *_path_specs(tm, D_MODEL, n_ctx_tok),
                  _mod_spec(2, tm, n_ctx_tok, dec_seq),
                  pl.BlockSpec((1, D_MODEL), const),
                  _mod_spec(4, tm, n_ctx_tok, dec_seq),
                  _mod_spec(3, tm, n_ctx_tok, dec_seq)],
        out_specs=[pl.BlockSpec((tm, D_MODEL), row),
                   pl.BlockSpec((tm, D_MODEL), row),
                   pl.BlockSpec((tm, D_MODEL), row),
                   pl.BlockSpec((tm * ROW_TILES, LANES), row)],
        out_shape=[jax.ShapeDtypeStruct((t, D_MODEL), F32),
                   jax.ShapeDtypeStruct((t, D_MODEL), BF16),
                   jax.ShapeDtypeStruct((t, D_MODEL), BF16),
                   jax.ShapeDtypeStruct((t * ROW_TILES, LANES), U32)],
        compiler_params=_cparams(("parallel",), 48),
        name="out_proj_norm",
    )(merged, w_out, x_ctx, x_lat, mod4, norm_w, mod4, mod4)


def _router_kernel(hhi_ref, hlo_ref, wrt_ref, bias_ref, eidx_ref, wts_ref, rank_ref, cnt_ref, carry):
    tm = hhi_ref.shape[0]
    n_e = N_EXPERTS

    @pl.when(pl.program_id(0) == 0)
    def _():
        carry[...] = jnp.zeros_like(carry)

    wr = wrt_ref[...]
    wr_hi = wr.astype(BF16)
    wr_lo = (wr - wr_hi.astype(F32)).astype(BF16)
    hh = hhi_ref[...]
    hl = hlo_ref[...]
    nt_dims = (((1,), (1,)), ((), ()))
    logits = (lax.dot_general(wr_hi, hh, nt_dims, preferred_element_type=F32)
              + lax.dot_general(wr_hi, hl, nt_dims, preferred_element_type=F32)
              + lax.dot_general(wr_lo, hh, nt_dims, preferred_element_type=F32))
    scores = _sigmoid(logits)
    choice = scores + bias_ref[...]
    neg = -jnp.inf
    row = lax.broadcasted_iota(I32, (n_e, tm), 0).astype(F32)
    grow = lax.broadcasted_iota(I32, (GROUP_SIZE, tm), 0).astype(F32)

    gs = []
    for g in range(N_GROUPS):
        blk = choice[g * GROUP_SIZE:(g + 1) * GROUP_SIZE]
        m1 = jnp.max(blk, axis=0, keepdims=True)
        i1 = jnp.min(jnp.where(blk == m1, grow, float(GROUP_SIZE)), axis=0, keepdims=True)
        m2 = jnp.max(jnp.where(grow == i1, neg, blk), axis=0, keepdims=True)
        gs.append(m1 + m2)
    gsc = jnp.concatenate(gs, axis=0)
    gid = lax.broadcasted_iota(I32, (N_GROUPS, tm), 0).astype(F32)
    gsel = jnp.zeros((N_GROUPS, tm), F32)
    for _ in range(TOPK_GROUPS):
        m = jnp.max(gsc, axis=0, keepdims=True)
        idx = jnp.min(jnp.where(gsc == m, gid, float(N_GROUPS)), axis=0, keepdims=True)
        hit = gid == idx
        gsel = jnp.where(hit, 1.0, gsel)
        gsc = jnp.where(hit, neg, gsc)
    masked = jnp.concatenate(
        [jnp.where(gsel[g:g + 1] > 0.0, choice[g * GROUP_SIZE:(g + 1) * GROUP_SIZE], neg)
         for g in range(N_GROUPS)], axis=0)

    idxs, ws = [], []
    onehot = jnp.zeros((n_e, tm), F32)
    for _ in range(TOP_K):
        m = jnp.max(masked, axis=0, keepdims=True)
        idx = jnp.min(jnp.where(masked == m, row, float(n_e)), axis=0, keepdims=True)
        hit = row == idx
        ws.append(jnp.sum(jnp.where(hit, scores, 0.0), axis=0, keepdims=True))
        masked = jnp.where(hit, neg, masked)
        onehot = jnp.where(hit, 1.0, onehot)
        idxs.append(idx)
    w = jnp.concatenate(ws, axis=0)
    wts_ref[...] = w / jnp.sum(w, axis=0, keepdims=True) * ROUTED_SCALE
    eidx_ref[...] = jnp.concatenate(idxs, axis=0).astype(I32)

    t_src = lax.broadcasted_iota(I32, (tm, tm), 0)
    t_dst = lax.broadcasted_iota(I32, (tm, tm), 1)
    before = jnp.where(t_src < t_dst, 1.0, 0.0).astype(BF16)
    prior = jnp.dot(onehot.astype(BF16), before, preferred_element_type=F32) + carry[:, :1]
    rank_ref[...] = jnp.concatenate(
        [jnp.sum(jnp.where(row == idx, prior, 0.0), axis=0, keepdims=True) for idx in idxs],
        axis=0).astype(I32)
    carry[...] = carry[...] + jnp.sum(onehot, axis=1, keepdims=True)
    cnt_ref[...] = carry[...]


def _router(h_hi, h_lo, wr_t, bias_col):
    t = h_hi.shape[0]
    tm = 512
    tok = lambda i: (0, i)
    const = lambda i: (0, 0)
    return pl.pallas_call(
        _router_kernel,
        grid=(t // tm,),
        in_specs=[pl.BlockSpec((tm, D_MODEL), lambda i: (i, 0)),
                  pl.BlockSpec((tm, D_MODEL), lambda i: (i, 0)),
                  pl.BlockSpec((N_EXPERTS, D_MODEL), const),
                  pl.BlockSpec((N_EXPERTS, 1), const)],
        out_specs=[pl.BlockSpec((TOP_K, tm), tok),
                   pl.BlockSpec((TOP_K, tm), tok),
                   pl.BlockSpec((TOP_K, tm), tok),
                   pl.BlockSpec((N_EXPERTS, LANES), const)],
        out_shape=[jax.ShapeDtypeStruct((TOP_K, t), I32),
                   jax.ShapeDtypeStruct((TOP_K, t), F32),
                   jax.ShapeDtypeStruct((TOP_K, t), I32),
                   jax.ShapeDtypeStruct((N_EXPERTS, LANES), F32)],
        scratch_shapes=[pltpu.VMEM((N_EXPERTS, LANES), F32)],
        compiler_params=_cparams(("arbitrary",), 40),
        name="router_topk",
    )(h_hi, h_lo, wr_t, bias_col)


def _dest_kernel(eidx_ref, rank_ref, pstart_ref, dest_ref):
    tm = eidx_ref.shape[1]
    row = lax.broadcasted_iota(I32, (N_EXPERTS, tm), 0)
    pstart = pstart_ref[...]
    eidx = eidx_ref[...]
    base = jnp.concatenate(
        [jnp.sum(jnp.where(row == eidx[k:k + 1], pstart, 0.0), axis=0, keepdims=True)
         for k in range(TOP_K)], axis=0)
    dest_ref[...] = (base.astype(I32) + rank_ref[...]) * ROW_TILES


def _dest_rows(eidx, rank, pstart_col):
    t = eidx.shape[1]
    tm = 1024
    tok = lambda i: (0, i)
    return pl.pallas_call(
        _dest_kernel,
        grid=(t // tm,),
        in_specs=[pl.BlockSpec((TOP_K, tm), tok),
                  pl.BlockSpec((TOP_K, tm), tok),
                  pl.BlockSpec((N_EXPERTS, 1), lambda i: (0, 0))],
        out_specs=pl.BlockSpec((TOP_K, tm), tok),
        out_shape=jax.ShapeDtypeStruct((TOP_K, t), I32),
        compiler_params=_cparams(("parallel",)),
        name="dest_rows",
    )(eidx, rank, pstart_col)


def _dispatch_kernel(fill_ref, meta_ref, dest_ref, hp_ref, xs_ref, zbuf, sem, fill_sem):
    tm = hp_ref.shape[0]
    block_rows = zbuf.shape[0]
    n_blocks = xs_ref.shape[0] // block_rows

    def zero_fills(start):
        def run(cp):
            cp.start() if start else cp.wait()

        def per_expert(e, c):
            row = fill_ref[0, e]
            left = fill_ref[1, e]
            p = EXPERT_BM // 2
            while p >= 1:
                take = left & p

                @pl.when(take != 0)
                def _(row=row, p=p):
                    dst = xs_ref.at[pl.ds(pl.multiple_of(row * ROW_TILES, ROW_TILES), p * ROW_TILES)]
                    run(pltpu.make_async_copy(zbuf.at[pl.ds(0, p * ROW_TILES)], dst, fill_sem))

                row = row + take
                p //= 2
            return c

        def per_tail_block(b, c):
            dst = xs_ref.at[pl.ds(pl.multiple_of(b * block_rows, block_rows), block_rows)]
            run(pltpu.make_async_copy(zbuf, dst, fill_sem))
            return c

        lax.fori_loop(0, N_EXPERTS, per_expert, 0)
        lax.fori_loop(meta_ref[0], n_blocks, per_tail_block, 0)

    @pl.when(pl.program_id(0) == 0)
    def _():
        zbuf[...] = jnp.zeros_like(zbuf)
        zero_fills(True)

    def issue(t, c):
        src = hp_ref.at[pl.ds(pl.multiple_of(t * ROW_TILES, ROW_TILES), ROW_TILES)]
        for k in range(TOP_K):
            r = pl.multiple_of(dest_ref[t * TOP_K + k], ROW_TILES)
            pltpu.make_async_copy(src, xs_ref.at[pl.ds(r, ROW_TILES)], sem).start(priority=k % 2)
        return c

    lax.fori_loop(0, tm // ROW_TILES, issue, 0)
    for _ in range(TOP_K):
        pltpu.make_async_copy(hp_ref, xs_ref.at[pl.ds(0, tm)], sem).wait()

    @pl.when(pl.program_id(0) == 0)
    def _():
        zero_fills(False)


def _dispatch(fill, meta, dest, hp, n_blocks):
    tm = TM_NORM * ROW_TILES
    block_rows = EXPERT_BM * ROW_TILES
    return pl.pallas_call(
        _dispatch_kernel,
        grid=(hp.shape[0] // tm,),
        in_specs=[pl.BlockSpec(memory_space=pltpu.SMEM),
                  pl.BlockSpec(memory_space=pltpu.SMEM),
                  pl.BlockSpec((TM_NORM * TOP_K,), lambda i: (i,), memory_space=pltpu.SMEM),
                  pl.BlockSpec((tm, LANES), lambda i: (i, 0))],
        out_specs=pl.BlockSpec(memory_space=pl.ANY),
        out_shape=jax.ShapeDtypeStruct((n_blocks * block_rows, LANES), U32),
        scratch_shapes=[pltpu.VMEM((block_rows, LANES), U32),
                        pltpu.SemaphoreType.DMA(()),
                        pltpu.SemaphoreType.DMA(())],
        compiler_params=_cparams(("arbitrary",)),
        name="moe_dispatch",
    )(fill, meta, dest, hp)


N_WSLOTS = 2


def _expert_kernel(uidx_ref, ue_ref, meta_ref, xs_ref, wi_hbm, wo_hbm, ys_ref,
                   wi_st, wo_st, wi_s, wo_s, sem):
    b = pl.program_id(0)
    n_used = meta_ref[0]
    n_ue = meta_ref[1]

    def weight_copies(j, slot):
        e = ue_ref[j]
        copies = []
        for m, (hbm, st) in enumerate(((wi_hbm, wi_st), (wo_hbm, wo_st))):
            half = hbm.shape[1] // 2
            for q in range(2):
                rows = pl.ds(q * half, half)
                copies.append((pltpu.make_async_copy(hbm.at[e, rows], st.at[slot, rows], sem.at[2 * m + q, slot]), q))
        return copies

    @pl.when(b < n_used)
    def _():
        j = uidx_ref[b]

        @pl.when(b == 0)
        def _():
            for cp, prio in weight_copies(0, 0):
                cp.start(priority=prio)

            @pl.when(n_ue > 1)
            def _():
                for cp, prio in weight_copies(1, 1):
                    cp.start(priority=prio)

        @pl.when((b == 0) | (j != uidx_ref[jnp.maximum(b - 1, 0)]))
        def _():
            slot = j % N_WSLOTS
            for cp, _ in weight_copies(j, slot):
                cp.wait()
            rows = 256

            def cast_in(i, c):
                r = pl.multiple_of(i * rows, rows)
                wi_s[pl.ds(r, rows), :] = wi_st[slot, pl.ds(r, rows), :].astype(BF16)
                return c

            def cast_out(i, c):
                r = pl.multiple_of(i * rows, rows)
                wo_s[pl.ds(r, rows), :] = wo_st[slot, pl.ds(r, rows), :].astype(BF16)
                return c

            lax.fori_loop(0, D_MODEL // rows, cast_in, 0)
            lax.fori_loop(0, D_EXPERT // rows, cast_out, 0)

            @pl.when(j + N_WSLOTS < n_ue)
            def _():
                for cp, prio in weight_copies(j + N_WSLOTS, slot):
                    cp.start(priority=prio)

        bm = EXPERT_BM
        x = jnp.concatenate([p.astype(BF16) for p in _load_row_tiles(xs_ref, bm)], axis=1)
        gu = jnp.dot(x, wi_s[...], preferred_element_type=F32)
        act = (_silu(gu[:, :D_EXPERT]) * gu[:, D_EXPERT:]).astype(BF16)
        _store_row_tiles(ys_ref, jnp.dot(act, wo_s[...], preferred_element_type=F32))


def _experts(uidx, ue, meta, xs, w_exp_in, w_exp_out):
    rows = EXPERT_BM * ROW_TILES
    nb = xs.shape[0] // rows
    live = lambda b, uidx, ue, meta: (jnp.minimum(b, meta[0] - 1), 0)
    grid_spec = pltpu.PrefetchScalarGridSpec(
        num_scalar_prefetch=3,
        grid=(nb,),
        in_specs=[pl.BlockSpec((rows, LANES), live),
                  pl.BlockSpec(memory_space=pl.ANY),
                  pl.BlockSpec(memory_space=pl.ANY)],
        out_specs=pl.BlockSpec((rows, LANES), live),
        scratch_shapes=[pltpu.VMEM((N_WSLOTS, D_MODEL, 2 * D_EXPERT), F32),
                        pltpu.VMEM((N_WSLOTS, D_EXPERT, D_MODEL), F32),
                        pltpu.VMEM((D_MODEL, 2 * D_EXPERT), BF16),
                        pltpu.VMEM((D_EXPERT, D_MODEL), BF16),
                        pltpu.SemaphoreType.DMA((4, N_WSLOTS))],
    )
    return pl.pallas_call(
        _expert_kernel,
        grid_spec=grid_spec,
        out_shape=jax.ShapeDtypeStruct(xs.shape, U32),
        input_output_aliases={3: 0},
        compiler_params=_cparams(("arbitrary",), 52),
        name="routed_experts",
    )(uidx, ue, meta, xs, w_exp_in, w_exp_out)


def _combine_kernel(dest_ref, dnext_ref, wt_ref, x1_ref, hhi_ref, g2_ref, wsi_ref, wso_ref, nf_ref, ys_ref,
                    yc_ref, yl_ref, rows, moe_scr, sem, *, n_ctx_tok):
    tm = x1_ref.shape[0]
    i = pl.program_id(0)
    n = pl.num_programs(0)

    def gather(idx_ref, slot, t0, n_tok):
        def issue(t, c):
            for k in range(TOP_K):
                r = pl.multiple_of(idx_ref[t * TOP_K + k], ROW_TILES)
                dst = rows.at[slot, k, pl.ds(pl.multiple_of(t * ROW_TILES, ROW_TILES), ROW_TILES)]
                pltpu.make_async_copy(ys_ref.at[pl.ds(r, ROW_TILES)], dst, sem.at[slot]).start(priority=k % 2)
            return c

        lax.fori_loop(t0, t0 + n_tok, issue, 0)

    @pl.when(i == 0)
    def _():
        gather(dest_ref, 0, 0, tm)

    sub = 128
    n_pieces = (tm // sub) * ROW_TILES + 1
    per_piece = tm // n_pieces + 1

    def gather_next_slice(piece):
        t0 = piece * per_piece
        n_tok = min(per_piece, tm - t0)
        if n_tok > 0:
            @pl.when(i + 1 < n)
            def _():
                gather(dnext_ref, (i + 1) % 2, t0, n_tok)

    gather_next_slice(0)
    su = jnp.dot(hhi_ref[...], wsi_ref[...], preferred_element_type=F32)
    sact = (_silu(su[:, :D_SHARED]) * su[:, D_SHARED:]).astype(BF16)
    shared = jnp.dot(sact, wso_ref[...], preferred_element_type=F32)

    slot = i % 2
    for k in range(TOP_K):
        pltpu.make_async_copy(ys_ref.at[pl.ds(0, tm * ROW_TILES)], rows.at[slot, k], sem.at[slot]).wait()

    piece = 0
    for r0 in range(0, tm, sub):
        wks = [wt_ref[r0:r0 + sub, k:k + 1] for k in range(TOP_K)]
        for s in range(ROW_TILES):
            piece += 1
            gather_next_slice(piece)
            acc_lo = acc_hi = None
            for k in range(TOP_K):
                lo, hi = _unpack_bf16_pair(rows[slot, k, pl.ds(r0 * ROW_TILES + s, sub, SUBLANES), :])
                acc_lo = wks[k] * lo if acc_lo is None else acc_lo + wks[k] * lo
                acc_hi = wks[k] * hi if acc_hi is None else acc_hi + wks[k] * hi
            col = 2 * LANES * s
            moe_scr[r0:r0 + sub, col:col + LANES] = acc_lo
            moe_scr[r0:r0 + sub, col + LANES:col + 2 * LANES] = acc_hi
    moe = moe_scr[...] + shared
    x2 = x1_ref[...] + g2_ref[...] * moe
    y = _rms(x2, nf_ref[...])
    is_ctx = i * tm < n_ctx_tok

    @pl.when(is_ctx)
    def _():
        yc_ref[...] = y

    @pl.when(jnp.logical_not(is_ctx))
    def _():
        yl_ref[...] = y


def _combine(dest, wts_t, x1, h_hi, mod4, w_sh_in, w_sh_out, norm_final, ys, n_ctx_tok, dec_seq):
    t = x1.shape[0]
    tm = TM_NORM
    n = t // tm
    row = lambda i: (i, 0)
    const = lambda i: (0, 0)
    return pl.pallas_call(
        functools.partial(_combine_kernel, n_ctx_tok=n_ctx_tok),
        grid=(n,),
        in_specs=[pl.BlockSpec((tm * TOP_K,), lambda i: (i,), memory_space=pltpu.SMEM),
                  pl.BlockSpec((tm * TOP_K,), lambda i: (jnp.minimum(i + 1, n - 1),), memory_space=pltpu.SMEM),
                  pl.BlockSpec((tm, TOP_K), row),
                  pl.BlockSpec((tm, D_MODEL), row),
                  pl.BlockSpec((tm, D_MODEL), row),
                  _mod_spec(5, tm, n_ctx_tok, dec_seq),
                  pl.BlockSpec((D_MODEL, 2 * D_SHARED), const),
                  pl.BlockSpec((D_SHARED, D_MODEL), const),
                  pl.BlockSpec((1, D_MODEL), const),
                  pl.BlockSpec(memory_space=pl.ANY)],
        out_specs=list(_path_specs(tm, D_MODEL, n_ctx_tok)),
        out_shape=[jax.ShapeDtypeStruct((n_ctx_tok, D_MODEL), F32),
                   jax.ShapeDtypeStruct((t - n_ctx_tok, D_MODEL), F32)],
        scratch_shapes=[pltpu.VMEM((2, TOP_K, tm * ROW_TILES, LANES), U32),
                        pltpu.VMEM((tm, D_MODEL), F32),
                        pltpu.SemaphoreType.DMA((2,))],
        compiler_params=_cparams(("arbitrary",), 56),
        name="moe_combine",
    )(dest, dest, wts_t, x1, h_hi, mod4, w_sh_in, w_sh_out, norm_final, ys)


def _rope_tables(tm, dec_seq):
    rows = dec_seq // GRID_W
    row = jnp.broadcast_to(jnp.arange(rows, dtype=F32)[:, None], (rows, GRID_W)).reshape(-1)
    col = jnp.broadcast_to(jnp.arange(GRID_W, dtype=F32)[None, :], (rows, GRID_W)).reshape(-1)
    axis_dim = MLA_ROPE // 2
    freqs = jnp.power(ROPE_BASE, -jnp.arange(0, axis_dim, 2, dtype=F32) / axis_dim)
    ang_r = row[:, None] * freqs
    ang_c = col[:, None] * freqs
    ang = jnp.concatenate([ang_r, ang_r, ang_c, ang_c], axis=-1)
    pad = LANES - MLA_ROPE
    cos = jnp.concatenate([jnp.cos(ang), jnp.ones((dec_seq, pad), F32)], axis=1)
    sin = jnp.concatenate([jnp.sin(ang), jnp.zeros((dec_seq, pad), F32)], axis=1)
    cos = jnp.concatenate([jnp.ones((tm, LANES), F32), cos], axis=0)
    sin = jnp.concatenate([jnp.zeros((tm, LANES), F32), sin], axis=0)
    return cos, sin


def _block_plan(counts, n_blocks):
    bm = EXPERT_BM
    nblk = (counts + bm - 1) // bm
    bend = jnp.cumsum(nblk)
    bstart = bend - nblk
    present = nblk > 0
    pcum = jnp.cumsum(present.astype(I32))
    bidx = jnp.arange(n_blocks, dtype=I32)
    uidx = jnp.sum((present[None, :] & (bstart[None, :] <= bidx[:, None])).astype(I32), axis=1) - 1
    order = jnp.arange(N_EXPERTS, dtype=I32)
    ue = jnp.minimum(jnp.sum((pcum[None, :] <= order[:, None]).astype(I32), axis=1), N_EXPERTS - 1)
    meta = jnp.stack([bend[-1], pcum[-1]]).astype(I32)
    pstart = bstart * bm
    fill = jnp.stack([pstart + counts, nblk * bm - counts]).astype(I32)
    return pstart, uidx.astype(I32), ue.astype(I32), meta, fill


def kernel(x_prompt, x_sample, c, cache_mla_ckv, cache_mla_krope, state_ret, c_ctx, w_mod, b_mod, norm_mix, norm_ffn, w_in, ret_decay_logit, ret_gn_w, w_ret_o, mla_q_norm, w_q_b, mla_kv_norm, w_kv_b, w_mla_o, w_out, w_router, router_bias, w_exp_in, w_exp_out, w_shared_in, w_shared_out, norm_final):
    n_ctx, seq, d = x_prompt.shape
    n_lat, dec_seq, _ = x_sample.shape
    past = cache_mla_ckv.shape[2]
    n_ctx_tok = n_ctx * seq
    n_lat_tok = n_lat * dec_seq
    t = n_ctx_tok + n_lat_tok
    assert d == D_MODEL and w_mod.shape[0] == 1 and n_lat + 1 <= MOD_ROWS
    assert seq % TM_NORM == 0 and dec_seq % TM_NORM == 0 and n_ctx_tok % dec_seq == 0

    x_ctx = x_prompt.reshape(n_ctx_tok, d)
    x_lat = x_sample.reshape(n_lat_tok, d)

    cond = jnp.concatenate([c_ctx[None, :], c, jnp.zeros((MOD_ROWS - 1 - n_lat, d), F32)], axis=0)
    mod = _modulation(cond, w_mod.reshape(d, N_MOD * d), b_mod.reshape(1, N_MOD * d))
    mod4 = mod.reshape(MOD_ROWS, N_MOD, 1, d)

    w_in2 = w_in.reshape(d, -1)
    o_g = 2 * RET_QK_W + RET_V_W
    o_qa = o_g + RET_V_W
    o_kpe = o_qa + MLA_Q_RANK + MLA_KV_RANK
    o_gr = o_kpe + MLA_ROPE
    w_low = jnp.pad(w_in2[:, o_qa:o_gr], ((0, 0), (0, LANES - MLA_ROPE))).astype(BF16)
    w_gates = w_in2[:, o_gr:]
    wq3 = w_q_b.reshape(MLA_Q_RANK, MLA_HEADS, MLA_NOPE + MLA_ROPE)
    w_qb = jnp.pad(wq3, ((0, 0), (0, 0), (0, Q_HEAD_W - MLA_NOPE - MLA_ROPE))
                   ).reshape(MLA_Q_RANK, MLA_HEADS * Q_HEAD_W).astype(BF16)
    w_kvb = w_kv_b.reshape(MLA_KV_RANK, -1).astype(BF16)
    cos_t, sin_t = _rope_tables(TM_NORM, dec_seq)

    h1 = _norm_modulate(x_ctx, x_lat, norm_mix.reshape(1, d), mod4, n_ctx_tok, dec_seq)
    qkvg = _matmul(h1, w_in2, n=o_qa, tm=1024, tn=1024, name="proj_qkvg")
    gates = _matmul(h1, w_gates, tm=1024, tn=1024, act="sigmoid", name="proj_gates")
    qn, ckv, ckv_b, kpe_raw, kpe_rot = _lowrank(
        h1, w_low, mla_q_norm.reshape(1, -1), mla_kv_norm.reshape(1, -1), cos_t, sin_t, n_ctx_tok, dec_seq)

    logits = ret_decay_logit.reshape(2, RET_HEADS)
    gn_w = ret_gn_w.reshape(1, RET_V_W)
    ctx_group = dec_seq // seq
    r_ctx, ret_state = _retention(qkvg, logits, gn_w, None, n_seq=n_ctx, seq=seq, group=ctx_group,
                                  row_block0=0, emit_state=True)
    (r_lat,) = _retention(qkvg, logits, gn_w, state_ret.reshape(n_lat, 2, RET_HEADS, RET_DK, RET_DV),
                          n_seq=n_lat, seq=dec_seq, group=1, row_block0=n_ctx_tok // dec_seq,
                          emit_state=False)

    q_all = _q_up(qn, w_qb, cos_t, sin_t, n_ctx_tok, dec_seq)
    kv_all = _matmul(ckv_b, w_kvb, tm=1024, tn=1024, name="kv_up")
    kv_ctx = _matmul(cache_mla_ckv.reshape(n_lat * past, MLA_KV_RANK), w_kvb, tm=1024, tn=1024,
                     name="kv_up_cache")
    kpe_ctx = jnp.pad(cache_mla_krope.reshape(n_lat * past, MLA_ROPE),
                      ((0, 0), (0, LANES - MLA_ROPE))).astype(BF16)
    a_ctx = _attention_ctx(q_all, kv_all, kpe_rot, n_ctx, seq)
    a_lat = _attention_lat(q_all, kv_ctx, kv_all, kpe_ctx, kpe_rot, n_lat, dec_seq, past, n_ctx_tok)

    merged = _merge(r_ctx, r_lat, a_ctx, a_lat, w_ret_o.reshape(RET_V_W, d).astype(BF16),
                    w_mla_o.reshape(MLA_HEADS * MLA_V, d).astype(BF16), gates, n_ctx_tok)
    x1, h_hi, h_lo, hp = _out_proj(merged, w_out.reshape(d, d).astype(BF16), x_ctx, x_lat,
                                   norm_ffn.reshape(1, d), mod4, n_ctx_tok, dec_seq)

    eidx, wts, rank, cnt = _router(h_hi, h_lo, w_router.reshape(d, N_EXPERTS).T,
                                   router_bias.reshape(N_EXPERTS, 1))
    n_blocks = t * TOP_K // EXPERT_BM + N_EXPERTS
    pstart, uidx, ue, meta, fill = _block_plan(cnt[:, 0].astype(I32), n_blocks)
    dest = _dest_rows(eidx, rank, pstart.astype(F32).reshape(N_EXPERTS, 1)).T.reshape(-1)
    xs = _dispatch(fill, meta, dest, hp, n_blocks)
    ys = _experts(uidx, ue, meta, xs, w_exp_in.reshape(N_EXPERTS, d, 2 * D_EXPERT),
                  w_exp_out.reshape(N_EXPERTS, D_EXPERT, d))
    y_ctx, y_lat = _combine(dest, wts.T, x1, h_hi, mod4, w_shared_in.reshape(d, 2 * D_SHARED).astype(BF16),
                            w_shared_out.reshape(D_SHARED, d).astype(BF16), norm_final.reshape(1, d), ys,
                            n_ctx_tok, dec_seq)

    y_prompt = y_ctx.reshape(n_ctx, seq, d)
    y_sample = y_lat.reshape(n_lat, dec_seq, d)
    new_ckv = ckv[:n_ctx_tok].reshape(n_ctx, 1, seq, MLA_KV_RANK)
    new_krope = kpe_raw[:n_ctx_tok, :MLA_ROPE].reshape(n_ctx, 1, seq, MLA_ROPE)
    new_state = ret_state.reshape(n_ctx, 1, 2, RET_HEADS, RET_DK, RET_DV)
    return (y_prompt, y_sample, new_ckv, new_krope, new_state)
```

```python
import functools

import jax
import jax.numpy as jnp
from jax import lax
from jax.experimental import pallas as pl
from jax.experimental.pallas import tpu as pltpu

F32 = jnp.float32
BF16 = jnp.bfloat16
U32 = jnp.uint32
I32 = jnp.int32

D_MODEL = 2048
GRID_W = 64
NORM_EPS = 1e-6
RET_HEADS = 8
RET_DK = 128
RET_DV = 256
RET_CHUNK = 128
RET_QK_W = RET_HEADS * RET_DK
RET_V_W = RET_HEADS * RET_DV
MLA_HEADS = 16
MLA_Q_RANK = 512
MLA_KV_RANK = 512
MLA_NOPE = 128
MLA_ROPE = 64
MLA_V = 128
ROPE_BASE = 10000.0
N_EXPERTS = 256
TOP_K = 8
N_GROUPS = 8
TOPK_GROUPS = 4
GROUP_SIZE = N_EXPERTS // N_GROUPS
D_EXPERT = 512
D_SHARED = 512
ROUTED_SCALE = 2.5

LANES = 128
SUBLANES = 8
ROW_TILES = D_MODEL // (2 * LANES)
assert ROW_TILES == SUBLANES
Q_HEAD_W = 256
N_MOD = 6
MOD_ROWS = 16
TM_NORM = 256
EXPERT_BM = 256
MIB = 1 << 20


def _cparams(sem, vmem_mib=None):
    kw = dict(dimension_semantics=sem)
    if vmem_mib is not None:
        kw["vmem_limit_bytes"] = vmem_mib * MIB
    return pltpu.CompilerParams(**kw)


def _sigmoid(x):
    return 1.0 / (1.0 + jnp.exp(-x))


def _silu(x):
    return x * _sigmoid(x)


def _rms(x, w):
    ms = jnp.mean(x * x, axis=-1, keepdims=True)
    return x * lax.rsqrt(ms + NORM_EPS) * w


def _pack_bf16_pair(lo, hi):
    return pltpu.pack_elementwise([lo, hi], packed_dtype=BF16)


def _unpack_bf16_pair(w):
    lo = pltpu.unpack_elementwise(w, index=0, packed_dtype=BF16, unpacked_dtype=F32)
    hi = pltpu.unpack_elementwise(w, index=1, packed_dtype=BF16, unpacked_dtype=F32)
    return lo, hi


def _store_row_tiles(ref, x):
    m = x.shape[0]
    for s in range(ROW_TILES):
        lo = x[:, 2 * LANES * s:2 * LANES * s + LANES]
        hi = x[:, 2 * LANES * s + LANES:2 * LANES * (s + 1)]
        ref[pl.ds(s, m, SUBLANES), :] = _pack_bf16_pair(lo, hi)


def _load_row_tiles(ref, m):
    parts = []
    for s in range(ROW_TILES):
        parts.extend(_unpack_bf16_pair(ref[pl.ds(s, m, SUBLANES), :]))
    return parts


def _rope128(x, cos, sin):
    lane = lax.broadcasted_iota(I32, x.shape, 1)
    first = (lane & 31) < 16
    rot = jnp.where(first, -pltpu.roll(x, LANES - 16, 1), pltpu.roll(x, 16, 1))
    return x * cos + rot * sin


def _mod_kernel(c_ref, w_ref, b_ref, o_ref):
    s = _silu(c_ref[...]).astype(BF16)
    o_ref[...] = jnp.dot(s, w_ref[...].astype(BF16), preferred_element_type=F32) + b_ref[...]


def _modulation(cond, w_mod, b_mod):
    n = w_mod.shape[1]
    tn = 1024
    return pl.pallas_call(
        _mod_kernel,
        grid=(n // tn,),
        in_specs=[pl.BlockSpec((MOD_ROWS, D_MODEL), lambda j: (0, 0)),
                  pl.BlockSpec((D_MODEL, tn), lambda j: (0, j)),
                  pl.BlockSpec((1, tn), lambda j: (0, j))],
        out_specs=pl.BlockSpec((MOD_ROWS, tn), lambda j: (0, j)),
        out_shape=jax.ShapeDtypeStruct((MOD_ROWS, n), F32),
        compiler_params=_cparams(("arbitrary",), 40),
        name="modulation",
    )(cond, w_mod, b_mod)


def _mod_spec(which, tm, n_ctx_tok, dec_seq):
    def index(i, *_):
        tok = i * tm
        row = jnp.where(tok < n_ctx_tok, 0, 1 + (tok - n_ctx_tok) // dec_seq)
        return (row, which, 0, 0)
    return pl.BlockSpec((None, None, 1, D_MODEL), index)


def _path_specs(tm, width, n_ctx_tok, axis=0):
    n_ctx_tiles = n_ctx_tok // tm

    def ctx(*idx):
        return (jnp.minimum(idx[axis], n_ctx_tiles - 1), 0)

    def lat(*idx):
        return (jnp.maximum(idx[axis] - n_ctx_tiles, 0), 0)

    return pl.BlockSpec((tm, width), ctx), pl.BlockSpec((tm, width), lat)


def _path_pick(tile, tm, n_ctx_tok, ctx_ref, lat_ref):
    return jnp.where(tile * tm < n_ctx_tok, ctx_ref[...], lat_ref[...])


def _normmod_kernel(xc_ref, xl_ref, w_ref, sc_ref, sh_ref, o_ref, *, n_ctx_tok):
    x = _path_pick(pl.program_id(0), o_ref.shape[0], n_ctx_tok, xc_ref, xl_ref)
    y = _rms(x, w_ref[...])
    o_ref[...] = (y * (1.0 + sc_ref[...]) + sh_ref[...]).astype(BF16)


def _norm_modulate(x_ctx, x_lat, w, mod4, n_ctx_tok, dec_seq):
    t = x_ctx.shape[0] + x_lat.shape[0]
    tm = TM_NORM
    return pl.pallas_call(
        functools.partial(_normmod_kernel, n_ctx_tok=n_ctx_tok),
        grid=(t // tm,),
        in_specs=[*_path_specs(tm, D_MODEL, n_ctx_tok),
                  pl.BlockSpec((1, D_MODEL), lambda i: (0, 0)),
                  _mod_spec(1, tm, n_ctx_tok, dec_seq),
                  _mod_spec(0, tm, n_ctx_tok, dec_seq)],
        out_specs=pl.BlockSpec((tm, D_MODEL), lambda i: (i, 0)),
        out_shape=jax.ShapeDtypeStruct((t, D_MODEL), BF16),
        compiler_params=_cparams(("parallel",)),
        name="norm_modulate",
    )(x_ctx, x_lat, w, mod4, mod4)


def _mm_kernel(a_ref, w_ref, o_ref, *, act):
    acc = jnp.dot(a_ref[...].astype(BF16), w_ref[...], preferred_element_type=F32)
    if act == "sigmoid":
        acc = _sigmoid(acc)
    o_ref[...] = acc.astype(o_ref.dtype)


def _matmul(a, w, *, tm, tn, act=None, name):
    m, k = a.shape
    n = w.shape[1]
    tm = min(tm, m)
    assert m % tm == 0 and n % tn == 0
    return pl.pallas_call(
        functools.partial(_mm_kernel, act=act),
        grid=(n // tn, m // tm),
        in_specs=[pl.BlockSpec((tm, k), lambda j, i: (i, 0)),
                  pl.BlockSpec((k, tn), lambda j, i: (0, j))],
        out_specs=pl.BlockSpec((tm, tn), lambda j, i: (i, j)),
        out_shape=jax.ShapeDtypeStruct((m, n), BF16),
        compiler_params=_cparams(("parallel", "parallel"), 48),
        name=name,
    )(a, w)


def _lowrank_kernel(h_ref, w_ref, qnw_ref, kvnw_ref, cos_ref, sin_ref,
                    qn_ref, ckv_ref, ckvb_ref, kpe_ref, kper_ref):
    acc = jnp.dot(h_ref[...], w_ref[...], preferred_element_type=F32)
    qn_ref[...] = _rms(acc[:, :MLA_Q_RANK], qnw_ref[...]).astype(BF16)
    ckv = _rms(acc[:, MLA_Q_RANK:MLA_Q_RANK + MLA_KV_RANK], kvnw_ref[...])
    ckv_ref[...] = ckv
    ckvb_ref[...] = ckv.astype(BF16)
    kp = acc[:, MLA_Q_RANK + MLA_KV_RANK:]
    kpe_ref[...] = kp
    kper_ref[...] = _rope128(kp, cos_ref[...], sin_ref[...]).astype(BF16)


def _rope_spec(tm, n_ctx_tok, dec_seq):
    per_seq = dec_seq // tm

    def index(i, *_):
        tok = i * tm
        return (jnp.where(tok < n_ctx_tok, 0, 1 + ((tok - n_ctx_tok) // tm) % per_seq), 0)
    return pl.BlockSpec((tm, LANES), index)


def _lowrank(h, w_low, qnw, kvnw, cos_t, sin_t, n_ctx_tok, dec_seq):
    t = h.shape[0]
    tm = TM_NORM
    n = w_low.shape[1]
    row = lambda i: (i, 0)
    const = lambda i: (0, 0)
    return pl.pallas_call(
        _lowrank_kernel,
        grid=(t // tm,),
        in_specs=[pl.BlockSpec((tm, D_MODEL), row),
                  pl.BlockSpec((D_MODEL, n), const),
                  pl.BlockSpec((1, MLA_Q_RANK), const),
                  pl.BlockSpec((1, MLA_KV_RANK), const),
                  _rope_spec(tm, n_ctx_tok, dec_seq),
                  _rope_spec(tm, n_ctx_tok, dec_seq)],
        out_specs=[pl.BlockSpec((tm, MLA_Q_RANK), row),
                   pl.BlockSpec((tm, MLA_KV_RANK), row),
                   pl.BlockSpec((tm, MLA_KV_RANK), row),
                   pl.BlockSpec((tm, LANES), row),
                   pl.BlockSpec((tm, LANES), row)],
        out_shape=[jax.ShapeDtypeStruct((t, MLA_Q_RANK), BF16),
                   jax.ShapeDtypeStruct((t, MLA_KV_RANK), F32),
                   jax.ShapeDtypeStruct((t, MLA_KV_RANK), BF16),
                   jax.ShapeDtypeStruct((t, LANES), F32),
                   jax.ShapeDtypeStruct((t, LANES), BF16)],
        compiler_params=_cparams(("parallel",), 40),
        name="lowrank_proj",
    )(h, w_low, qnw, kvnw, cos_t, sin_t)


Q_PRESCALE = (MLA_NOPE + MLA_ROPE) ** -0.5 * 1.4426950408889634


def _qb_kernel(qn_ref, w_ref, cos_ref, sin_ref, o_ref):
    qn = qn_ref[...]
    cos = cos_ref[...]
    sin = sin_ref[...]
    for h in range(MLA_HEADS):
        lo = h * Q_HEAD_W
        acc = jnp.dot(qn, w_ref[:, lo:lo + Q_HEAD_W], preferred_element_type=F32) * Q_PRESCALE
        o_ref[:, lo:lo + MLA_NOPE] = acc[:, :MLA_NOPE].astype(BF16)
        o_ref[:, lo + MLA_NOPE:lo + Q_HEAD_W] = _rope128(acc[:, MLA_NOPE:], cos, sin).astype(BF16)


def _q_up(qn, w_qb, cos_t, sin_t, n_ctx_tok, dec_seq):
    t = qn.shape[0]
    tm = TM_NORM
    n = w_qb.shape[1]
    return pl.pallas_call(
        _qb_kernel,
        grid=(t // tm,),
        in_specs=[pl.BlockSpec((tm, MLA_Q_RANK), lambda i: (i, 0)),
                  pl.BlockSpec((MLA_Q_RANK, n), lambda i: (0, 0)),
                  _rope_spec(tm, n_ctx_tok, dec_seq),
                  _rope_spec(tm, n_ctx_tok, dec_seq)],
        out_specs=pl.BlockSpec((tm, n), lambda i: (i, 0)),
        out_shape=jax.ShapeDtypeStruct((t, n), BF16),
        compiler_params=_cparams(("parallel",), 40),
        name="q_up_rope",
    )(qn, w_qb, cos_t, sin_t)


def _ret_kernel(logit_ref, q_ref, k_ref, v_ref, g_ref, gnw_ref, *rest, n_chunks, group, has_s0, emit_state):
    rest = list(rest)
    s0_ref = rest.pop(0) if has_s0 else None
    o_ref = rest.pop(0)
    sfin_ref = rest.pop(0) if emit_state else None
    sf_scr, sb_scr = rest
    c_len = RET_CHUNK
    head = pl.program_id(1)

    def log_gamma(direction):
        l = jnp.full((c_len, 1), logit_ref[direction, head], F32)
        return jnp.minimum(l, 0.0) - jnp.log1p(jnp.exp(-jnp.abs(l)))

    lgf_c = log_gamma(0)
    lgb_c = log_gamma(1)
    ii = lax.broadcasted_iota(I32, (c_len, c_len), 0).astype(F32)
    jj = lax.broadcasted_iota(I32, (c_len, c_len), 1).astype(F32)
    k_scale = RET_DK ** -0.5
    decay = (jnp.where(ii >= jj, jnp.exp(jnp.maximum(ii - jj, 0.0) * lgf_c), 0.0)
             + jnp.where(jj >= ii, jnp.exp(jnp.maximum(jj - ii, 0.0) * lgb_c), 0.0)) * k_scale
    pos = lax.broadcasted_iota(I32, (c_len, 1), 0).astype(F32)
    qdec_f = jnp.exp((pos + 1.0) * lgf_c)
    qdec_b = jnp.exp((c_len - pos) * lgb_c)
    kdec_f = jnp.exp((c_len - 1.0 - pos) * lgf_c) * k_scale
    kdec_b = jnp.exp(pos * lgb_c) * k_scale
    cdec_f = jnp.exp(c_len * lgf_c)
    cdec_b = jnp.exp(c_len * lgb_c)
    tn_dims = (((0,), (0,)), ((), ()))
    nt_dims = (((1,), (1,)), ((), ()))

    def chunk(c):
        return pl.ds(c * c_len, c_len)

    for c in range(n_chunks):
        kc = k_ref[chunk(c), :].astype(F32)
        vc = v_ref[chunk(c), :]
        sf_scr[c] = lax.dot_general((kc * kdec_f).astype(BF16), vc, tn_dims, preferred_element_type=F32)
        sb_scr[c] = lax.dot_general((kc * kdec_b).astype(BF16), vc, tn_dims, preferred_element_type=F32)

    per_seq = n_chunks // group
    for g in range(group):
        chunks = range(g * per_seq, (g + 1) * per_seq)
        s = s0_ref[g, 0] if has_s0 else jnp.zeros((RET_DK, RET_DV), F32)
        for c in chunks:
            kv = sf_scr[c]
            sf_scr[c] = s
            s = cdec_f * s + kv
        if emit_state:
            sfin_ref[g, 0] = s
        s = s0_ref[g, 1] if has_s0 else jnp.zeros((RET_DK, RET_DV), F32)
        for c in reversed(chunks):
            kv = sb_scr[c]
            sb_scr[c] = s
            s = cdec_b * s + kv
        if emit_state:
            sfin_ref[g, 1] = s

    gnw = gnw_ref[...]
    for c in range(n_chunks):
        qc = q_ref[chunk(c), :]
        kc = k_ref[chunk(c), :]
        vc = v_ref[chunk(c), :]
        sc = lax.dot_general(qc, kc, nt_dims, preferred_element_type=F32) * decay
        qf = qc.astype(F32)
        qcat = jnp.concatenate([(qf * qdec_f).astype(BF16), (qf * qdec_b).astype(BF16)], axis=1)
        scat = jnp.concatenate([sf_scr[c], sb_scr[c]], axis=0).astype(BF16)
        o = (jnp.dot(sc.astype(BF16), vc, preferred_element_type=F32)
             + jnp.dot(qcat, scat, preferred_element_type=F32))
        mu = jnp.mean(o, axis=-1, keepdims=True)
        var = jnp.mean(jnp.square(o - mu), axis=-1, keepdims=True)
        on = (o - mu) * lax.rsqrt(var + NORM_EPS) * gnw
        o_ref[chunk(c), :] = (_silu(g_ref[chunk(c), :].astype(F32)) * on).astype(BF16)


def _retention(qkvg, logits, gn_w, s0, *, n_seq, seq, group, row_block0, emit_state):
    rows = group * seq
    n_chunks = rows // RET_CHUNK
    has_s0 = s0 is not None
    assert n_seq % group == 0
    k_off = RET_QK_W // RET_DK
    v_off = 2 * RET_QK_W // RET_DV
    g_off = (2 * RET_QK_W + RET_V_W) // RET_DV
    in_specs = [pl.BlockSpec(memory_space=pltpu.SMEM),
                pl.BlockSpec((rows, RET_DK), lambda b, h: (row_block0 + b, h)),
                pl.BlockSpec((rows, RET_DK), lambda b, h: (row_block0 + b, k_off + h)),
                pl.BlockSpec((rows, RET_DV), lambda b, h: (row_block0 + b, v_off + h)),
                pl.BlockSpec((rows, RET_DV), lambda b, h: (row_block0 + b, g_off + h)),
                pl.BlockSpec((1, RET_DV), lambda b, h: (0, h))]
    args = [logits, qkvg, qkvg, qkvg, qkvg, gn_w]
    state_spec = pl.BlockSpec((group, 2, None, RET_DK, RET_DV), lambda b, h: (b, 0, h, 0, 0))
    if has_s0:
        in_specs.append(state_spec)
        args.append(s0)
    out_specs = [pl.BlockSpec((rows, RET_DV), lambda b, h: (b, h))]
    out_shape = [jax.ShapeDtypeStruct((n_seq * seq, RET_V_W), BF16)]
    if emit_state:
        out_specs.append(state_spec)
        out_shape.append(jax.ShapeDtypeStruct((n_seq, 2, RET_HEADS, RET_DK, RET_DV), F32))
    res = pl.pallas_call(
        functools.partial(_ret_kernel, n_chunks=n_chunks, group=group, has_s0=has_s0, emit_state=emit_state),
        grid=(n_seq // group, RET_HEADS),
        in_specs=in_specs,
        out_specs=out_specs,
        out_shape=out_shape,
        scratch_shapes=[pltpu.VMEM((n_chunks, RET_DK, RET_DV), F32),
                        pltpu.VMEM((n_chunks, RET_DK, RET_DV), F32)],
        compiler_params=_cparams(("parallel", "parallel")),
        name="retention_seq%d" % seq,
    )(*args)
    return res


V_EXT_W = MLA_V + LANES


def _ones_column(rows):
    lane = lax.broadcasted_iota(I32, (rows, LANES), 1)
    return jnp.where(lane == 0, 1.0, 0.0).astype(BF16)


def _softmax_pv(q, kcat, v_ext):
    s = lax.dot_general(q, kcat, (((1,), (1,)), ((), ())), preferred_element_type=F32)
    e = jnp.exp2(s - jnp.max(s, axis=-1, keepdims=True))
    o = jnp.dot(e.astype(BF16), v_ext, preferred_element_type=F32)
    return o[:, :MLA_V] / o[:, MLA_V:MLA_V + 1]


def _attn_ctx_kernel(q_ref, kv_ref, kpe_ref, o_ref):
    kpe = kpe_ref[...]
    ones = _ones_column(kpe.shape[0])
    kvw = MLA_NOPE + MLA_V
    for h in range(MLA_HEADS):
        kv = kv_ref[:, h * kvw:(h + 1) * kvw]
        kcat = jnp.concatenate([kv[:, :MLA_NOPE], kpe], axis=1)
        v_ext = jnp.concatenate([kv[:, MLA_NOPE:], ones], axis=1)
        o = _softmax_pv(q_ref[:, h * Q_HEAD_W:(h + 1) * Q_HEAD_W], kcat, v_ext)
        o_ref[:, h * MLA_V:(h + 1) * MLA_V] = o.astype(o_ref.dtype)


def _attention_ctx(q_all, kv_all, kpe_all, n_seq, seq):
    return pl.pallas_call(
        _attn_ctx_kernel,
        grid=(n_seq,),
        in_specs=[pl.BlockSpec((seq, MLA_HEADS * Q_HEAD_W), lambda b: (b, 0)),
                  pl.BlockSpec((seq, MLA_HEADS * (MLA_NOPE + MLA_V)), lambda b: (b, 0)),
                  pl.BlockSpec((seq, LANES), lambda b: (b, 0))],
        out_specs=pl.BlockSpec((seq, MLA_HEADS * MLA_V), lambda b: (b, 0)),
        out_shape=jax.ShapeDtypeStruct((n_seq * seq, MLA_HEADS * MLA_V), BF16),
        compiler_params=_cparams(("parallel",), 40),
        name="attention_ctx",
    )(q_all, kv_all, kpe_all)


def _attn_lat_kernel(q_ref, kvc_ref, kvl_ref, kpc_ref, kpl_ref, o_ref, kcat, vcat, *, past, seq):
    @pl.when(pl.program_id(2) == 0)
    def _():
        kcat[0:past, 0:MLA_NOPE] = kvc_ref[:, :MLA_NOPE]
        kcat[0:past, MLA_NOPE:] = kpc_ref[...]
        kcat[past:past + seq, 0:MLA_NOPE] = kvl_ref[:, :MLA_NOPE]
        kcat[past:past + seq, MLA_NOPE:] = kpl_ref[...]
        vcat[0:past, 0:MLA_V] = kvc_ref[:, MLA_NOPE:]
        vcat[past:past + seq, 0:MLA_V] = kvl_ref[:, MLA_NOPE:]
        vcat[:, MLA_V:] = _ones_column(past + seq)

    sub = 256
    for r in range(q_ref.shape[0] // sub):
        rows = slice(r * sub, (r + 1) * sub)
        o_ref[rows, :] = _softmax_pv(q_ref[rows, :], kcat[...], vcat[...]).astype(o_ref.dtype)


def _attention_lat(q_all, kv_ctx, kv_all, kpe_ctx, kpe_all, n_seq, seq, past, n_ctx_tok):
    tq = 1024
    nq = seq // tq
    q0 = n_ctx_tok // tq
    s0 = n_ctx_tok // seq
    kvw = MLA_NOPE + MLA_V
    return pl.pallas_call(
        functools.partial(_attn_lat_kernel, past=past, seq=seq),
        grid=(n_seq, MLA_HEADS, nq),
        in_specs=[pl.BlockSpec((tq, Q_HEAD_W), lambda b, h, i: (q0 + b * nq + i, h)),
                  pl.BlockSpec((past, kvw), lambda b, h, i: (b, h)),
                  pl.BlockSpec((seq, kvw), lambda b, h, i: (s0 + b, h)),
                  pl.BlockSpec((past, LANES), lambda b, h, i: (b, 0)),
                  pl.BlockSpec((seq, LANES), lambda b, h, i: (s0 + b, 0))],
        out_specs=pl.BlockSpec((tq, MLA_V), lambda b, h, i: (b * nq + i, h)),
        out_shape=jax.ShapeDtypeStruct((n_seq * seq, MLA_HEADS * MLA_V), BF16),
        scratch_shapes=[pltpu.VMEM((past + seq, MLA_NOPE + LANES), BF16),
                        pltpu.VMEM((past + seq, V_EXT_W), BF16)],
        compiler_params=_cparams(("parallel", "parallel", "arbitrary"), 40),
        name="attention_lat",
    )(q_all, kv_ctx, kv_all, kpe_ctx, kpe_all)


def _merge_kernel(rc_ref, rl_ref, ac_ref, al_ref, wr_ref, wm_ref, gr_ref, gm_ref, o_ref, *, n_ctx_tok):
    tile = pl.program_id(1)
    tm = o_ref.shape[0]
    r = _path_pick(tile, tm, n_ctx_tok, rc_ref, rl_ref)
    a = _path_pick(tile, tm, n_ctx_tok, ac_ref, al_ref)
    ro = jnp.dot(r, wr_ref[...], preferred_element_type=F32)
    mo = jnp.dot(a, wm_ref[...], preferred_element_type=F32)
    o_ref[...] = (gr_ref[...].astype(F32) * ro + gm_ref[...].astype(F32) * mo).astype(BF16)


def _merge(r_ctx, r_lat, a_ctx, a_lat, w_ret_o, w_mla_o, gates, n_ctx_tok):
    t = gates.shape[0]
    tm, tn = 512, 1024
    nj = D_MODEL // tn
    return pl.pallas_call(
        functools.partial(_merge_kernel, n_ctx_tok=n_ctx_tok),
        grid=(nj, t // tm),
        in_specs=[*_path_specs(tm, RET_V_W, n_ctx_tok, axis=1),
                  *_path_specs(tm, MLA_HEADS * MLA_V, n_ctx_tok, axis=1),
                  pl.BlockSpec((RET_V_W, tn), lambda j, i: (0, j)),
                  pl.BlockSpec((MLA_HEADS * MLA_V, tn), lambda j, i: (0, j)),
                  pl.BlockSpec((tm, tn), lambda j, i: (i, j)),
                  pl.BlockSpec((tm, tn), lambda j, i: (i, nj + j))],
        out_specs=pl.BlockSpec((tm, tn), lambda j, i: (i, j)),
        out_shape=jax.ShapeDtypeStruct((t, D_MODEL), BF16),
        compiler_params=_cparams(("parallel", "parallel"), 48),
        name="branch_merge",
    )(r_ctx, r_lat, a_ctx, a_lat, w_ret_o, w_mla_o, gates, gates)


def _outproj_kernel(m_ref, w_ref, xc_ref, xl_ref, g1_ref, nw_ref, sc_ref, sh_ref,
                    x1_ref, hhi_ref, hlo_ref, hp_ref, *, n_ctx_tok):
    mix = jnp.dot(m_ref[...], w_ref[...], preferred_element_type=F32)
    x = _path_pick(pl.program_id(0), x1_ref.shape[0], n_ctx_tok, xc_ref, xl_ref)
    x1 = x + g1_ref[...] * mix
    x1_ref[...] = x1
    h2 = _rms(x1, nw_ref[...]) * (1.0 + sc_ref[...]) + sh_ref[...]
    hi = h2.astype(BF16)
    hif = hi.astype(F32)
    hhi_ref[...] = hi
    hlo_ref[...] = (h2 - hif).astype(BF16)
    _store_row_tiles(hp_ref, hif)


def _out_proj(merged, w_out, x_ctx, x_lat, norm_w, mod4, n_ctx_tok, dec_seq):
    t = merged.shape[0]
    tm = TM_NORM
    row = lambda i: (i, 0)
    const = lambda i: (0, 0)
    return pl.pallas_call(
        functools.partial(_outproj_kernel, n_ctx_tok=n_ctx_tok),
        grid=(t // tm,),
        in_specs=[pl.BlockSpec((tm, D_MODEL), row),
                  pl.BlockSpec((D_MODEL, D_MODEL), const),
                  *_path_specs(tm, D_MODEL, n_ctx_tok),
                  _mod_spec(2, tm, n_ctx_tok, dec_seq),
                  pl.BlockSpec((1, D_MODEL), const),
                  _mod_spec(4, tm, n_ctx_tok, dec_seq),
                  _mod_spec(3, tm, n_ctx_tok, dec_seq)],
        out_specs=[pl.BlockSpec((tm, D_MODEL), row),
                   pl.BlockSpec((tm, D_MODEL), row),
                   pl.BlockSpec((tm, D_MODEL), row),
                   pl.BlockSpec((tm * ROW_TILES, LANES), row)],
        out_shape=[jax.ShapeDtypeStruct((t, D_MODEL), F32),
                   jax.ShapeDtypeStruct((t, D_MODEL), BF16),
                   jax.ShapeDtypeStruct((t, D_MODEL), BF16),
                   jax.ShapeDtypeStruct((t * ROW_TILES, LANES), U32)],
        compiler_params=_cparams(("parallel",), 48),
        name="out_proj_norm",
    )(merged, w_out, x_ctx, x_lat, mod4, norm_w, mod4, mod4)


def _router_kernel(hhi_ref, hlo_ref, wrt_ref, bias_ref, eidx_ref, wts_ref, rank_ref, cnt_ref, carry):
    tm = hhi_ref.shape[0]
    n_e = N_EXPERTS

    @pl.when(pl.program_id(0) == 0)
    def _():
        carry[...] = jnp.zeros_like(carry)

    wr = wrt_ref[...]
    wr_hi = wr.astype(BF16)
    wr_lo = (wr - wr_hi.astype(F32)).astype(BF16)
    hh = hhi_ref[...]
    hl = hlo_ref[...]
    nt_dims = (((1,), (1,)), ((), ()))
    logits = (lax.dot_general(wr_hi, hh, nt_dims, preferred_element_type=F32)
              + lax.dot_general(wr_hi, hl, nt_dims, preferred_element_type=F32)
              + lax.dot_general(wr_lo, hh, nt_dims, preferred_element_type=F32))
    scores = _sigmoid(logits)
    choice = scores + bias_ref[...]
    neg = -jnp.inf
    row = lax.broadcasted_iota(I32, (n_e, tm), 0).astype(F32)
    grow = lax.broadcasted_iota(I32, (GROUP_SIZE, tm), 0).astype(F32)

    gs = []
    for g in range(N_GROUPS):
        blk = choice[g * GROUP_SIZE:(g + 1) * GROUP_SIZE]
        m1 = jnp.max(blk, axis=0, keepdims=True)
        i1 = jnp.min(jnp.where(blk == m1, grow, float(GROUP_SIZE)), axis=0, keepdims=True)
        m2 = jnp.max(jnp.where(grow == i1, neg, blk), axis=0, keepdims=True)
        gs.append(m1 + m2)
    gsc = jnp.concatenate(gs, axis=0)
    gid = lax.broadcasted_iota(I32, (N_GROUPS, tm), 0).astype(F32)
    gsel = jnp.zeros((N_GROUPS, tm), F32)
    for _ in range(TOPK_GROUPS):
        m = jnp.max(gsc, axis=0, keepdims=True)
        idx = jnp.min(jnp.where(gsc == m, gid, float(N_GROUPS)), axis=0, keepdims=True)
        hit = gid == idx
        gsel = jnp.where(hit, 1.0, gsel)
        gsc = jnp.where(hit, neg, gsc)
    masked = jnp.concatenate(
        [jnp.where(gsel[g:g + 1] > 0.0, choice[g * GROUP_SIZE:(g + 1) * GROUP_SIZE], neg)
         for g in range(N_GROUPS)], axis=0)

    idxs, ws = [], []
    onehot = jnp.zeros((n_e, tm), F32)
    for _ in range(TOP_K):
        m = jnp.max(masked, axis=0, keepdims=True)
        idx = jnp.min(jnp.where(masked == m, row, float(n_e)), axis=0, keepdims=True)
        hit = row == idx
        ws.append(jnp.sum(jnp.where(hit, scores, 0.0), axis=0, keepdims=True))
        masked = jnp.where(hit, neg, masked)
        onehot = jnp.where(hit, 1.0, onehot)
        idxs.append(idx)
    w = jnp.concatenate(ws, axis=0)
    wts_ref[...] = w / jnp.sum(w, axis=0, keepdims=True) * ROUTED_SCALE
    eidx_ref[...] = jnp.concatenate(idxs, axis=0).astype(I32)

    t_src = lax.broadcasted_iota(I32, (tm, tm), 0)
    t_dst = lax.broadcasted_iota(I32, (tm, tm), 1)
    before = jnp.where(t_src < t_dst, 1.0, 0.0).astype(BF16)
    prior = jnp.dot(onehot.astype(BF16), before, preferred_element_type=F32) + carry[:, :1]
    rank_ref[...] = jnp.concatenate(
        [jnp.sum(jnp.where(row == idx, prior, 0.0), axis=0, keepdims=True) for idx in idxs],
        axis=0).astype(I32)
    carry[...] = carry[...] + jnp.sum(onehot, axis=1, keepdims=True)
    cnt_ref[...] = carry[...]


def _router(h_hi, h_lo, wr_t, bias_col):
    t = h_hi.shape[0]
    tm = 512
    tok = lambda i: (0, i)
    const = lambda i: (0, 0)
    return pl.pallas_call(
        _router_kernel,
        grid=(t // tm,),
        in_specs=[pl.BlockSpec((tm, D_MODEL), lambda i: (i, 0)),
                  pl.BlockSpec((tm, D_MODEL), lambda i: (i, 0)),
                  pl.BlockSpec((N_EXPERTS, D_MODEL), const),
                  pl.BlockSpec((N_EXPERTS, 1), const)],
        out_specs=[pl.BlockSpec((TOP_K, tm), tok),
                   pl.BlockSpec((TOP_K, tm), tok),
                   pl.BlockSpec((TOP_K, tm), tok),
                   pl.BlockSpec((N_EXPERTS, LANES), const)],
        out_shape=[jax.ShapeDtypeStruct((TOP_K, t), I32),
                   jax.ShapeDtypeStruct((TOP_K, t), F32),
                   jax.ShapeDtypeStruct((TOP_K, t), I32),
                   jax.ShapeDtypeStruct((N_EXPERTS, LANES), F32)],
        scratch_shapes=[pltpu.VMEM((N_EXPERTS, LANES), F32)],
        compiler_params=_cparams(("arbitrary",), 40),
        name="router_topk",
    )(h_hi, h_lo, wr_t, bias_col)


def _dest_kernel(eidx_ref, rank_ref, pstart_ref, dest_ref):
    tm = eidx_ref.shape[1]
    row = lax.broadcasted_iota(I32, (N_EXPERTS, tm), 0)
    pstart = pstart_ref[...]
    eidx = eidx_ref[...]
    base = jnp.concatenate(
        [jnp.sum(jnp.where(row == eidx[k:k + 1], pstart, 0.0), axis=0, keepdims=True)
         for k in range(TOP_K)], axis=0)
    dest_ref[...] = (base.astype(I32) + rank_ref[...]) * ROW_TILES


def _dest_rows(eidx, rank, pstart_col):
    t = eidx.shape[1]
    tm = 1024
    tok = lambda i: (0, i)
    return pl.pallas_call(
        _dest_kernel,
        grid=(t // tm,),
        in_specs=[pl.BlockSpec((TOP_K, tm), tok),
                  pl.BlockSpec((TOP_K, tm), tok),
                  pl.BlockSpec((N_EXPERTS, 1), lambda i: (0, 0))],
        out_specs=pl.BlockSpec((TOP_K, tm), tok),
        out_shape=jax.ShapeDtypeStruct((TOP_K, t), I32),
        compiler_params=_cparams(("parallel",)),
        name="dest_rows",
    )(eidx, rank, pstart_col)


def _dispatch_kernel(fill_ref, meta_ref, dest_ref, hp_ref, xs_ref, zbuf, sem, fill_sem):
    tm = hp_ref.shape[0]
    block_rows = zbuf.shape[0]
    n_blocks = xs_ref.shape[0] // block_rows

    def zero_fills(start):
        def run(cp):
            cp.start() if start else cp.wait()

        def per_expert(e, c):
            row = fill_ref[0, e]
            left = fill_ref[1, e]
            p = EXPERT_BM // 2
            while p >= 1:
                take = left & p

                @pl.when(take != 0)
                def _(row=row, p=p):
                    dst = xs_ref.at[pl.ds(pl.multiple_of(row * ROW_TILES, ROW_TILES), p * ROW_TILES)]
                    run(pltpu.make_async_copy(zbuf.at[pl.ds(0, p * ROW_TILES)], dst, fill_sem))

                row = row + take
                p //= 2
            return c

        def per_tail_block(b, c):
            dst = xs_ref.at[pl.ds(pl.multiple_of(b * block_rows, block_rows), block_rows)]
            run(pltpu.make_async_copy(zbuf, dst, fill_sem))
            return c

        lax.fori_loop(0, N_EXPERTS, per_expert, 0)
        lax.fori_loop(meta_ref[0], n_blocks, per_tail_block, 0)

    @pl.when(pl.program_id(0) == 0)
    def _():
        zbuf[...] = jnp.zeros_like(zbuf)
        zero_fills(True)

    def issue(t, c):
        src = hp_ref.at[pl.ds(pl.multiple_of(t * ROW_TILES, ROW_TILES), ROW_TILES)]
        for k in range(TOP_K):
            r = pl.multiple_of(dest_ref[t * TOP_K + k], ROW_TILES)
            pltpu.make_async_copy(src, xs_ref.at[pl.ds(r, ROW_TILES)], sem).start(priority=k % 2)
        return c

    lax.fori_loop(0, tm // ROW_TILES, issue, 0)
    for _ in range(TOP_K):
        pltpu.make_async_copy(hp_ref, xs_ref.at[pl.ds(0, tm)], sem).wait()

    @pl.when(pl.program_id(0) == 0)
    def _():
        zero_fills(False)


def _dispatch(fill, meta, dest, hp, n_blocks):
    tm = TM_NORM * ROW_TILES
    block_rows = EXPERT_BM * ROW_TILES
    return pl.pallas_call(
        _dispatch_kernel,
        grid=(hp.shape[0] // tm,),
        in_specs=[pl.BlockSpec(memory_space=pltpu.SMEM),
                  pl.BlockSpec(memory_space=pltpu.SMEM),
                  pl.BlockSpec((TM_NORM * TOP_K,), lambda i: (i,), memory_space=pltpu.SMEM),
                  pl.BlockSpec((tm, LANES), lambda i: (i, 0))],
        out_specs=pl.BlockSpec(memory_space=pl.ANY),
        out_shape=jax.ShapeDtypeStruct((n_blocks * block_rows, LANES), U32),
        scratch_shapes=[pltpu.VMEM((block_rows, LANES), U32),
                        pltpu.SemaphoreType.DMA(()),
                        pltpu.SemaphoreType.DMA(())],
        compiler_params=_cparams(("arbitrary",)),
        name="moe_dispatch",
    )(fill, meta, dest, hp)


N_WSLOTS = 2


def _expert_kernel(uidx_ref, ue_ref, meta_ref, xs_ref, wi_hbm, wo_hbm, ys_ref,
                   wi_st, wo_st, wi_s, wo_s, sem):
    b = pl.program_id(0)
    n_used = meta_ref[0]
    n_ue = meta_ref[1]

    def weight_copies(j, slot):
        e = ue_ref[j]
        copies = []
        for m, (hbm, st) in enumerate(((wi_hbm, wi_st), (wo_hbm, wo_st))):
            half = hbm.shape[1] // 2
            for q in range(2):
                rows = pl.ds(q * half, half)
                copies.append((pltpu.make_async_copy(hbm.at[e, rows], st.at[slot, rows], sem.at[2 * m + q, slot]), q))
        return copies

    @pl.when(b < n_used)
    def _():
        j = uidx_ref[b]

        @pl.when(b == 0)
        def _():
            for cp, prio in weight_copies(0, 0):
                cp.start(priority=prio)

            @pl.when(n_ue > 1)
            def _():
                for cp, prio in weight_copies(1, 1):
                    cp.start(priority=prio)

        @pl.when((b == 0) | (j != uidx_ref[jnp.maximum(b - 1, 0)]))
        def _():
            slot = j % N_WSLOTS
            for cp, _ in weight_copies(j, slot):
                cp.wait()
            rows = 256

            def cast_in(i, c):
                r = pl.multiple_of(i * rows, rows)
                wi_s[pl.ds(r, rows), :] = wi_st[slot, pl.ds(r, rows), :].astype(BF16)
                return c

            def cast_out(i, c):
                r = pl.multiple_of(i * rows, rows)
                wo_s[pl.ds(r, rows), :] = wo_st[slot, pl.ds(r, rows), :].astype(BF16)
                return c

            lax.fori_loop(0, D_MODEL // rows, cast_in, 0)
            lax.fori_loop(0, D_EXPERT // rows, cast_out, 0)

            @pl.when(j + N_WSLOTS < n_ue)
            def _():
                for cp, prio in weight_copies(j + N_WSLOTS, slot):
                    cp.start(priority=prio)

        bm = EXPERT_BM
        x = jnp.concatenate([p.astype(BF16) for p in _load_row_tiles(xs_ref, bm)], axis=1)
        gu = jnp.dot(x, wi_s[...], preferred_element_type=F32)
        act = (_silu(gu[:, :D_EXPERT]) * gu[:, D_EXPERT:]).astype(BF16)
        _store_row_tiles(ys_ref, jnp.dot(act, wo_s[...], preferred_element_type=F32))


def _experts(uidx, ue, meta, xs, w_exp_in, w_exp_out):
    rows = EXPERT_BM * ROW_TILES
    nb = xs.shape[0] // rows
    live = lambda b, uidx, ue, meta: (jnp.minimum(b, meta[0] - 1), 0)
    grid_spec = pltpu.PrefetchScalarGridSpec(
        num_scalar_prefetch=3,
        grid=(nb,),
        in_specs=[pl.BlockSpec((rows, LANES), live),
                  pl.BlockSpec(memory_space=pl.ANY),
                  pl.BlockSpec(memory_space=pl.ANY)],
        out_specs=pl.BlockSpec((rows, LANES), live),
        scratch_shapes=[pltpu.VMEM((N_WSLOTS, D_MODEL, 2 * D_EXPERT), F32),
                        pltpu.VMEM((N_WSLOTS, D_EXPERT, D_MODEL), F32),
                        pltpu.VMEM((D_MODEL, 2 * D_EXPERT), BF16),
                        pltpu.VMEM((D_EXPERT, D_MODEL), BF16),
                        pltpu.SemaphoreType.DMA((4, N_WSLOTS))],
    )
    return pl.pallas_call(
        _expert_kernel,
        grid_spec=grid_spec,
        out_shape=jax.ShapeDtypeStruct(xs.shape, U32),
        input_output_aliases={3: 0},
        compiler_params=_cparams(("arbitrary",), 52),
        name="routed_experts",
    )(uidx, ue, meta, xs, w_exp_in, w_exp_out)


def _combine_kernel(dest_ref, dnext_ref, wt_ref, x1_ref, hhi_ref, g2_ref, wsi_ref, wso_ref, nf_ref, ys_ref,
                    yc_ref, yl_ref, rows, moe_scr, sem, *, n_ctx_tok):
    tm = x1_ref.shape[0]
    i = pl.program_id(0)
    n = pl.num_programs(0)

    def gather(idx_ref, slot):
        def issue(t, c):
            for k in range(TOP_K):
                r = pl.multiple_of(idx_ref[t * TOP_K + k], ROW_TILES)
                dst = rows.at[slot, k, pl.ds(pl.multiple_of(t * ROW_TILES, ROW_TILES), ROW_TILES)]
                pltpu.make_async_copy(ys_ref.at[pl.ds(r, ROW_TILES)], dst, sem.at[slot]).start(priority=k % 2)
            return c

        lax.fori_loop(0, tm, issue, 0)

    @pl.when(i == 0)
    def _():
        gather(dest_ref, 0)

    slot = i % 2
    nslot = (i + 1) % 2
    for k in range(TOP_K):
        pltpu.make_async_copy(ys_ref.at[pl.ds(0, tm * ROW_TILES)], rows.at[slot, k], sem.at[slot]).wait()

    for t in range(tm):
        for k in range(TOP_K):
            r = pl.multiple_of(dnext_ref[t * TOP_K + k], ROW_TILES)
            dst = rows.at[nslot, k, pl.ds(t * ROW_TILES, ROW_TILES)]
            pltpu.make_async_copy(ys_ref.at[pl.ds(r, ROW_TILES)], dst, sem.at[nslot]).start(priority=k % 2)

    su = jnp.dot(hhi_ref[...], wsi_ref[...], preferred_element_type=F32)
    sact = (_silu(su[:, :D_SHARED]) * su[:, D_SHARED:]).astype(BF16)
    shared = jnp.dot(sact, wso_ref[...], preferred_element_type=F32)

    sub = 128
    for r0 in range(0, tm, sub):
        wks = [wt_ref[r0:r0 + sub, k:k + 1] for k in range(TOP_K)]
        for s in range(ROW_TILES):
            acc_lo = acc_hi = None
            for k in range(TOP_K):
                lo, hi = _unpack_bf16_pair(rows[slot, k, pl.ds(r0 * ROW_TILES + s, sub, SUBLANES), :])
                acc_lo = wks[k] * lo if acc_lo is None else acc_lo + wks[k] * lo
                acc_hi = wks[k] * hi if acc_hi is None else acc_hi + wks[k] * hi
            col = 2 * LANES * s
            moe_scr[r0:r0 + sub, col:col + LANES] = acc_lo
            moe_scr[r0:r0 + sub, col + LANES:col + 2 * LANES] = acc_hi
    moe = moe_scr[...] + shared
    x2 = x1_ref[...] + g2_ref[...] * moe
    y = _rms(x2, nf_ref[...])
    is_ctx = i * tm < n_ctx_tok

    @pl.when(is_ctx)
    def _():
        yc_ref[...] = y

    @pl.when(jnp.logical_not(is_ctx))
    def _():
        yl_ref[...] = y

    @pl.when(i == n - 1)
    def _():
        for k in range(TOP_K):
            pltpu.make_async_copy(ys_ref.at[pl.ds(0, tm * ROW_TILES)], rows.at[nslot, k], sem.at[nslot]).wait()


def _combine(dest, wts_t, x1, h_hi, mod4, w_sh_in, w_sh_out, norm_final, ys, n_ctx_tok, dec_seq):
    t = x1.shape[0]
    tm = TM_NORM
    n = t // tm
    row = lambda i: (i, 0)
    const = lambda i: (0, 0)
    return pl.pallas_call(
        functools.partial(_combine_kernel, n_ctx_tok=n_ctx_tok),
        grid=(n,),
        in_specs=[pl.BlockSpec((tm * TOP_K,), lambda i: (i,), memory_space=pltpu.SMEM),
                  pl.BlockSpec((tm * TOP_K,), lambda i: (jnp.minimum(i + 1, n - 1),), memory_space=pltpu.SMEM),
                  pl.BlockSpec((tm, TOP_K), row),
                  pl.BlockSpec((tm, D_MODEL), row),
                  pl.BlockSpec((tm, D_MODEL), row),
                  _mod_spec(5, tm, n_ctx_tok, dec_seq),
                  pl.BlockSpec((D_MODEL, 2 * D_SHARED), const),
                  pl.BlockSpec((D_SHARED, D_MODEL), const),
                  pl.BlockSpec((1, D_MODEL), const),
                  pl.BlockSpec(memory_space=pl.ANY)],
        out_specs=list(_path_specs(tm, D_MODEL, n_ctx_tok)),
        out_shape=[jax.ShapeDtypeStruct((n_ctx_tok, D_MODEL), F32),
                   jax.ShapeDtypeStruct((t - n_ctx_tok, D_MODEL), F32)],
        scratch_shapes=[pltpu.VMEM((2, TOP_K, tm * ROW_TILES, LANES), U32),
                        pltpu.VMEM((tm, D_MODEL), F32),
                        pltpu.SemaphoreType.DMA((2,))],
        compiler_params=_cparams(("arbitrary",), 56),
        name="moe_combine",
    )(dest, dest, wts_t, x1, h_hi, mod4, w_sh_in, w_sh_out, norm_final, ys)


def _rope_tables(tm, dec_seq):
    rows = dec_seq // GRID_W
    row = jnp.broadcast_to(jnp.arange(rows, dtype=F32)[:, None], (rows, GRID_W)).reshape(-1)
    col = jnp.broadcast_to(jnp.arange(GRID_W, dtype=F32)[None, :], (rows, GRID_W)).reshape(-1)
    axis_dim = MLA_ROPE // 2
    freqs = jnp.power(ROPE_BASE, -jnp.arange(0, axis_dim, 2, dtype=F32) / axis_dim)
    ang_r = row[:, None] * freqs
    ang_c = col[:, None] * freqs
    ang = jnp.concatenate([ang_r, ang_r, ang_c, ang_c], axis=-1)
    pad = LANES - MLA_ROPE
    cos = jnp.concatenate([jnp.cos(ang), jnp.ones((dec_seq, pad), F32)], axis=1)
    sin = jnp.concatenate([jnp.sin(ang), jnp.zeros((dec_seq, pad), F32)], axis=1)
    cos = jnp.concatenate([jnp.ones((tm, LANES), F32), cos], axis=0)
    sin = jnp.concatenate([jnp.zeros((tm, LANES), F32), sin], axis=0)
    return cos, sin


def _block_plan(counts, n_blocks):
    bm = EXPERT_BM
    nblk = (counts + bm - 1) // bm
    bend = jnp.cumsum(nblk)
    bstart = bend - nblk
    present = nblk > 0
    pcum = jnp.cumsum(present.astype(I32))
    bidx = jnp.arange(n_blocks, dtype=I32)
    uidx = jnp.sum((present[None, :] & (bstart[None, :] <= bidx[:, None])).astype(I32), axis=1) - 1
    order = jnp.arange(N_EXPERTS, dtype=I32)
    ue = jnp.minimum(jnp.sum((pcum[None, :] <= order[:, None]).astype(I32), axis=1), N_EXPERTS - 1)
    meta = jnp.stack([bend[-1], pcum[-1]]).astype(I32)
    pstart = bstart * bm
    fill = jnp.stack([pstart + counts, nblk * bm - counts]).astype(I32)
    return pstart, uidx.astype(I32), ue.astype(I32), meta, fill


def kernel(x_prompt, x_sample, c, cache_mla_ckv, cache_mla_krope, state_ret, c_ctx, w_mod, b_mod, norm_mix, norm_ffn, w_in, ret_decay_logit, ret_gn_w, w_ret_o, mla_q_norm, w_q_b, mla_kv_norm, w_kv_b, w_mla_o, w_out, w_router, router_bias, w_exp_in, w_exp_out, w_shared_in, w_shared_out, norm_final):
    n_ctx, seq, d = x_prompt.shape
    n_lat, dec_seq, _ = x_sample.shape
    past = cache_mla_ckv.shape[2]
    n_ctx_tok = n_ctx * seq
    n_lat_tok = n_lat * dec_seq
    t = n_ctx_tok + n_lat_tok
    assert d == D_MODEL and w_mod.shape[0] == 1 and n_lat + 1 <= MOD_ROWS
    assert seq % TM_NORM == 0 and dec_seq % TM_NORM == 0 and n_ctx_tok % dec_seq == 0

    x_ctx = x_prompt.reshape(n_ctx_tok, d)
    x_lat = x_sample.reshape(n_lat_tok, d)

    cond = jnp.concatenate([c_ctx[None, :], c, jnp.zeros((MOD_ROWS - 1 - n_lat, d), F32)], axis=0)
    mod = _modulation(cond, w_mod.reshape(d, N_MOD * d), b_mod.reshape(1, N_MOD * d))
    mod4 = mod.reshape(MOD_ROWS, N_MOD, 1, d)

    w_in2 = w_in.reshape(d, -1)
    o_g = 2 * RET_QK_W + RET_V_W
    o_qa = o_g + RET_V_W
    o_kpe = o_qa + MLA_Q_RANK + MLA_KV_RANK
    o_gr = o_kpe + MLA_ROPE
    w_qkvg = w_in2[:, :o_qa].astype(BF16)
    w_low = jnp.pad(w_in2[:, o_qa:o_gr], ((0, 0), (0, LANES - MLA_ROPE))).astype(BF16)
    w_gates = w_in2[:, o_gr:].astype(BF16)
    wq3 = w_q_b.reshape(MLA_Q_RANK, MLA_HEADS, MLA_NOPE + MLA_ROPE)
    w_qb = jnp.pad(wq3, ((0, 0), (0, 0), (0, Q_HEAD_W - MLA_NOPE - MLA_ROPE))
                   ).reshape(MLA_Q_RANK, MLA_HEADS * Q_HEAD_W).astype(BF16)
    w_kvb = w_kv_b.reshape(MLA_KV_RANK, -1).astype(BF16)
    cos_t, sin_t = _rope_tables(TM_NORM, dec_seq)

    h1 = _norm_modulate(x_ctx, x_lat, norm_mix.reshape(1, d), mod4, n_ctx_tok, dec_seq)
    qkvg = _matmul(h1, w_qkvg, tm=2048, tn=1024, name="proj_qkvg")
    gates = _matmul(h1, w_gates, tm=2048, tn=1024, act="sigmoid", name="proj_gates")
    qn, ckv, ckv_b, kpe_raw, kpe_rot = _lowrank(
        h1, w_low, mla_q_norm.reshape(1, -1), mla_kv_norm.reshape(1, -1), cos_t, sin_t, n_ctx_tok, dec_seq)

    logits = ret_decay_logit.reshape(2, RET_HEADS)
    gn_w = ret_gn_w.reshape(1, RET_V_W)
    ctx_group = dec_seq // seq
    r_ctx, ret_state = _retention(qkvg, logits, gn_w, None, n_seq=n_ctx, seq=seq, group=ctx_group,
                                  row_block0=0, emit_state=True)
    (r_lat,) = _retention(qkvg, logits, gn_w, state_ret.reshape(n_lat, 2, RET_HEADS, RET_DK, RET_DV),
                          n_seq=n_lat, seq=dec_seq, group=1, row_block0=n_ctx_tok // dec_seq,
                          emit_state=False)

    q_all = _q_up(qn, w_qb, cos_t, sin_t, n_ctx_tok, dec_seq)
    kv_all = _matmul(ckv_b, w_kvb, tm=1024, tn=1024, name="kv_up")
    kv_ctx = _matmul(cache_mla_ckv.reshape(n_lat * past, MLA_KV_RANK), w_kvb, tm=1024, tn=1024,
                     name="kv_up_cache")
    kpe_ctx = jnp.pad(cache_mla_krope.reshape(n_lat * past, MLA_ROPE),
                      ((0, 0), (0, LANES - MLA_ROPE))).astype(BF16)
    a_ctx = _attention_ctx(q_all, kv_all, kpe_rot, n_ctx, seq)
    a_lat = _attention_lat(q_all, kv_ctx, kv_all, kpe_ctx, kpe_rot, n_lat, dec_seq, past, n_ctx_tok)

    merged = _merge(r_ctx, r_lat, a_ctx, a_lat, w_ret_o.reshape(RET_V_W, d).astype(BF16),
                    w_mla_o.reshape(MLA_HEADS * MLA_V, d).astype(BF16), gates, n_ctx_tok)
    x1, h_hi, h_lo, hp = _out_proj(merged, w_out.reshape(d, d).astype(BF16), x_ctx, x_lat,
                                   norm_ffn.reshape(1, d), mod4, n_ctx_tok, dec_seq)

    eidx, wts, rank, cnt = _router(h_hi, h_lo, w_router.reshape(d, N_EXPERTS).T,
                                   router_bias.reshape(N_EXPERTS, 1))
    n_blocks = t * TOP_K // EXPERT_BM + N_EXPERTS
    pstart, uidx, ue, meta, fill = _block_plan(cnt[:, 0].astype(I32), n_blocks)
    dest = _dest_rows(eidx, rank, pstart.astype(F32).reshape(N_EXPERTS, 1)).T.reshape(-1)
    xs = _dispatch(fill, meta, dest, hp, n_blocks)
    ys = _experts(uidx, ue, meta, xs, w_exp_in.reshape(N_EXPERTS, d, 2 * D_EXPERT),
                  w_exp_out.reshape(N_EXPERTS, D_EXPERT, d))
    y_ctx, y_lat = _combine(dest, wts.T, x1, h_hi, mod4, w_shared_in.reshape(d, 2 * D_SHARED).astype(BF16),
                            w_shared_out.reshape(D_SHARED, d).astype(BF16), norm_final.reshape(1, d), ys,
                            n_ctx_tok, dec_seq)

    y_prompt = y_ctx.reshape(n_ctx, seq, d)
    y_sample = y_lat.reshape(n_lat, dec_seq, d)
    new_ckv = ckv[:n_ctx_tok].reshape(n_ctx, 1, seq, MLA_KV_RANK)
    new_krope = kpe_raw[:n_ctx_tok, :MLA_ROPE].reshape(n_ctx, 1, seq, MLA_ROPE)
    new_state = ret_state.reshape(n_ctx, 1, 2, RET_HEADS, RET_DK, RET_DV)
    return (y_prompt, y_sample, new_ckv, new_krope, new_state)
```

```python
import functools

import jax
import jax.numpy as jnp
from jax import lax
from jax.experimental import pallas as pl
from jax.experimental.pallas import tpu as pltpu

F32 = jnp.float32
BF16 = jnp.bfloat16
U32 = jnp.uint32
I32 = jnp.int32

D_MODEL = 2048
GRID_W = 64
NORM_EPS = 1e-6
RET_HEADS = 8
RET_DK = 128
RET_DV = 256
RET_CHUNK = 128
RET_QK_W = RET_HEADS * RET_DK
RET_V_W = RET_HEADS * RET_DV
MLA_HEADS = 16
MLA_Q_RANK = 512
MLA_KV_RANK = 512
MLA_NOPE = 128
MLA_ROPE = 64
MLA_V = 128
ROPE_BASE = 10000.0
N_EXPERTS = 256
TOP_K = 8
N_GROUPS = 8
TOPK_GROUPS = 4
GROUP_SIZE = N_EXPERTS // N_GROUPS
D_EXPERT = 512
D_SHARED = 512
ROUTED_SCALE = 2.5

LANES = 128
SUBLANES = 8
ROW_TILES = D_MODEL // (2 * LANES)
assert ROW_TILES == SUBLANES
Q_HEAD_W = 256
N_MOD = 6
MOD_ROWS = 16
TM_NORM = 256
EXPERT_BM = 256
MIB = 1 << 20


def _cparams(sem, vmem_mib=None):
    kw = dict(dimension_semantics=sem)
    if vmem_mib is not None:
        kw["vmem_limit_bytes"] = vmem_mib * MIB
    return pltpu.CompilerParams(**kw)


def _sigmoid(x):
    return 1.0 / (1.0 + jnp.exp(-x))


def _silu(x):
    return x * _sigmoid(x)


def _rms(x, w):
    ms = jnp.mean(x * x, axis=-1, keepdims=True)
    return x * lax.rsqrt(ms + NORM_EPS) * w


def _pack_bf16_pair(lo, hi):
    return pltpu.pack_elementwise([lo, hi], packed_dtype=BF16)


def _unpack_bf16_pair(w):
    lo = pltpu.unpack_elementwise(w, index=0, packed_dtype=BF16, unpacked_dtype=F32)
    hi = pltpu.unpack_elementwise(w, index=1, packed_dtype=BF16, unpacked_dtype=F32)
    return lo, hi


def _store_row_tiles(ref, x):
    m = x.shape[0]
    for s in range(ROW_TILES):
        lo = x[:, 2 * LANES * s:2 * LANES * s + LANES]
        hi = x[:, 2 * LANES * s + LANES:2 * LANES * (s + 1)]
        ref[pl.ds(s, m, SUBLANES), :] = _pack_bf16_pair(lo, hi)


def _load_row_tiles(ref, m):
    parts = []
    for s in range(ROW_TILES):
        parts.extend(_unpack_bf16_pair(ref[pl.ds(s, m, SUBLANES), :]))
    return parts


def _rope128(x, cos, sin):
    lane = lax.broadcasted_iota(I32, x.shape, 1)
    first = (lane & 31) < 16
    rot = jnp.where(first, -pltpu.roll(x, LANES - 16, 1), pltpu.roll(x, 16, 1))
    return x * cos + rot * sin


def _mod_kernel(c_ref, w_ref, b_ref, o_ref):
    s = _silu(c_ref[...]).astype(BF16)
    o_ref[...] = jnp.dot(s, w_ref[...].astype(BF16), preferred_element_type=F32) + b_ref[...]


def _modulation(cond, w_mod, b_mod):
    n = w_mod.shape[1]
    tn = 1024
    return pl.pallas_call(
        _mod_kernel,
        grid=(n // tn,),
        in_specs=[pl.BlockSpec((MOD_ROWS, D_MODEL), lambda j: (0, 0)),
                  pl.BlockSpec((D_MODEL, tn), lambda j: (0, j)),
                  pl.BlockSpec((1, tn), lambda j: (0, j))],
        out_specs=pl.BlockSpec((MOD_ROWS, tn), lambda j: (0, j)),
        out_shape=jax.ShapeDtypeStruct((MOD_ROWS, n), F32),
        compiler_params=_cparams(("arbitrary",), 40),
        name="modulation",
    )(cond, w_mod, b_mod)


def _mod_spec(which, tm, n_ctx_tok, dec_seq):
    def index(i, *_):
        tok = i * tm
        row = jnp.where(tok < n_ctx_tok, 0, 1 + (tok - n_ctx_tok) // dec_seq)
        return (row, which, 0, 0)
    return pl.BlockSpec((None, None, 1, D_MODEL), index)


def _path_specs(tm, width, n_ctx_tok, axis=0):
    n_ctx_tiles = n_ctx_tok // tm

    def ctx(*idx):
        return (jnp.minimum(idx[axis], n_ctx_tiles - 1), 0)

    def lat(*idx):
        return (jnp.maximum(idx[axis] - n_ctx_tiles, 0), 0)

    return pl.BlockSpec((tm, width), ctx), pl.BlockSpec((tm, width), lat)


def _path_pick(tile, tm, n_ctx_tok, ctx_ref, lat_ref):
    return jnp.where(tile * tm < n_ctx_tok, ctx_ref[...], lat_ref[...])


def _normmod_kernel(xc_ref, xl_ref, w_ref, sc_ref, sh_ref, o_ref, *, n_ctx_tok):
    x = _path_pick(pl.program_id(0), o_ref.shape[0], n_ctx_tok, xc_ref, xl_ref)
    y = _rms(x, w_ref[...])
    o_ref[...] = (y * (1.0 + sc_ref[...]) + sh_ref[...]).astype(BF16)


def _norm_modulate(x_ctx, x_lat, w, mod4, n_ctx_tok, dec_seq):
    t = x_ctx.shape[0] + x_lat.shape[0]
    tm = TM_NORM
    return pl.pallas_call(
        functools.partial(_normmod_kernel, n_ctx_tok=n_ctx_tok),
        grid=(t // tm,),
        in_specs=[*_path_specs(tm, D_MODEL, n_ctx_tok),
                  pl.BlockSpec((1, D_MODEL), lambda i: (0, 0)),
                  _mod_spec(1, tm, n_ctx_tok, dec_seq),
                  _mod_spec(0, tm, n_ctx_tok, dec_seq)],
        out_specs=pl.BlockSpec((tm, D_MODEL), lambda i: (i, 0)),
        out_shape=jax.ShapeDtypeStruct((t, D_MODEL), BF16),
        compiler_params=_cparams(("parallel",)),
        name="norm_modulate",
    )(x_ctx, x_lat, w, mod4, mod4)


def _mm_kernel(a_ref, w_ref, o_ref, *, act):
    acc = jnp.dot(a_ref[...].astype(BF16), w_ref[...], preferred_element_type=F32)
    if act == "sigmoid":
        acc = _sigmoid(acc)
    o_ref[...] = acc.astype(o_ref.dtype)


def _matmul(a, w, *, tm, tn, act=None, name):
    m, k = a.shape
    n = w.shape[1]
    tm = min(tm, m)
    assert m % tm == 0 and n % tn == 0
    return pl.pallas_call(
        functools.partial(_mm_kernel, act=act),
        grid=(n // tn, m // tm),
        in_specs=[pl.BlockSpec((tm, k), lambda j, i: (i, 0)),
                  pl.BlockSpec((k, tn), lambda j, i: (0, j))],
        out_specs=pl.BlockSpec((tm, tn), lambda j, i: (i, j)),
        out_shape=jax.ShapeDtypeStruct((m, n), BF16),
        compiler_params=_cparams(("parallel", "parallel"), 48),
        name=name,
    )(a, w)


def _lowrank_kernel(h_ref, w_ref, qnw_ref, kvnw_ref, cos_ref, sin_ref,
                    qn_ref, ckv_ref, ckvb_ref, kpe_ref, kper_ref):
    acc = jnp.dot(h_ref[...], w_ref[...], preferred_element_type=F32)
    qn_ref[...] = _rms(acc[:, :MLA_Q_RANK], qnw_ref[...]).astype(BF16)
    ckv = _rms(acc[:, MLA_Q_RANK:MLA_Q_RANK + MLA_KV_RANK], kvnw_ref[...])
    ckv_ref[...] = ckv
    ckvb_ref[...] = ckv.astype(BF16)
    kp = acc[:, MLA_Q_RANK + MLA_KV_RANK:]
    kpe_ref[...] = kp
    kper_ref[...] = _rope128(kp, cos_ref[...], sin_ref[...]).astype(BF16)


def _rope_spec(tm, n_ctx_tok, dec_seq):
    per_seq = dec_seq // tm

    def index(i, *_):
        tok = i * tm
        return (jnp.where(tok < n_ctx_tok, 0, 1 + ((tok - n_ctx_tok) // tm) % per_seq), 0)
    return pl.BlockSpec((tm, LANES), index)


def _lowrank(h, w_low, qnw, kvnw, cos_t, sin_t, n_ctx_tok, dec_seq):
    t = h.shape[0]
    tm = TM_NORM
    n = w_low.shape[1]
    row = lambda i: (i, 0)
    const = lambda i: (0, 0)
    return pl.pallas_call(
        _lowrank_kernel,
        grid=(t // tm,),
        in_specs=[pl.BlockSpec((tm, D_MODEL), row),
                  pl.BlockSpec((D_MODEL, n), const),
                  pl.BlockSpec((1, MLA_Q_RANK), const),
                  pl.BlockSpec((1, MLA_KV_RANK), const),
                  _rope_spec(tm, n_ctx_tok, dec_seq),
                  _rope_spec(tm, n_ctx_tok, dec_seq)],
        out_specs=[pl.BlockSpec((tm, MLA_Q_RANK), row),
                   pl.BlockSpec((tm, MLA_KV_RANK), row),
                   pl.BlockSpec((tm, MLA_KV_RANK), row),
                   pl.BlockSpec((tm, LANES), row),
                   pl.BlockSpec((tm, LANES), row)],
        out_shape=[jax.ShapeDtypeStruct((t, MLA_Q_RANK), BF16),
                   jax.ShapeDtypeStruct((t, MLA_KV_RANK), F32),
                   jax.ShapeDtypeStruct((t, MLA_KV_RANK), BF16),
                   jax.ShapeDtypeStruct((t, LANES), F32),
                   jax.ShapeDtypeStruct((t, LANES), BF16)],
        compiler_params=_cparams(("parallel",), 40),
        name="lowrank_proj",
    )(h, w_low, qnw, kvnw, cos_t, sin_t)


Q_PRESCALE = (MLA_NOPE + MLA_ROPE) ** -0.5 * 1.4426950408889634


def _qb_kernel(qn_ref, w_ref, cos_ref, sin_ref, o_ref):
    qn = qn_ref[...]
    cos = cos_ref[...]
    sin = sin_ref[...]
    for h in range(MLA_HEADS):
        lo = h * Q_HEAD_W
        acc = jnp.dot(qn, w_ref[:, lo:lo + Q_HEAD_W], preferred_element_type=F32) * Q_PRESCALE
        o_ref[:, lo:lo + MLA_NOPE] = acc[:, :MLA_NOPE].astype(BF16)
        o_ref[:, lo + MLA_NOPE:lo + Q_HEAD_W] = _rope128(acc[:, MLA_NOPE:], cos, sin).astype(BF16)


def _q_up(qn, w_qb, cos_t, sin_t, n_ctx_tok, dec_seq):
    t = qn.shape[0]
    tm = TM_NORM
    n = w_qb.shape[1]
    return pl.pallas_call(
        _qb_kernel,
        grid=(t // tm,),
        in_specs=[pl.BlockSpec((tm, MLA_Q_RANK), lambda i: (i, 0)),
                  pl.BlockSpec((MLA_Q_RANK, n), lambda i: (0, 0)),
                  _rope_spec(tm, n_ctx_tok, dec_seq),
                  _rope_spec(tm, n_ctx_tok, dec_seq)],
        out_specs=pl.BlockSpec((tm, n), lambda i: (i, 0)),
        out_shape=jax.ShapeDtypeStruct((t, n), BF16),
        compiler_params=_cparams(("parallel",), 40),
        name="q_up_rope",
    )(qn, w_qb, cos_t, sin_t)


def _ret_kernel(logit_ref, q_ref, k_ref, v_ref, g_ref, gnw_ref, *rest, n_chunks, group, has_s0, emit_state):
    rest = list(rest)
    s0_ref = rest.pop(0) if has_s0 else None
    o_ref = rest.pop(0)
    sfin_ref = rest.pop(0) if emit_state else None
    sf_scr, sb_scr = rest
    c_len = RET_CHUNK
    head = pl.program_id(1)

    def log_gamma(direction):
        l = jnp.full((c_len, 1), logit_ref[direction, head], F32)
        return jnp.minimum(l, 0.0) - jnp.log1p(jnp.exp(-jnp.abs(l)))

    lgf_c = log_gamma(0)
    lgb_c = log_gamma(1)
    ii = lax.broadcasted_iota(I32, (c_len, c_len), 0).astype(F32)
    jj = lax.broadcasted_iota(I32, (c_len, c_len), 1).astype(F32)
    k_scale = RET_DK ** -0.5
    decay = (jnp.where(ii >= jj, jnp.exp(jnp.maximum(ii - jj, 0.0) * lgf_c), 0.0)
             + jnp.where(jj >= ii, jnp.exp(jnp.maximum(jj - ii, 0.0) * lgb_c), 0.0)) * k_scale
    pos = lax.broadcasted_iota(I32, (c_len, 1), 0).astype(F32)
    qdec_f = jnp.exp((pos + 1.0) * lgf_c)
    qdec_b = jnp.exp((c_len - pos) * lgb_c)
    kdec_f = jnp.exp((c_len - 1.0 - pos) * lgf_c) * k_scale
    kdec_b = jnp.exp(pos * lgb_c) * k_scale
    cdec_f = jnp.exp(c_len * lgf_c)
    cdec_b = jnp.exp(c_len * lgb_c)
    tn_dims = (((0,), (0,)), ((), ()))
    nt_dims = (((1,), (1,)), ((), ()))

    def chunk(c):
        return pl.ds(c * c_len, c_len)

    for c in range(n_chunks):
        kc = k_ref[chunk(c), :].astype(F32)
        vc = v_ref[chunk(c), :]
        sf_scr[c] = lax.dot_general((kc * kdec_f).astype(BF16), vc, tn_dims, preferred_element_type=F32)
        sb_scr[c] = lax.dot_general((kc * kdec_b).astype(BF16), vc, tn_dims, preferred_element_type=F32)

    per_seq = n_chunks // group
    for g in range(group):
        chunks = range(g * per_seq, (g + 1) * per_seq)
        s = s0_ref[g, 0] if has_s0 else jnp.zeros((RET_DK, RET_DV), F32)
        for c in chunks:
            kv = sf_scr[c]
            sf_scr[c] = s
            s = cdec_f * s + kv
        if emit_state:
            sfin_ref[g, 0] = s
        s = s0_ref[g, 1] if has_s0 else jnp.zeros((RET_DK, RET_DV), F32)
        for c in reversed(chunks):
            kv = sb_scr[c]
            sb_scr[c] = s
            s = cdec_b * s + kv
        if emit_state:
            sfin_ref[g, 1] = s

    gnw = gnw_ref[...]
    for c in range(n_chunks):
        qc = q_ref[chunk(c), :]
        kc = k_ref[chunk(c), :]
        vc = v_ref[chunk(c), :]
        sc = lax.dot_general(qc, kc, nt_dims, preferred_element_type=F32) * decay
        qf = qc.astype(F32)
        qcat = jnp.concatenate([(qf * qdec_f).astype(BF16), (qf * qdec_b).astype(BF16)], axis=1)
        scat = jnp.concatenate([sf_scr[c], sb_scr[c]], axis=0).astype(BF16)
        o = (jnp.dot(sc.astype(BF16), vc, preferred_element_type=F32)
             + jnp.dot(qcat, scat, preferred_element_type=F32))
        mu = jnp.mean(o, axis=-1, keepdims=True)
        var = jnp.mean(jnp.square(o - mu), axis=-1, keepdims=True)
        on = (o - mu) * lax.rsqrt(var + NORM_EPS) * gnw
        o_ref[chunk(c), :] = (_silu(g_ref[chunk(c), :].astype(F32)) * on).astype(BF16)


def _retention(qkvg, logits, gn_w, s0, *, n_seq, seq, group, row_block0, emit_state):
    rows = group * seq
    n_chunks = rows // RET_CHUNK
    has_s0 = s0 is not None
    assert n_seq % group == 0
    k_off = RET_QK_W // RET_DK
    v_off = 2 * RET_QK_W // RET_DV
    g_off = (2 * RET_QK_W + RET_V_W) // RET_DV
    in_specs = [pl.BlockSpec(memory_space=pltpu.SMEM),
                pl.BlockSpec((rows, RET_DK), lambda b, h: (row_block0 + b, h)),
                pl.BlockSpec((rows, RET_DK), lambda b, h: (row_block0 + b, k_off + h)),
                pl.BlockSpec((rows, RET_DV), lambda b, h: (row_block0 + b, v_off + h)),
                pl.BlockSpec((rows, RET_DV), lambda b, h: (row_block0 + b, g_off + h)),
                pl.BlockSpec((1, RET_DV), lambda b, h: (0, h))]
    args = [logits, qkvg, qkvg, qkvg, qkvg, gn_w]
    state_spec = pl.BlockSpec((group, 2, None, RET_DK, RET_DV), lambda b, h: (b, 0, h, 0, 0))
    if has_s0:
        in_specs.append(state_spec)
        args.append(s0)
    out_specs = [pl.BlockSpec((rows, RET_DV), lambda b, h: (b, h))]
    out_shape = [jax.ShapeDtypeStruct((n_seq * seq, RET_V_W), BF16)]
    if emit_state:
        out_specs.append(state_spec)
        out_shape.append(jax.ShapeDtypeStruct((n_seq, 2, RET_HEADS, RET_DK, RET_DV), F32))
    res = pl.pallas_call(
        functools.partial(_ret_kernel, n_chunks=n_chunks, group=group, has_s0=has_s0, emit_state=emit_state),
        grid=(n_seq // group, RET_HEADS),
        in_specs=in_specs,
        out_specs=out_specs,
        out_shape=out_shape,
        scratch_shapes=[pltpu.VMEM((n_chunks, RET_DK, RET_DV), F32),
                        pltpu.VMEM((n_chunks, RET_DK, RET_DV), F32)],
        compiler_params=_cparams(("parallel", "parallel")),
        name="retention_seq%d" % seq,
    )(*args)
    return res


V_EXT_W = MLA_V + LANES


def _ones_column(rows):
    lane = lax.broadcasted_iota(I32, (rows, LANES), 1)
    return jnp.where(lane == 0, 1.0, 0.0).astype(BF16)


def _softmax_pv(q, kcat, v_ext):
    s = lax.dot_general(q, kcat, (((1,), (1,)), ((), ())), preferred_element_type=F32)
    e = jnp.exp2(s - jnp.max(s, axis=-1, keepdims=True))
    o = jnp.dot(e.astype(BF16), v_ext, preferred_element_type=F32)
    return o[:, :MLA_V] / o[:, MLA_V:MLA_V + 1]


def _attn_ctx_kernel(q_ref, kv_ref, kpe_ref, o_ref):
    kpe = kpe_ref[...]
    ones = _ones_column(kpe.shape[0])
    kvw = MLA_NOPE + MLA_V
    for h in range(MLA_HEADS):
        kv = kv_ref[:, h * kvw:(h + 1) * kvw]
        kcat = jnp.concatenate([kv[:, :MLA_NOPE], kpe], axis=1)
        v_ext = jnp.concatenate([kv[:, MLA_NOPE:], ones], axis=1)
        o = _softmax_pv(q_ref[:, h * Q_HEAD_W:(h + 1) * Q_HEAD_W], kcat, v_ext)
        o_ref[:, h * MLA_V:(h + 1) * MLA_V] = o.astype(o_ref.dtype)


def _attention_ctx(q_all, kv_all, kpe_all, n_seq, seq):
    return pl.pallas_call(
        _attn_ctx_kernel,
        grid=(n_seq,),
        in_specs=[pl.BlockSpec((seq, MLA_HEADS * Q_HEAD_W), lambda b: (b, 0)),
                  pl.BlockSpec((seq, MLA_HEADS * (MLA_NOPE + MLA_V)), lambda b: (b, 0)),
                  pl.BlockSpec((seq, LANES), lambda b: (b, 0))],
        out_specs=pl.BlockSpec((seq, MLA_HEADS * MLA_V), lambda b: (b, 0)),
        out_shape=jax.ShapeDtypeStruct((n_seq * seq, MLA_HEADS * MLA_V), BF16),
        compiler_params=_cparams(("parallel",), 40),
        name="attention_ctx",
    )(q_all, kv_all, kpe_all)


def _attn_lat_kernel(q_ref, kvc_ref, kvl_ref, kpc_ref, kpl_ref, o_ref, kcat, vcat, *, past, seq):
    @pl.when(pl.program_id(2) == 0)
    def _():
        kcat[0:past, 0:MLA_NOPE] = kvc_ref[:, :MLA_NOPE]
        kcat[0:past, MLA_NOPE:] = kpc_ref[...]
        kcat[past:past + seq, 0:MLA_NOPE] = kvl_ref[:, :MLA_NOPE]
        kcat[past:past + seq, MLA_NOPE:] = kpl_ref[...]
        vcat[0:past, 0:MLA_V] = kvc_ref[:, MLA_NOPE:]
        vcat[past:past + seq, 0:MLA_V] = kvl_ref[:, MLA_NOPE:]
        vcat[:, MLA_V:] = _ones_column(past + seq)

    sub = 256
    for r in range(q_ref.shape[0] // sub):
        rows = slice(r * sub, (r + 1) * sub)
        o_ref[rows, :] = _softmax_pv(q_ref[rows, :], kcat[...], vcat[...]).astype(o_ref.dtype)


def _attention_lat(q_all, kv_ctx, kv_all, kpe_ctx, kpe_all, n_seq, seq, past, n_ctx_tok):
    tq = 1024
    nq = seq // tq
    q0 = n_ctx_tok // tq
    s0 = n_ctx_tok // seq
    kvw = MLA_NOPE + MLA_V
    return pl.pallas_call(
        functools.partial(_attn_lat_kernel, past=past, seq=seq),
        grid=(n_seq, MLA_HEADS, nq),
        in_specs=[pl.BlockSpec((tq, Q_HEAD_W), lambda b, h, i: (q0 + b * nq + i, h)),
                  pl.BlockSpec((past, kvw), lambda b, h, i: (b, h)),
                  pl.BlockSpec((seq, kvw), lambda b, h, i: (s0 + b, h)),
                  pl.BlockSpec((past, LANES), lambda b, h, i: (b, 0)),
                  pl.BlockSpec((seq, LANES), lambda b, h, i: (s0 + b, 0))],
        out_specs=pl.BlockSpec((tq, MLA_V), lambda b, h, i: (b * nq + i, h)),
        out_shape=jax.ShapeDtypeStruct((n_seq * seq, MLA_HEADS * MLA_V), BF16),
        scratch_shapes=[pltpu.VMEM((past + seq, MLA_NOPE + LANES), BF16),
                        pltpu.VMEM((past + seq, V_EXT_W), BF16)],
        compiler_params=_cparams(("parallel", "parallel", "arbitrary"), 40),
        name="attention_lat",
    )(q_all, kv_ctx, kv_all, kpe_ctx, kpe_all)


def _merge_kernel(rc_ref, rl_ref, ac_ref, al_ref, wr_ref, wm_ref, gr_ref, gm_ref, o_ref, *, n_ctx_tok):
    tile = pl.program_id(1)
    tm = o_ref.shape[0]
    r = _path_pick(tile, tm, n_ctx_tok, rc_ref, rl_ref)
    a = _path_pick(tile, tm, n_ctx_tok, ac_ref, al_ref)
    ro = jnp.dot(r, wr_ref[...], preferred_element_type=F32)
    mo = jnp.dot(a, wm_ref[...], preferred_element_type=F32)
    o_ref[...] = (gr_ref[...].astype(F32) * ro + gm_ref[...].astype(F32) * mo).astype(BF16)


def _merge(r_ctx, r_lat, a_ctx, a_lat, w_ret_o, w_mla_o, gates, n_ctx_tok):
    t = gates.shape[0]
    tm, tn = 512, 1024
    nj = D_MODEL // tn
    return pl.pallas_call(
        functools.partial(_merge_kernel, n_ctx_tok=n_ctx_tok),
        grid=(nj, t // tm),
        in_specs=[*_path_specs(tm, RET_V_W, n_ctx_tok, axis=1),
                  *_path_specs(tm, MLA_HEADS * MLA_V, n_ctx_tok, axis=1),
                  pl.BlockSpec((RET_V_W, tn), lambda j, i: (0, j)),
                  pl.BlockSpec((MLA_HEADS * MLA_V, tn), lambda j, i: (0, j)),
                  pl.BlockSpec((tm, tn), lambda j, i: (i, j)),
                  pl.BlockSpec((tm, tn), lambda j, i: (i, nj + j))],
        out_specs=pl.BlockSpec((tm, tn), lambda j, i: (i, j)),
        out_shape=jax.ShapeDtypeStruct((t, D_MODEL), BF16),
        compiler_params=_cparams(("parallel", "parallel"), 48),
        name="branch_merge",
    )(r_ctx, r_lat, a_ctx, a_lat, w_ret_o, w_mla_o, gates, gates)


def _outproj_kernel(m_ref, w_ref, xc_ref, xl_ref, g1_ref, nw_ref, sc_ref, sh_ref,
                    x1_ref, hhi_ref, hlo_ref, hp_ref, *, n_ctx_tok):
    mix = jnp.dot(m_ref[...], w_ref[...], preferred_element_type=F32)
    x = _path_pick(pl.program_id(0), x1_ref.shape[0], n_ctx_tok, xc_ref, xl_ref)
    x1 = x + g1_ref[...] * mix
    x1_ref[...] = x1
    h2 = _rms(x1, nw_ref[...]) * (1.0 + sc_ref[...]) + sh_ref[...]
    hi = h2.astype(BF16)
    hif = hi.astype(F32)
    hhi_ref[...] = hi
    hlo_ref[...] = (h2 - hif).astype(BF16)
    _store_row_tiles(hp_ref, hif)


def _out_proj(merged, w_out, x_ctx, x_lat, norm_w, mod4, n_ctx_tok, dec_seq):
    t = merged.shape[0]
    tm = TM_NORM
    row = lambda i: (i, 0)
    const = lambda i: (0, 0)
    return pl.pallas_call(
        functools.partial(_outproj_kernel, n_ctx_tok=n_ctx_tok),
        grid=(t // tm,),
        in_specs=[pl.BlockSpec((tm, D_MODEL), row),
                  pl.BlockSpec((D_MODEL, D_MODEL), const),
                  *_path_specs(tm, D_MODEL, n_ctx_tok),
                  _mod_spec(2, tm, n_ctx_tok, dec_seq),
                  pl.BlockSpec((1, D_MODEL), const),
                  _mod_spec(4, tm, n_ctx_tok, dec_seq),
                  _mod_spec(3, tm, n_ctx_tok, dec_seq)],
        out_specs=[pl.BlockSpec((tm, D_MODEL), row),
                   pl.BlockSpec((tm, D_MODEL), row),
                   pl.BlockSpec((tm, D_MODEL), row),
                   pl.BlockSpec((tm * ROW_TILES, LANES), row)],
        out_shape=[jax.ShapeDtypeStruct((t, D_MODEL), F32),
                   jax.ShapeDtypeStruct((t, D_MODEL), BF16),
                   jax.ShapeDtypeStruct((t, D_MODEL), BF16),
                   jax.ShapeDtypeStruct((t * ROW_TILES, LANES), U32)],
        compiler_params=_cparams(("parallel",), 48),
        name="out_proj_norm",
    )(merged, w_out, x_ctx, x_lat, mod4, norm_w, mod4, mod4)


def _router_kernel(hhi_ref, hlo_ref, wrt_ref, bias_ref, eidx_ref, wts_ref, rank_ref, cnt_ref, carry):
    tm = hhi_ref.shape[0]
    n_e = N_EXPERTS

    @pl.when(pl.program_id(0) == 0)
    def _():
        carry[...] = jnp.zeros_like(carry)

    wr = wrt_ref[...]
    wr_hi = wr.astype(BF16)
    wr_lo = (wr - wr_hi.astype(F32)).astype(BF16)
    hh = hhi_ref[...]
    hl = hlo_ref[...]
    nt_dims = (((1,), (1,)), ((), ()))
    logits = (lax.dot_general(wr_hi, hh, nt_dims, preferred_element_type=F32)
              + lax.dot_general(wr_hi, hl, nt_dims, preferred_element_type=F32)
              + lax.dot_general(wr_lo, hh, nt_dims, preferred_element_type=F32))
    scores = _sigmoid(logits)
    choice = scores + bias_ref[...]
    neg = -jnp.inf
    row = lax.broadcasted_iota(I32, (n_e, tm), 0).astype(F32)
    grow = lax.broadcasted_iota(I32, (GROUP_SIZE, tm), 0).astype(F32)

    gs = []
    for g in range(N_GROUPS):
        blk = choice[g * GROUP_SIZE:(g + 1) * GROUP_SIZE]
        m1 = jnp.max(blk, axis=0, keepdims=True)
        i1 = jnp.min(jnp.where(blk == m1, grow, float(GROUP_SIZE)), axis=0, keepdims=True)
        m2 = jnp.max(jnp.where(grow == i1, neg, blk), axis=0, keepdims=True)
        gs.append(m1 + m2)
    gsc = jnp.concatenate(gs, axis=0)
    gid = lax.broadcasted_iota(I32, (N_GROUPS, tm), 0).astype(F32)
    gsel = jnp.zeros((N_GROUPS, tm), F32)
    for _ in range(TOPK_GROUPS):
        m = jnp.max(gsc, axis=0, keepdims=True)
        idx = jnp.min(jnp.where(gsc == m, gid, float(N_GROUPS)), axis=0, keepdims=True)
        hit = gid == idx
        gsel = jnp.where(hit, 1.0, gsel)
        gsc = jnp.where(hit, neg, gsc)
    masked = jnp.concatenate(
        [jnp.where(gsel[g:g + 1] > 0.0, choice[g * GROUP_SIZE:(g + 1) * GROUP_SIZE], neg)
         for g in range(N_GROUPS)], axis=0)

    idxs, ws = [], []
    onehot = jnp.zeros((n_e, tm), F32)
    for _ in range(TOP_K):
        m = jnp.max(masked, axis=0, keepdims=True)
        idx = jnp.min(jnp.where(masked == m, row, float(n_e)), axis=0, keepdims=True)
        hit = row == idx
        ws.append(jnp.sum(jnp.where(hit, scores, 0.0), axis=0, keepdims=True))
        masked = jnp.where(hit, neg, masked)
        onehot = jnp.where(hit, 1.0, onehot)
        idxs.append(idx)
    w = jnp.concatenate(ws, axis=0)
    wts_ref[...] = w / jnp.sum(w, axis=0, keepdims=True) * ROUTED_SCALE
    eidx_ref[...] = jnp.concatenate(idxs, axis=0).astype(I32)

    t_src = lax.broadcasted_iota(I32, (tm, tm), 0)
    t_dst = lax.broadcasted_iota(I32, (tm, tm), 1)
    before = jnp.where(t_src < t_dst, 1.0, 0.0).astype(BF16)
    prior = jnp.dot(onehot.astype(BF16), before, preferred_element_type=F32) + carry[:, :1]
    rank_ref[...] = jnp.concatenate(
        [jnp.sum(jnp.where(row == idx, prior, 0.0), axis=0, keepdims=True) for idx in idxs],
        axis=0).astype(I32)
    carry[...] = carry[...] + jnp.sum(onehot, axis=1, keepdims=True)
    cnt_ref[...] = carry[...]


def _router(h_hi, h_lo, wr_t, bias_col):
    t = h_hi.shape[0]
    tm = 512
    tok = lambda i: (0, i)
    const = lambda i: (0, 0)
    return pl.pallas_call(
        _router_kernel,
        grid=(t // tm,),
        in_specs=[pl.BlockSpec((tm, D_MODEL), lambda i: (i, 0)),
                  pl.BlockSpec((tm, D_MODEL), lambda i: (i, 0)),
                  pl.BlockSpec((N_EXPERTS, D_MODEL), const),
                  pl.BlockSpec((N_EXPERTS, 1), const)],
        out_specs=[pl.BlockSpec((TOP_K, tm), tok),
                   pl.BlockSpec((TOP_K, tm), tok),
                   pl.BlockSpec((TOP_K, tm), tok),
                   pl.BlockSpec((N_EXPERTS, LANES), const)],
        out_shape=[jax.ShapeDtypeStruct((TOP_K, t), I32),
                   jax.ShapeDtypeStruct((TOP_K, t), F32),
                   jax.ShapeDtypeStruct((TOP_K, t), I32),
                   jax.ShapeDtypeStruct((N_EXPERTS, LANES), F32)],
        scratch_shapes=[pltpu.VMEM((N_EXPERTS, LANES), F32)],
        compiler_params=_cparams(("arbitrary",), 40),
        name="router_topk",
    )(h_hi, h_lo, wr_t, bias_col)


def _dest_kernel(eidx_ref, rank_ref, pstart_ref, dest_ref):
    tm = eidx_ref.shape[1]
    row = lax.broadcasted_iota(I32, (N_EXPERTS, tm), 0)
    pstart = pstart_ref[...]
    eidx = eidx_ref[...]
    base = jnp.concatenate(
        [jnp.sum(jnp.where(row == eidx[k:k + 1], pstart, 0.0), axis=0, keepdims=True)
         for k in range(TOP_K)], axis=0)
    dest_ref[...] = (base.astype(I32) + rank_ref[...]) * ROW_TILES


def _dest_rows(eidx, rank, pstart_col):
    t = eidx.shape[1]
    tm = 1024
    tok = lambda i: (0, i)
    return pl.pallas_call(
        _dest_kernel,
        grid=(t // tm,),
        in_specs=[pl.BlockSpec((TOP_K, tm), tok),
                  pl.BlockSpec((TOP_K, tm), tok),
                  pl.BlockSpec((N_EXPERTS, 1), lambda i: (0, 0))],
        out_specs=pl.BlockSpec((TOP_K, tm), tok),
        out_shape=jax.ShapeDtypeStruct((TOP_K, t), I32),
        compiler_params=_cparams(("parallel",)),
        name="dest_rows",
    )(eidx, rank, pstart_col)


def _dispatch_kernel(fill_ref, meta_ref, dest_ref, hp_ref, xs_ref, zbuf, sem, fill_sem):
    tm = hp_ref.shape[0]
    block_rows = zbuf.shape[0]
    n_blocks = xs_ref.shape[0] // block_rows

    def zero_fills(start):
        def run(cp):
            cp.start() if start else cp.wait()

        def per_expert(e, c):
            row = fill_ref[0, e]
            left = fill_ref[1, e]
            p = EXPERT_BM // 2
            while p >= 1:
                take = left & p

                @pl.when(take != 0)
                def _(row=row, p=p):
                    dst = xs_ref.at[pl.ds(pl.multiple_of(row * ROW_TILES, ROW_TILES), p * ROW_TILES)]
                    run(pltpu.make_async_copy(zbuf.at[pl.ds(0, p * ROW_TILES)], dst, fill_sem))

                row = row + take
                p //= 2
            return c

        def per_tail_block(b, c):
            dst = xs_ref.at[pl.ds(pl.multiple_of(b * block_rows, block_rows), block_rows)]
            run(pltpu.make_async_copy(zbuf, dst, fill_sem))
            return c

        lax.fori_loop(0, N_EXPERTS, per_expert, 0)
        lax.fori_loop(meta_ref[0], n_blocks, per_tail_block, 0)

    @pl.when(pl.program_id(0) == 0)
    def _():
        zbuf[...] = jnp.zeros_like(zbuf)
        zero_fills(True)

    def issue(t, c):
        src = hp_ref.at[pl.ds(pl.multiple_of(t * ROW_TILES, ROW_TILES), ROW_TILES)]
        for k in range(TOP_K):
            r = pl.multiple_of(dest_ref[t * TOP_K + k], ROW_TILES)
            pltpu.make_async_copy(src, xs_ref.at[pl.ds(r, ROW_TILES)], sem).start(priority=k % 2)
        return c

    lax.fori_loop(0, tm // ROW_TILES, issue, 0)
    for _ in range(TOP_K):
        pltpu.make_async_copy(hp_ref, xs_ref.at[pl.ds(0, tm)], sem).wait()

    @pl.when(pl.program_id(0) == 0)
    def _():
        zero_fills(False)


def _dispatch(fill, meta, dest, hp, n_blocks):
    tm = TM_NORM * ROW_TILES
    block_rows = EXPERT_BM * ROW_TILES
    return pl.pallas_call(
        _dispatch_kernel,
        grid=(hp.shape[0] // tm,),
        in_specs=[pl.BlockSpec(memory_space=pltpu.SMEM),
                  pl.BlockSpec(memory_space=pltpu.SMEM),
                  pl.BlockSpec((TM_NORM * TOP_K,), lambda i: (i,), memory_space=pltpu.SMEM),
                  pl.BlockSpec((tm, LANES), lambda i: (i, 0))],
        out_specs=pl.BlockSpec(memory_space=pl.ANY),
        out_shape=jax.ShapeDtypeStruct((n_blocks * block_rows, LANES), U32),
        scratch_shapes=[pltpu.VMEM((block_rows, LANES), U32),
                        pltpu.SemaphoreType.DMA(()),
                        pltpu.SemaphoreType.DMA(())],
        compiler_params=_cparams(("arbitrary",)),
        name="moe_dispatch",
    )(fill, meta, dest, hp)


N_WSLOTS = 2


def _expert_kernel(uidx_ref, ue_ref, meta_ref, xs_ref, wi_hbm, wo_hbm, ys_ref,
                   wi_st, wo_st, wi_s, wo_s, sem):
    b = pl.program_id(0)
    n_used = meta_ref[0]
    n_ue = meta_ref[1]

    def weight_copies(j, slot):
        e = ue_ref[j]
        copies = []
        for m, (hbm, st) in enumerate(((wi_hbm, wi_st), (wo_hbm, wo_st))):
            half = hbm.shape[1] // 2
            for q in range(2):
                rows = pl.ds(q * half, half)
                copies.append((pltpu.make_async_copy(hbm.at[e, rows], st.at[slot, rows], sem.at[2 * m + q, slot]), q))
        return copies

    @pl.when(b < n_used)
    def _():
        j = uidx_ref[b]

        @pl.when(b == 0)
        def _():
            for cp, prio in weight_copies(0, 0):
                cp.start(priority=prio)

            @pl.when(n_ue > 1)
            def _():
                for cp, prio in weight_copies(1, 1):
                    cp.start(priority=prio)

        @pl.when((b == 0) | (j != uidx_ref[jnp.maximum(b - 1, 0)]))
        def _():
            slot = j % N_WSLOTS
            for cp, _ in weight_copies(j, slot):
                cp.wait()
            rows = 256

            def cast_in(i, c):
                r = pl.multiple_of(i * rows, rows)
                wi_s[pl.ds(r, rows), :] = wi_st[slot, pl.ds(r, rows), :].astype(BF16)
                return c

            def cast_out(i, c):
                r = pl.multiple_of(i * rows, rows)
                wo_s[pl.ds(r, rows), :] = wo_st[slot, pl.ds(r, rows), :].astype(BF16)
                return c

            lax.fori_loop(0, D_MODEL // rows, cast_in, 0)
            lax.fori_loop(0, D_EXPERT // rows, cast_out, 0)

            @pl.when(j + N_WSLOTS < n_ue)
            def _():
                for cp, prio in weight_copies(j + N_WSLOTS, slot):
                    cp.start(priority=prio)

        bm = EXPERT_BM
        x = jnp.concatenate([p.astype(BF16) for p in _load_row_tiles(xs_ref, bm)], axis=1)
        gu = jnp.dot(x, wi_s[...], preferred_element_type=F32)
        act = (_silu(gu[:, :D_EXPERT]) * gu[:, D_EXPERT:]).astype(BF16)
        _store_row_tiles(ys_ref, jnp.dot(act, wo_s[...], preferred_element_type=F32))


def _experts(uidx, ue, meta, xs, w_exp_in, w_exp_out):
    rows = EXPERT_BM * ROW_TILES
    nb = xs.shape[0] // rows
    live = lambda b, uidx, ue, meta: (jnp.minimum(b, meta[0] - 1), 0)
    grid_spec = pltpu.PrefetchScalarGridSpec(
        num_scalar_prefetch=3,
        grid=(nb,),
        in_specs=[pl.BlockSpec((rows, LANES), live),
                  pl.BlockSpec(memory_space=pl.ANY),
                  pl.BlockSpec(memory_space=pl.ANY)],
        out_specs=pl.BlockSpec((rows, LANES), live),
        scratch_shapes=[pltpu.VMEM((N_WSLOTS, D_MODEL, 2 * D_EXPERT), F32),
                        pltpu.VMEM((N_WSLOTS, D_EXPERT, D_MODEL), F32),
                        pltpu.VMEM((D_MODEL, 2 * D_EXPERT), BF16),
                        pltpu.VMEM((D_EXPERT, D_MODEL), BF16),
                        pltpu.SemaphoreType.DMA((4, N_WSLOTS))],
    )
    return pl.pallas_call(
        _expert_kernel,
        grid_spec=grid_spec,
        out_shape=jax.ShapeDtypeStruct(xs.shape, U32),
        input_output_aliases={3: 0},
        compiler_params=_cparams(("arbitrary",), 52),
        name="routed_experts",
    )(uidx, ue, meta, xs, w_exp_in, w_exp_out)


def _combine_kernel(dest_ref, dnext_ref, wt_ref, x1_ref, hhi_ref, g2_ref, wsi_ref, wso_ref, nf_ref, ys_ref,
                    yc_ref, yl_ref, rows_a, rows_b, moe_scr, sem, *, n_ctx_tok):
    tm = x1_ref.shape[0]
    i = pl.program_id(0)
    n = pl.num_programs(0)

    def wait_tile(buf, s):
        for k in range(TOP_K):
            pltpu.make_async_copy(ys_ref.at[pl.ds(0, tm * ROW_TILES)], buf.at[k], sem.at[s]).wait()

    @pl.when(i == 0)
    def _():
        def issue(t, c):
            for k in range(TOP_K):
                r = pl.multiple_of(dest_ref[t * TOP_K + k], ROW_TILES)
                dst = rows_a.at[k, pl.ds(pl.multiple_of(t * ROW_TILES, ROW_TILES), ROW_TILES)]
                pltpu.make_async_copy(ys_ref.at[pl.ds(r, ROW_TILES)], dst, sem.at[0]).start(priority=k % 2)
            return c

        lax.fori_loop(0, tm, issue, 0)

    def step(cur, cur_sem, nxt, nxt_sem):
        wait_tile(cur, cur_sem)

        for t in range(tm):
            for k in range(TOP_K):
                r = pl.multiple_of(dnext_ref[t * TOP_K + k], ROW_TILES)
                dst = nxt.at[k, pl.ds(t * ROW_TILES, ROW_TILES)]
                pltpu.make_async_copy(ys_ref.at[pl.ds(r, ROW_TILES)], dst, sem.at[nxt_sem]).start(priority=k % 2)

        su = jnp.dot(hhi_ref[...], wsi_ref[...], preferred_element_type=F32)
        sact = (_silu(su[:, :D_SHARED]) * su[:, D_SHARED:]).astype(BF16)
        shared = jnp.dot(sact, wso_ref[...], preferred_element_type=F32)

        sub = 128
        for r0 in range(0, tm, sub):
            wks = [wt_ref[r0:r0 + sub, k:k + 1] for k in range(TOP_K)]
            for s in range(ROW_TILES):
                acc_lo = acc_hi = None
                for k in range(TOP_K):
                    lo, hi = _unpack_bf16_pair(cur[k, pl.ds(r0 * ROW_TILES + s, sub, SUBLANES), :])
                    acc_lo = wks[k] * lo if acc_lo is None else acc_lo + wks[k] * lo
                    acc_hi = wks[k] * hi if acc_hi is None else acc_hi + wks[k] * hi
                col = 2 * LANES * s
                moe_scr[r0:r0 + sub, col:col + LANES] = acc_lo
                moe_scr[r0:r0 + sub, col + LANES:col + 2 * LANES] = acc_hi
        moe = moe_scr[...] + shared
        x2 = x1_ref[...] + g2_ref[...] * moe
        y = _rms(x2, nf_ref[...])
        is_ctx = i * tm < n_ctx_tok

        @pl.when(is_ctx)
        def _():
            yc_ref[...] = y

        @pl.when(jnp.logical_not(is_ctx))
        def _():
            yl_ref[...] = y

        @pl.when(i == n - 1)
        def _():
            wait_tile(nxt, nxt_sem)

    @pl.when(i % 2 == 0)
    def _():
        step(rows_a, 0, rows_b, 1)

    @pl.when(i % 2 == 1)
    def _():
        step(rows_b, 1, rows_a, 0)


def _combine(dest, wts_t, x1, h_hi, mod4, w_sh_in, w_sh_out, norm_final, ys, n_ctx_tok, dec_seq):
    t = x1.shape[0]
    tm = TM_NORM
    n = t // tm
    row = lambda i: (i, 0)
    const = lambda i: (0, 0)
    return pl.pallas_call(
        functools.partial(_combine_kernel, n_ctx_tok=n_ctx_tok),
        grid=(n,),
        in_specs=[pl.BlockSpec((tm * TOP_K,), lambda i: (i,), memory_space=pltpu.SMEM),
                  pl.BlockSpec((tm * TOP_K,), lambda i: (jnp.minimum(i + 1, n - 1),), memory_space=pltpu.SMEM),
                  pl.BlockSpec((tm, TOP_K), row),
                  pl.BlockSpec((tm, D_MODEL), row),
                  pl.BlockSpec((tm, D_MODEL), row),
                  _mod_spec(5, tm, n_ctx_tok, dec_seq),
                  pl.BlockSpec((D_MODEL, 2 * D_SHARED), const),
                  pl.BlockSpec((D_SHARED, D_MODEL), const),
                  pl.BlockSpec((1, D_MODEL), const),
                  pl.BlockSpec(memory_space=pl.ANY)],
        out_specs=list(_path_specs(tm, D_MODEL, n_ctx_tok)),
        out_shape=[jax.ShapeDtypeStruct((n_ctx_tok, D_MODEL), F32),
                   jax.ShapeDtypeStruct((t - n_ctx_tok, D_MODEL), F32)],
        scratch_shapes=[pltpu.VMEM((TOP_K, tm * ROW_TILES, LANES), U32),
                        pltpu.VMEM((TOP_K, tm * ROW_TILES, LANES), U32),
                        pltpu.VMEM((tm, D_MODEL), F32),
                        pltpu.SemaphoreType.DMA((2,))],
        compiler_params=_cparams(("arbitrary",), 56),
        name="moe_combine",
    )(dest, dest, wts_t, x1, h_hi, mod4, w_sh_in, w_sh_out, norm_final, ys)


def _rope_tables(tm, dec_seq):
    rows = dec_seq // GRID_W
    row = jnp.broadcast_to(jnp.arange(rows, dtype=F32)[:, None], (rows, GRID_W)).reshape(-1)
    col = jnp.broadcast_to(jnp.arange(GRID_W, dtype=F32)[None, :], (rows, GRID_W)).reshape(-1)
    axis_dim = MLA_ROPE // 2
    freqs = jnp.power(ROPE_BASE, -jnp.arange(0, axis_dim, 2, dtype=F32) / axis_dim)
    ang_r = row[:, None] * freqs
    ang_c = col[:, None] * freqs
    ang = jnp.concatenate([ang_r, ang_r, ang_c, ang_c], axis=-1)
    pad = LANES - MLA_ROPE
    cos = jnp.concatenate([jnp.cos(ang), jnp.ones((dec_seq, pad), F32)], axis=1)
    sin = jnp.concatenate([jnp.sin(ang), jnp.zeros((dec_seq, pad), F32)], axis=1)
    cos = jnp.concatenate([jnp.ones((tm, LANES), F32), cos], axis=0)
    sin = jnp.concatenate([jnp.zeros((tm, LANES), F32), sin], axis=0)
    return cos, sin


def _block_plan(counts, n_blocks):
    bm = EXPERT_BM
    nblk = (counts + bm - 1) // bm
    bend = jnp.cumsum(nblk)
    bstart = bend - nblk
    present = nblk > 0
    pcum = jnp.cumsum(present.astype(I32))
    bidx = jnp.arange(n_blocks, dtype=I32)
    uidx = jnp.sum((present[None, :] & (bstart[None, :] <= bidx[:, None])).astype(I32), axis=1) - 1
    order = jnp.arange(N_EXPERTS, dtype=I32)
    ue = jnp.minimum(jnp.sum((pcum[None, :] <= order[:, None]).astype(I32), axis=1), N_EXPERTS - 1)
    meta = jnp.stack([bend[-1], pcum[-1]]).astype(I32)
    pstart = bstart * bm
    fill = jnp.stack([pstart + counts, nblk * bm - counts]).astype(I32)
    return pstart, uidx.astype(I32), ue.astype(I32), meta, fill


def kernel(x_prompt, x_sample, c, cache_mla_ckv, cache_mla_krope, state_ret, c_ctx, w_mod, b_mod, norm_mix, norm_ffn, w_in, ret_decay_logit, ret_gn_w, w_ret_o, mla_q_norm, w_q_b, mla_kv_norm, w_kv_b, w_mla_o, w_out, w_router, router_bias, w_exp_in, w_exp_out, w_shared_in, w_shared_out, norm_final):
    n_ctx, seq, d = x_prompt.shape
    n_lat, dec_seq, _ = x_sample.shape
    past = cache_mla_ckv.shape[2]
    n_ctx_tok = n_ctx * seq
    n_lat_tok = n_lat * dec_seq
    t = n_ctx_tok + n_lat_tok
    assert d == D_MODEL and w_mod.shape[0] == 1 and n_lat + 1 <= MOD_ROWS
    assert seq % TM_NORM == 0 and dec_seq % TM_NORM == 0 and n_ctx_tok % dec_seq == 0

    x_ctx = x_prompt.reshape(n_ctx_tok, d)
    x_lat = x_sample.reshape(n_lat_tok, d)

    cond = jnp.concatenate([c_ctx[None, :], c, jnp.zeros((MOD_ROWS - 1 - n_lat, d), F32)], axis=0)
    mod = _modulation(cond, w_mod.reshape(d, N_MOD * d), b_mod.reshape(1, N_MOD * d))
    mod4 = mod.reshape(MOD_ROWS, N_MOD, 1, d)

    w_in2 = w_in.reshape(d, -1)
    o_g = 2 * RET_QK_W + RET_V_W
    o_qa = o_g + RET_V_W
    o_kpe = o_qa + MLA_Q_RANK + MLA_KV_RANK
    o_gr = o_kpe + MLA_ROPE
    w_qkvg = w_in2[:, :o_qa].astype(BF16)
    w_low = jnp.pad(w_in2[:, o_qa:o_gr], ((0, 0), (0, LANES - MLA_ROPE))).astype(BF16)
    w_gates = w_in2[:, o_gr:].astype(BF16)
    wq3 = w_q_b.reshape(MLA_Q_RANK, MLA_HEADS, MLA_NOPE + MLA_ROPE)
    w_qb = jnp.pad(wq3, ((0, 0), (0, 0), (0, Q_HEAD_W - MLA_NOPE - MLA_ROPE))
                   ).reshape(MLA_Q_RANK, MLA_HEADS * Q_HEAD_W).astype(BF16)
    w_kvb = w_kv_b.reshape(MLA_KV_RANK, -1).astype(BF16)
    cos_t, sin_t = _rope_tables(TM_NORM, dec_seq)

    h1 = _norm_modulate(x_ctx, x_lat, norm_mix.reshape(1, d), mod4, n_ctx_tok, dec_seq)
    qkvg = _matmul(h1, w_qkvg, tm=2048, tn=1024, name="proj_qkvg")
    gates = _matmul(h1, w_gates, tm=2048, tn=1024, act="sigmoid", name="proj_gates")
    qn, ckv, ckv_b, kpe_raw, kpe_rot = _lowrank(
        h1, w_low, mla_q_norm.reshape(1, -1), mla_kv_norm.reshape(1, -1), cos_t, sin_t, n_ctx_tok, dec_seq)

    logits = ret_decay_logit.reshape(2, RET_HEADS)
    gn_w = ret_gn_w.reshape(1, RET_V_W)
    ctx_group = dec_seq // seq
    r_ctx, ret_state = _retention(qkvg, logits, gn_w, None, n_seq=n_ctx, seq=seq, group=ctx_group,
                                  row_block0=0, emit_state=True)
    (r_lat,) = _retention(qkvg, logits, gn_w, state_ret.reshape(n_lat, 2, RET_HEADS, RET_DK, RET_DV),
                          n_seq=n_lat, seq=dec_seq, group=1, row_block0=n_ctx_tok // dec_seq,
                          emit_state=False)

    q_all = _q_up(qn, w_qb, cos_t, sin_t, n_ctx_tok, dec_seq)
    kv_all = _matmul(ckv_b, w_kvb, tm=1024, tn=1024, name="kv_up")
    kv_ctx = _matmul(cache_mla_ckv.reshape(n_lat * past, MLA_KV_RANK), w_kvb, tm=1024, tn=1024,
                     name="kv_up_cache")
    kpe_ctx = jnp.pad(cache_mla_krope.reshape(n_lat * past, MLA_ROPE),
                      ((0, 0), (0, LANES - MLA_ROPE))).astype(BF16)
    a_ctx = _attention_ctx(q_all, kv_all, kpe_rot, n_ctx, seq)
    a_lat = _attention_lat(q_all, kv_ctx, kv_all, kpe_ctx, kpe_rot, n_lat, dec_seq, past, n_ctx_tok)

    merged = _merge(r_ctx, r_lat, a_ctx, a_lat, w_ret_o.reshape(RET_V_W, d).astype(BF16),
                    w_mla_o.reshape(MLA_HEADS * MLA_V, d).astype(BF16), gates, n_ctx_tok)
    x1, h_hi, h_lo, hp = _out_proj(merged, w_out.reshape(d, d).astype(BF16), x_ctx, x_lat,
                                   norm_ffn.reshape(1, d), mod4, n_ctx_tok, dec_seq)

    eidx, wts, rank, cnt = _router(h_hi, h_lo, w_router.reshape(d, N_EXPERTS).T,
                                   router_bias.reshape(N_EXPERTS, 1))
    n_blocks = t * TOP_K // EXPERT_BM + N_EXPERTS
    pstart, uidx, ue, meta, fill = _block_plan(cnt[:, 0].astype(I32), n_blocks)
    dest = _dest_rows(eidx, rank, pstart.astype(F32).reshape(N_EXPERTS, 1)).T.reshape(-1)
    xs = _dispatch(fill, meta, dest, hp, n_blocks)
    ys = _experts(uidx, ue, meta, xs, w_exp_in.reshape(N_EXPERTS, d, 2 * D_EXPERT),
                  w_exp_out.reshape(N_EXPERTS, D_EXPERT, d))
    y_ctx, y_lat = _combine(dest, wts.T, x1, h_hi, mod4, w_shared_in.reshape(d, 2 * D_SHARED).astype(BF16),
                            w_shared_out.reshape(D_SHARED, d).astype(BF16), norm_final.reshape(1, d), ys,
                            n_ctx_tok, dec_seq)

    y_prompt = y_ctx.reshape(n_ctx, seq, d)
    y_sample = y_lat.reshape(n_lat, dec_seq, d)
    new_ckv = ckv[:n_ctx_tok].reshape(n_ctx, 1, seq, MLA_KV_RANK)
    new_krope = kpe_raw[:n_ctx_tok, :MLA_ROPE].reshape(n_ctx, 1, seq, MLA_ROPE)
    new_state = ret_state.reshape(n_ctx, 1, 2, RET_HEADS, RET_DK, RET_DV)
    return (y_prompt, y_sample, new_ckv, new_krope, new_state)
```

```python
import functools

import jax
import jax.numpy as jnp
from jax import lax
from jax.experimental import pallas as pl
from jax.experimental.pallas import tpu as pltpu

F32 = jnp.float32
BF16 = jnp.bfloat16
U32 = jnp.uint32
I32 = jnp.int32

D_MODEL = 2048
GRID_W = 64
NORM_EPS = 1e-6
RET_HEADS = 8
RET_DK = 128
RET_DV = 256
RET_CHUNK = 128
RET_QK_W = RET_HEADS * RET_DK
RET_V_W = RET_HEADS * RET_DV
MLA_HEADS = 16
MLA_Q_RANK = 512
MLA_KV_RANK = 512
MLA_NOPE = 128
MLA_ROPE = 64
MLA_V = 128
ROPE_BASE = 10000.0
N_EXPERTS = 256
TOP_K = 8
N_GROUPS = 8
TOPK_GROUPS = 4
GROUP_SIZE = N_EXPERTS // N_GROUPS
D_EXPERT = 512
D_SHARED = 512
ROUTED_SCALE = 2.5

LANES = 128
SUBLANES = 8
ROW_TILES = D_MODEL // (2 * LANES)
assert ROW_TILES == SUBLANES
Q_HEAD_W = 256
N_MOD = 6
MOD_ROWS = 16
TM_NORM = 256
EXPERT_BM = 256
MIB = 1 << 20


def _cparams(sem, vmem_mib=None):
    kw = dict(dimension_semantics=sem)
    if vmem_mib is not None:
        kw["vmem_limit_bytes"] = vmem_mib * MIB
    return pltpu.CompilerParams(**kw)


def _sigmoid(x):
    return 1.0 / (1.0 + jnp.exp(-x))


def _silu(x):
    return x * _sigmoid(x)


def _rms(x, w):
    ms = jnp.mean(x * x, axis=-1, keepdims=True)
    return x * lax.rsqrt(ms + NORM_EPS) * w


def _pack_bf16_pair(lo, hi):
    return pltpu.pack_elementwise([lo, hi], packed_dtype=BF16)


def _unpack_bf16_pair(w):
    lo = pltpu.unpack_elementwise(w, index=0, packed_dtype=BF16, unpacked_dtype=F32)
    hi = pltpu.unpack_elementwise(w, index=1, packed_dtype=BF16, unpacked_dtype=F32)
    return lo, hi


def _store_row_tiles(ref, x):
    m = x.shape[0]
    for s in range(ROW_TILES):
        lo = x[:, 2 * LANES * s:2 * LANES * s + LANES]
        hi = x[:, 2 * LANES * s + LANES:2 * LANES * (s + 1)]
        ref[pl.ds(s, m, SUBLANES), :] = _pack_bf16_pair(lo, hi)


def _load_row_tiles(ref, m):
    parts = []
    for s in range(ROW_TILES):
        parts.extend(_unpack_bf16_pair(ref[pl.ds(s, m, SUBLANES), :]))
    return parts


def _rope128(x, cos, sin):
    lane = lax.broadcasted_iota(I32, x.shape, 1)
    first = (lane & 31) < 16
    rot = jnp.where(first, -pltpu.roll(x, LANES - 16, 1), pltpu.roll(x, 16, 1))
    return x * cos + rot * sin


def _mod_kernel(c_ref, w_ref, b_ref, o_ref):
    s = _silu(c_ref[...]).astype(BF16)
    o_ref[...] = jnp.dot(s, w_ref[...].astype(BF16), preferred_element_type=F32) + b_ref[...]


def _modulation(cond, w_mod, b_mod):
    n = w_mod.shape[1]
    tn = 1024
    return pl.pallas_call(
        _mod_kernel,
        grid=(n // tn,),
        in_specs=[pl.BlockSpec((MOD_ROWS, D_MODEL), lambda j: (0, 0)),
                  pl.BlockSpec((D_MODEL, tn), lambda j: (0, j)),
                  pl.BlockSpec((1, tn), lambda j: (0, j))],
        out_specs=pl.BlockSpec((MOD_ROWS, tn), lambda j: (0, j)),
        out_shape=jax.ShapeDtypeStruct((MOD_ROWS, n), F32),
        compiler_params=_cparams(("arbitrary",), 40),
        name="modulation",
    )(cond, w_mod, b_mod)


def _mod_spec(which, tm, n_ctx_tok, dec_seq):
    def index(i, *_):
        tok = i * tm
        row = jnp.where(tok < n_ctx_tok, 0, 1 + (tok - n_ctx_tok) // dec_seq)
        return (row, which, 0, 0)
    return pl.BlockSpec((None, None, 1, D_MODEL), index)


def _path_specs(tm, width, n_ctx_tok, axis=0):
    n_ctx_tiles = n_ctx_tok // tm

    def ctx(*idx):
        return (jnp.minimum(idx[axis], n_ctx_tiles - 1), 0)

    def lat(*idx):
        return (jnp.maximum(idx[axis] - n_ctx_tiles, 0), 0)

    return pl.BlockSpec((tm, width), ctx), pl.BlockSpec((tm, width), lat)


def _path_pick(tile, tm, n_ctx_tok, ctx_ref, lat_ref):
    return jnp.where(tile * tm < n_ctx_tok, ctx_ref[...], lat_ref[...])


def _normmod_kernel(xc_ref, xl_ref, w_ref, sc_ref, sh_ref, o_ref, *, n_ctx_tok):
    x = _path_pick(pl.program_id(0), o_ref.shape[0], n_ctx_tok, xc_ref, xl_ref)
    y = _rms(x, w_ref[...])
    o_ref[...] = (y * (1.0 + sc_ref[...]) + sh_ref[...]).astype(BF16)


def _norm_modulate(x_ctx, x_lat, w, mod4, n_ctx_tok, dec_seq):
    t = x_ctx.shape[0] + x_lat.shape[0]
    tm = TM_NORM
    return pl.pallas_call(
        functools.partial(_normmod_kernel, n_ctx_tok=n_ctx_tok),
        grid=(t // tm,),
        in_specs=[*_path_specs(tm, D_MODEL, n_ctx_tok),
                  pl.BlockSpec((1, D_MODEL), lambda i: (0, 0)),
                  _mod_spec(1, tm, n_ctx_tok, dec_seq),
                  _mod_spec(0, tm, n_ctx_tok, dec_seq)],
        out_specs=pl.BlockSpec((tm, D_MODEL), lambda i: (i, 0)),
        out_shape=jax.ShapeDtypeStruct((t, D_MODEL), BF16),
        compiler_params=_cparams(("parallel",)),
        name="norm_modulate",
    )(x_ctx, x_lat, w, mod4, mod4)


def _mm_kernel(a_ref, w_ref, o_ref, *, act):
    acc = jnp.dot(a_ref[...].astype(BF16), w_ref[...], preferred_element_type=F32)
    if act == "sigmoid":
        acc = _sigmoid(acc)
    o_ref[...] = acc.astype(o_ref.dtype)


def _matmul(a, w, *, tm, tn, act=None, name):
    m, k = a.shape
    n = w.shape[1]
    tm = min(tm, m)
    assert m % tm == 0 and n % tn == 0
    return pl.pallas_call(
        functools.partial(_mm_kernel, act=act),
        grid=(n // tn, m // tm),
        in_specs=[pl.BlockSpec((tm, k), lambda j, i: (i, 0)),
                  pl.BlockSpec((k, tn), lambda j, i: (0, j))],
        out_specs=pl.BlockSpec((tm, tn), lambda j, i: (i, j)),
        out_shape=jax.ShapeDtypeStruct((m, n), BF16),
        compiler_params=_cparams(("parallel", "parallel"), 48),
        name=name,
    )(a, w)


def _lowrank_kernel(h_ref, w_ref, qnw_ref, kvnw_ref, cos_ref, sin_ref,
                    qn_ref, ckv_ref, ckvb_ref, kpe_ref, kper_ref):
    acc = jnp.dot(h_ref[...], w_ref[...], preferred_element_type=F32)
    qn_ref[...] = _rms(acc[:, :MLA_Q_RANK], qnw_ref[...]).astype(BF16)
    ckv = _rms(acc[:, MLA_Q_RANK:MLA_Q_RANK + MLA_KV_RANK], kvnw_ref[...])
    ckv_ref[...] = ckv
    ckvb_ref[...] = ckv.astype(BF16)
    kp = acc[:, MLA_Q_RANK + MLA_KV_RANK:]
    kpe_ref[...] = kp
    kper_ref[...] = _rope128(kp, cos_ref[...], sin_ref[...]).astype(BF16)


def _rope_spec(tm, n_ctx_tok, dec_seq):
    per_seq = dec_seq // tm

    def index(i, *_):
        tok = i * tm
        return (jnp.where(tok < n_ctx_tok, 0, 1 + ((tok - n_ctx_tok) // tm) % per_seq), 0)
    return pl.BlockSpec((tm, LANES), index)


def _lowrank(h, w_low, qnw, kvnw, cos_t, sin_t, n_ctx_tok, dec_seq):
    t = h.shape[0]
    tm = TM_NORM
    n = w_low.shape[1]
    row = lambda i: (i, 0)
    const = lambda i: (0, 0)
    return pl.pallas_call(
        _lowrank_kernel,
        grid=(t // tm,),
        in_specs=[pl.BlockSpec((tm, D_MODEL), row),
                  pl.BlockSpec((D_MODEL, n), const),
                  pl.BlockSpec((1, MLA_Q_RANK), const),
                  pl.BlockSpec((1, MLA_KV_RANK), const),
                  _rope_spec(tm, n_ctx_tok, dec_seq),
                  _rope_spec(tm, n_ctx_tok, dec_seq)],
        out_specs=[pl.BlockSpec((tm, MLA_Q_RANK), row),
                   pl.BlockSpec((tm, MLA_KV_RANK), row),
                   pl.BlockSpec((tm, MLA_KV_RANK), row),
                   pl.BlockSpec((tm, LANES), row),
                   pl.BlockSpec((tm, LANES), row)],
        out_shape=[jax.ShapeDtypeStruct((t, MLA_Q_RANK), BF16),
                   jax.ShapeDtypeStruct((t, MLA_KV_RANK), F32),
                   jax.ShapeDtypeStruct((t, MLA_KV_RANK), BF16),
                   jax.ShapeDtypeStruct((t, LANES), F32),
                   jax.ShapeDtypeStruct((t, LANES), BF16)],
        compiler_params=_cparams(("parallel",), 40),
        name="lowrank_proj",
    )(h, w_low, qnw, kvnw, cos_t, sin_t)


Q_PRESCALE = (MLA_NOPE + MLA_ROPE) ** -0.5 * 1.4426950408889634


def _qb_kernel(qn_ref, w_ref, cos_ref, sin_ref, o_ref):
    qn = qn_ref[...]
    cos = cos_ref[...]
    sin = sin_ref[...]
    for h in range(MLA_HEADS):
        lo = h * Q_HEAD_W
        acc = jnp.dot(qn, w_ref[:, lo:lo + Q_HEAD_W], preferred_element_type=F32) * Q_PRESCALE
        o_ref[:, lo:lo + MLA_NOPE] = acc[:, :MLA_NOPE].astype(BF16)
        o_ref[:, lo + MLA_NOPE:lo + Q_HEAD_W] = _rope128(acc[:, MLA_NOPE:], cos, sin).astype(BF16)


def _q_up(qn, w_qb, cos_t, sin_t, n_ctx_tok, dec_seq):
    t = qn.shape[0]
    tm = TM_NORM
    n = w_qb.shape[1]
    return pl.pallas_call(
        _qb_kernel,
        grid=(t // tm,),
        in_specs=[pl.BlockSpec((tm, MLA_Q_RANK), lambda i: (i, 0)),
                  pl.BlockSpec((MLA_Q_RANK, n), lambda i: (0, 0)),
                  _rope_spec(tm, n_ctx_tok, dec_seq),
                  _rope_spec(tm, n_ctx_tok, dec_seq)],
        out_specs=pl.BlockSpec((tm, n), lambda i: (i, 0)),
        out_shape=jax.ShapeDtypeStruct((t, n), BF16),
        compiler_params=_cparams(("parallel",), 40),
        name="q_up_rope",
    )(qn, w_qb, cos_t, sin_t)


def _ret_kernel(logit_ref, q_ref, k_ref, v_ref, g_ref, gnw_ref, *rest, n_chunks, group, has_s0, emit_state):
    rest = list(rest)
    s0_ref = rest.pop(0) if has_s0 else None
    o_ref = rest.pop(0)
    sfin_ref = rest.pop(0) if emit_state else None
    sf_scr, sb_scr = rest
    c_len = RET_CHUNK
    head = pl.program_id(1)

    def log_gamma(direction):
        l = jnp.full((c_len, 1), logit_ref[direction, head], F32)
        return jnp.minimum(l, 0.0) - jnp.log1p(jnp.exp(-jnp.abs(l)))

    lgf_c = log_gamma(0)
    lgb_c = log_gamma(1)
    ii = lax.broadcasted_iota(I32, (c_len, c_len), 0).astype(F32)
    jj = lax.broadcasted_iota(I32, (c_len, c_len), 1).astype(F32)
    k_scale = RET_DK ** -0.5
    decay = (jnp.where(ii >= jj, jnp.exp(jnp.maximum(ii - jj, 0.0) * lgf_c), 0.0)
             + jnp.where(jj >= ii, jnp.exp(jnp.maximum(jj - ii, 0.0) * lgb_c), 0.0)) * k_scale
    pos = lax.broadcasted_iota(I32, (c_len, 1), 0).astype(F32)
    qdec_f = jnp.exp((pos + 1.0) * lgf_c)
    qdec_b = jnp.exp((c_len - pos) * lgb_c)
    kdec_f = jnp.exp((c_len - 1.0 - pos) * lgf_c) * k_scale
    kdec_b = jnp.exp(pos * lgb_c) * k_scale
    cdec_f = jnp.exp(c_len * lgf_c)
    cdec_b = jnp.exp(c_len * lgb_c)
    tn_dims = (((0,), (0,)), ((), ()))
    nt_dims = (((1,), (1,)), ((), ()))

    def chunk(c):
        return pl.ds(c * c_len, c_len)

    for c in range(n_chunks):
        kc = k_ref[chunk(c), :].astype(F32)
        vc = v_ref[chunk(c), :]
        sf_scr[c] = lax.dot_general((kc * kdec_f).astype(BF16), vc, tn_dims, preferred_element_type=F32)
        sb_scr[c] = lax.dot_general((kc * kdec_b).astype(BF16), vc, tn_dims, preferred_element_type=F32)

    per_seq = n_chunks // group
    for g in range(group):
        chunks = range(g * per_seq, (g + 1) * per_seq)
        s = s0_ref[g, 0] if has_s0 else jnp.zeros((RET_DK, RET_DV), F32)
        for c in chunks:
            kv = sf_scr[c]
            sf_scr[c] = s
            s = cdec_f * s + kv
        if emit_state:
            sfin_ref[g, 0] = s
        s = s0_ref[g, 1] if has_s0 else jnp.zeros((RET_DK, RET_DV), F32)
        for c in reversed(chunks):
            kv = sb_scr[c]
            sb_scr[c] = s
            s = cdec_b * s + kv
        if emit_state:
            sfin_ref[g, 1] = s

    gnw = gnw_ref[...]
    for c in range(n_chunks):
        qc = q_ref[chunk(c), :]
        kc = k_ref[chunk(c), :]
        vc = v_ref[chunk(c), :]
        sc = lax.dot_general(qc, kc, nt_dims, preferred_element_type=F32) * decay
        qf = qc.astype(F32)
        qcat = jnp.concatenate([(qf * qdec_f).astype(BF16), (qf * qdec_b).astype(BF16)], axis=1)
        scat = jnp.concatenate([sf_scr[c], sb_scr[c]], axis=0).astype(BF16)
        o = (jnp.dot(sc.astype(BF16), vc, preferred_element_type=F32)
             + jnp.dot(qcat, scat, preferred_element_type=F32))
        mu = jnp.mean(o, axis=-1, keepdims=True)
        var = jnp.mean(jnp.square(o - mu), axis=-1, keepdims=True)
        on = (o - mu) * lax.rsqrt(var + NORM_EPS) * gnw
        o_ref[chunk(c), :] = (_silu(g_ref[chunk(c), :].astype(F32)) * on).astype(BF16)


def _retention(qkvg, logits, gn_w, s0, *, n_seq, seq, group, row_block0, emit_state):
    rows = group * seq
    n_chunks = rows // RET_CHUNK
    has_s0 = s0 is not None
    assert n_seq % group == 0
    k_off = RET_QK_W // RET_DK
    v_off = 2 * RET_QK_W // RET_DV
    g_off = (2 * RET_QK_W + RET_V_W) // RET_DV
    in_specs = [pl.BlockSpec(memory_space=pltpu.SMEM),
                pl.BlockSpec((rows, RET_DK), lambda b, h: (row_block0 + b, h)),
                pl.BlockSpec((rows, RET_DK), lambda b, h: (row_block0 + b, k_off + h)),
                pl.BlockSpec((rows, RET_DV), lambda b, h: (row_block0 + b, v_off + h)),
                pl.BlockSpec((rows, RET_DV), lambda b, h: (row_block0 + b, g_off + h)),
                pl.BlockSpec((1, RET_DV), lambda b, h: (0, h))]
    args = [logits, qkvg, qkvg, qkvg, qkvg, gn_w]
    state_spec = pl.BlockSpec((group, 2, None, RET_DK, RET_DV), lambda b, h: (b, 0, h, 0, 0))
    if has_s0:
        in_specs.append(state_spec)
        args.append(s0)
    out_specs = [pl.BlockSpec((rows, RET_DV), lambda b, h: (b, h))]
    out_shape = [jax.ShapeDtypeStruct((n_seq * seq, RET_V_W), BF16)]
    if emit_state:
        out_specs.append(state_spec)
        out_shape.append(jax.ShapeDtypeStruct((n_seq, 2, RET_HEADS, RET_DK, RET_DV), F32))
    res = pl.pallas_call(
        functools.partial(_ret_kernel, n_chunks=n_chunks, group=group, has_s0=has_s0, emit_state=emit_state),
        grid=(n_seq // group, RET_HEADS),
        in_specs=in_specs,
        out_specs=out_specs,
        out_shape=out_shape,
        scratch_shapes=[pltpu.VMEM((n_chunks, RET_DK, RET_DV), F32),
                        pltpu.VMEM((n_chunks, RET_DK, RET_DV), F32)],
        compiler_params=_cparams(("parallel", "parallel")),
        name="retention_seq%d" % seq,
    )(*args)
    return res


V_EXT_W = MLA_V + LANES


def _ones_column(rows):
    lane = lax.broadcasted_iota(I32, (rows, LANES), 1)
    return jnp.where(lane == 0, 1.0, 0.0).astype(BF16)


def _softmax_pv(q, kcat, v_ext):
    s = lax.dot_general(q, kcat, (((1,), (1,)), ((), ())), preferred_element_type=F32)
    e = jnp.exp2(s - jnp.max(s, axis=-1, keepdims=True))
    o = jnp.dot(e.astype(BF16), v_ext, preferred_element_type=F32)
    return o[:, :MLA_V] / o[:, MLA_V:MLA_V + 1]


def _attn_ctx_kernel(q_ref, kv_ref, kpe_ref, o_ref):
    kpe = kpe_ref[...]
    ones = _ones_column(kpe.shape[0])
    kvw = MLA_NOPE + MLA_V
    for h in range(MLA_HEADS):
        kv = kv_ref[:, h * kvw:(h + 1) * kvw]
        kcat = jnp.concatenate([kv[:, :MLA_NOPE], kpe], axis=1)
        v_ext = jnp.concatenate([kv[:, MLA_NOPE:], ones], axis=1)
        o = _softmax_pv(q_ref[:, h * Q_HEAD_W:(h + 1) * Q_HEAD_W], kcat, v_ext)
        o_ref[:, h * MLA_V:(h + 1) * MLA_V] = o.astype(o_ref.dtype)


def _attention_ctx(q_all, kv_all, kpe_all, n_seq, seq):
    return pl.pallas_call(
        _attn_ctx_kernel,
        grid=(n_seq,),
        in_specs=[pl.BlockSpec((seq, MLA_HEADS * Q_HEAD_W), lambda b: (b, 0)),
                  pl.BlockSpec((seq, MLA_HEADS * (MLA_NOPE + MLA_V)), lambda b: (b, 0)),
                  pl.BlockSpec((seq, LANES), lambda b: (b, 0))],
        out_specs=pl.BlockSpec((seq, MLA_HEADS * MLA_V), lambda b: (b, 0)),
        out_shape=jax.ShapeDtypeStruct((n_seq * seq, MLA_HEADS * MLA_V), BF16),
        compiler_params=_cparams(("parallel",), 40),
        name="attention_ctx",
    )(q_all, kv_all, kpe_all)


def _attn_lat_kernel(q_ref, kvc_ref, kvl_ref, kpc_ref, kpl_ref, o_ref, kcat, vcat, *, past, seq):
    @pl.when(pl.program_id(2) == 0)
    def _():
        kcat[0:past, 0:MLA_NOPE] = kvc_ref[:, :MLA_NOPE]
        kcat[0:past, MLA_NOPE:] = kpc_ref[...]
        kcat[past:past + seq, 0:MLA_NOPE] = kvl_ref[:, :MLA_NOPE]
        kcat[past:past + seq, MLA_NOPE:] = kpl_ref[...]
        vcat[0:past, 0:MLA_V] = kvc_ref[:, MLA_NOPE:]
        vcat[past:past + seq, 0:MLA_V] = kvl_ref[:, MLA_NOPE:]
        vcat[:, MLA_V:] = _ones_column(past + seq)

    sub = 256
    for r in range(q_ref.shape[0] // sub):
        rows = slice(r * sub, (r + 1) * sub)
        o_ref[rows, :] = _softmax_pv(q_ref[rows, :], kcat[...], vcat[...]).astype(o_ref.dtype)


def _attention_lat(q_all, kv_ctx, kv_all, kpe_ctx, kpe_all, n_seq, seq, past, n_ctx_tok):
    tq = 1024
    nq = seq // tq
    q0 = n_ctx_tok // tq
    s0 = n_ctx_tok // seq
    kvw = MLA_NOPE + MLA_V
    return pl.pallas_call(
        functools.partial(_attn_lat_kernel, past=past, seq=seq),
        grid=(n_seq, MLA_HEADS, nq),
        in_specs=[pl.BlockSpec((tq, Q_HEAD_W), lambda b, h, i: (q0 + b * nq + i, h)),
                  pl.BlockSpec((past, kvw), lambda b, h, i: (b, h)),
                  pl.BlockSpec((seq, kvw), lambda b, h, i: (s0 + b, h)),
                  pl.BlockSpec((past, LANES), lambda b, h, i: (b, 0)),
                  pl.BlockSpec((seq, LANES), lambda b, h, i: (s0 + b, 0))],
        out_specs=pl.BlockSpec((tq, MLA_V), lambda b, h, i: (b * nq + i, h)),
        out_shape=jax.ShapeDtypeStruct((n_seq * seq, MLA_HEADS * MLA_V), BF16),
        scratch_shapes=[pltpu.VMEM((past + seq, MLA_NOPE + LANES), BF16),
                        pltpu.VMEM((past + seq, V_EXT_W), BF16)],
        compiler_params=_cparams(("parallel", "parallel", "arbitrary"), 40),
        name="attention_lat",
    )(q_all, kv_ctx, kv_all, kpe_ctx, kpe_all)


def _merge_kernel(rc_ref, rl_ref, ac_ref, al_ref, wr_ref, wm_ref, gr_ref, gm_ref, o_ref, *, n_ctx_tok):
    tile = pl.program_id(1)
    tm = o_ref.shape[0]
    r = _path_pick(tile, tm, n_ctx_tok, rc_ref, rl_ref)
    a = _path_pick(tile, tm, n_ctx_tok, ac_ref, al_ref)
    ro = jnp.dot(r, wr_ref[...], preferred_element_type=F32)
    mo = jnp.dot(a, wm_ref[...], preferred_element_type=F32)
    o_ref[...] = (gr_ref[...].astype(F32) * ro + gm_ref[...].astype(F32) * mo).astype(BF16)


def _merge(r_ctx, r_lat, a_ctx, a_lat, w_ret_o, w_mla_o, gates, n_ctx_tok):
    t = gates.shape[0]
    tm, tn = 512, 1024
    nj = D_MODEL // tn
    return pl.pallas_call(
        functools.partial(_merge_kernel, n_ctx_tok=n_ctx_tok),
        grid=(nj, t // tm),
        in_specs=[*_path_specs(tm, RET_V_W, n_ctx_tok, axis=1),
                  *_path_specs(tm, MLA_HEADS * MLA_V, n_ctx_tok, axis=1),
                  pl.BlockSpec((RET_V_W, tn), lambda j, i: (0, j)),
                  pl.BlockSpec((MLA_HEADS * MLA_V, tn), lambda j, i: (0, j)),
                  pl.BlockSpec((tm, tn), lambda j, i: (i, j)),
                  pl.BlockSpec((tm, tn), lambda j, i: (i, nj + j))],
        out_specs=pl.BlockSpec((tm, tn), lambda j, i: (i, j)),
        out_shape=jax.ShapeDtypeStruct((t, D_MODEL), BF16),
        compiler_params=_cparams(("parallel", "parallel"), 48),
        name="branch_merge",
    )(r_ctx, r_lat, a_ctx, a_lat, w_ret_o, w_mla_o, gates, gates)


def _outproj_kernel(m_ref, w_ref, xc_ref, xl_ref, g1_ref, nw_ref, sc_ref, sh_ref,
                    x1_ref, hhi_ref, hlo_ref, hp_ref, *, n_ctx_tok):
    mix = jnp.dot(m_ref[...], w_ref[...], preferred_element_type=F32)
    x = _path_pick(pl.program_id(0), x1_ref.shape[0], n_ctx_tok, xc_ref, xl_ref)
    x1 = x + g1_ref[...] * mix
    x1_ref[...] = x1
    h2 = _rms(x1, nw_ref[...]) * (1.0 + sc_ref[...]) + sh_ref[...]
    hi = h2.astype(BF16)
    hif = hi.astype(F32)
    hhi_ref[...] = hi
    hlo_ref[...] = (h2 - hif).astype(BF16)
    _store_row_tiles(hp_ref, hif)


def _out_proj(merged, w_out, x_ctx, x_lat, norm_w, mod4, n_ctx_tok, dec_seq):
    t = merged.shape[0]
    tm = 2 * TM_NORM
    row = lambda i: (i, 0)
    const = lambda i: (0, 0)
    return pl.pallas_call(
        functools.partial(_outproj_kernel, n_ctx_tok=n_ctx_tok),
        grid=(t // tm,),
        in_specs=[pl.BlockSpec((tm, D_MODEL), row),
                  pl.BlockSpec((D_MODEL, D_MODEL), const, pipeline_mode=pl.Buffered(1)),
                  *_path_specs(tm, D_MODEL, n_ctx_tok),
                  _mod_spec(2, tm, n_ctx_tok, dec_seq),
                  pl.BlockSpec((1, D_MODEL), const),
                  _mod_spec(4, tm, n_ctx_tok, dec_seq),
                  _mod_spec(3, tm, n_ctx_tok, dec_seq)],
        out_specs=[pl.BlockSpec((tm, D_MODEL), row),
                   pl.BlockSpec((tm, D_MODEL), row),
                   pl.BlockSpec((tm, D_MODEL), row),
                   pl.BlockSpec((tm * ROW_TILES, LANES), row)],
        out_shape=[jax.ShapeDtypeStruct((t, D_MODEL), F32),
                   jax.ShapeDtypeStruct((t, D_MODEL), BF16),
                   jax.ShapeDtypeStruct((t, D_MODEL), BF16),
                   jax.ShapeDtypeStruct((t * ROW_TILES, LANES), U32)],
        compiler_params=_cparams(("parallel",), 56),
        name="out_proj_norm",
    )(merged, w_out, x_ctx, x_lat, mod4, norm_w, mod4, mod4)


def _router_kernel(hhi_ref, hlo_ref, wrt_ref, bias_ref, eidx_ref, wts_ref, rank_ref, cnt_ref, carry):
    tm = hhi_ref.shape[0]
    n_e = N_EXPERTS

    @pl.when(pl.program_id(0) == 0)
    def _():
        carry[...] = jnp.zeros_like(carry)

    wr = wrt_ref[...]
    wr_hi = wr.astype(BF16)
    wr_lo = (wr - wr_hi.astype(F32)).astype(BF16)
    hh = hhi_ref[...]
    hl = hlo_ref[...]
    nt_dims = (((1,), (1,)), ((), ()))
    logits = (lax.dot_general(wr_hi, hh, nt_dims, preferred_element_type=F32)
              + lax.dot_general(wr_hi, hl, nt_dims, preferred_element_type=F32)
              + lax.dot_general(wr_lo, hh, nt_dims, preferred_element_type=F32))
    scores = _sigmoid(logits)
    choice = scores + bias_ref[...]
    neg = -jnp.inf
    row = lax.broadcasted_iota(I32, (n_e, tm), 0).astype(F32)
    grow = lax.broadcasted_iota(I32, (GROUP_SIZE, tm), 0).astype(F32)

    gs = []
    for g in range(N_GROUPS):
        blk = choice[g * GROUP_SIZE:(g + 1) * GROUP_SIZE]
        m1 = jnp.max(blk, axis=0, keepdims=True)
        i1 = jnp.min(jnp.where(blk == m1, grow, float(GROUP_SIZE)), axis=0, keepdims=True)
        m2 = jnp.max(jnp.where(grow == i1, neg, blk), axis=0, keepdims=True)
        gs.append(m1 + m2)
    gsc = jnp.concatenate(gs, axis=0)
    gid = lax.broadcasted_iota(I32, (N_GROUPS, tm), 0).astype(F32)
    gsel = jnp.zeros((N_GROUPS, tm), F32)
    for _ in range(TOPK_GROUPS):
        m = jnp.max(gsc, axis=0, keepdims=True)
        idx = jnp.min(jnp.where(gsc == m, gid, float(N_GROUPS)), axis=0, keepdims=True)
        hit = gid == idx
        gsel = jnp.where(hit, 1.0, gsel)
        gsc = jnp.where(hit, neg, gsc)
    masked = jnp.concatenate(
        [jnp.where(gsel[g:g + 1] > 0.0, choice[g * GROUP_SIZE:(g + 1) * GROUP_SIZE], neg)
         for g in range(N_GROUPS)], axis=0)

    idxs, ws = [], []
    onehot = jnp.zeros((n_e, tm), F32)
    for _ in range(TOP_K):
        m = jnp.max(masked, axis=0, keepdims=True)
        idx = jnp.min(jnp.where(masked == m, row, float(n_e)), axis=0, keepdims=True)
        hit = row == idx
        ws.append(jnp.sum(jnp.where(hit, scores, 0.0), axis=0, keepdims=True))
        masked = jnp.where(hit, neg, masked)
        onehot = jnp.where(hit, 1.0, onehot)
        idxs.append(idx)
    w = jnp.concatenate(ws, axis=0)
    wts_ref[...] = w / jnp.sum(w, axis=0, keepdims=True) * ROUTED_SCALE
    eidx_ref[...] = jnp.concatenate(idxs, axis=0).astype(I32)

    t_src = lax.broadcasted_iota(I32, (tm, tm), 0)
    t_dst = lax.broadcasted_iota(I32, (tm, tm), 1)
    before = jnp.where(t_src < t_dst, 1.0, 0.0).astype(BF16)
    prior = jnp.dot(onehot.astype(BF16), before, preferred_element_type=F32) + carry[:, :1]
    rank_ref[...] = jnp.concatenate(
        [jnp.sum(jnp.where(row == idx, prior, 0.0), axis=0, keepdims=True) for idx in idxs],
        axis=0).astype(I32)
    carry[...] = carry[...] + jnp.sum(onehot, axis=1, keepdims=True)
    cnt_ref[...] = carry[...]


def _router(h_hi, h_lo, wr_t, bias_col):
    t = h_hi.shape[0]
    tm = 512
    tok = lambda i: (0, i)
    const = lambda i: (0, 0)
    return pl.pallas_call(
        _router_kernel,
        grid=(t // tm,),
        in_specs=[pl.BlockSpec((tm, D_MODEL), lambda i: (i, 0)),
                  pl.BlockSpec((tm, D_MODEL), lambda i: (i, 0)),
                  pl.BlockSpec((N_EXPERTS, D_MODEL), const),
                  pl.BlockSpec((N_EXPERTS, 1), const)],
        out_specs=[pl.BlockSpec((TOP_K, tm), tok),
                   pl.BlockSpec((TOP_K, tm), tok),
                   pl.BlockSpec((TOP_K, tm), tok),
                   pl.BlockSpec((N_EXPERTS, LANES), const)],
        out_shape=[jax.ShapeDtypeStruct((TOP_K, t), I32),
                   jax.ShapeDtypeStruct((TOP_K, t), F32),
                   jax.ShapeDtypeStruct((TOP_K, t), I32),
                   jax.ShapeDtypeStruct((N_EXPERTS, LANES), F32)],
        scratch_shapes=[pltpu.VMEM((N_EXPERTS, LANES), F32)],
        compiler_params=_cparams(("arbitrary",), 40),
        name="router_topk",
    )(h_hi, h_lo, wr_t, bias_col)


def _dest_kernel(eidx_ref, rank_ref, pstart_ref, dest_ref):
    tm = eidx_ref.shape[1]
    row = lax.broadcasted_iota(I32, (N_EXPERTS, tm), 0)
    pstart = pstart_ref[...]
    eidx = eidx_ref[...]
    base = jnp.concatenate(
        [jnp.sum(jnp.where(row == eidx[k:k + 1], pstart, 0.0), axis=0, keepdims=True)
         for k in range(TOP_K)], axis=0)
    dest_ref[...] = (base.astype(I32) + rank_ref[...]) * ROW_TILES


def _dest_rows(eidx, rank, pstart_col):
    t = eidx.shape[1]
    tm = 1024
    tok = lambda i: (0, i)
    return pl.pallas_call(
        _dest_kernel,
        grid=(t // tm,),
        in_specs=[pl.BlockSpec((TOP_K, tm), tok),
                  pl.BlockSpec((TOP_K, tm), tok),
                  pl.BlockSpec((N_EXPERTS, 1), lambda i: (0, 0))],
        out_specs=pl.BlockSpec((TOP_K, tm), tok),
        out_shape=jax.ShapeDtypeStruct((TOP_K, t), I32),
        compiler_params=_cparams(("parallel",)),
        name="dest_rows",
    )(eidx, rank, pstart_col)


def _dispatch_kernel(fill_ref, meta_ref, dest_ref, hp_ref, xs_ref, zbuf, sem, fill_sem):
    tm = hp_ref.shape[0]
    block_rows = zbuf.shape[0]
    n_blocks = xs_ref.shape[0] // block_rows

    def zero_fills(start):
        def run(cp):
            cp.start() if start else cp.wait()

        def per_expert(e, c):
            row = fill_ref[0, e]
            left = fill_ref[1, e]
            p = EXPERT_BM // 2
            while p >= 1:
                take = left & p

                @pl.when(take != 0)
                def _(row=row, p=p):
                    dst = xs_ref.at[pl.ds(pl.multiple_of(row * ROW_TILES, ROW_TILES), p * ROW_TILES)]
                    run(pltpu.make_async_copy(zbuf.at[pl.ds(0, p * ROW_TILES)], dst, fill_sem))

                row = row + take
                p //= 2
            return c

        def per_tail_block(b, c):
            dst = xs_ref.at[pl.ds(pl.multiple_of(b * block_rows, block_rows), block_rows)]
            run(pltpu.make_async_copy(zbuf, dst, fill_sem))
            return c

        lax.fori_loop(0, N_EXPERTS, per_expert, 0)
        lax.fori_loop(meta_ref[0], n_blocks, per_tail_block, 0)

    @pl.when(pl.program_id(0) == 0)
    def _():
        zbuf[...] = jnp.zeros_like(zbuf)
        zero_fills(True)

    def issue(t, c):
        src = hp_ref.at[pl.ds(pl.multiple_of(t * ROW_TILES, ROW_TILES), ROW_TILES)]
        for k in range(TOP_K):
            r = pl.multiple_of(dest_ref[t * TOP_K + k], ROW_TILES)
            pltpu.make_async_copy(src, xs_ref.at[pl.ds(r, ROW_TILES)], sem).start(priority=k % 2)
        return c

    lax.fori_loop(0, tm // ROW_TILES, issue, 0)
    for _ in range(TOP_K):
        pltpu.make_async_copy(hp_ref, xs_ref.at[pl.ds(0, tm)], sem).wait()

    @pl.when(pl.program_id(0) == 0)
    def _():
        zero_fills(False)


def _dispatch(fill, meta, dest, hp, n_blocks):
    tm = TM_NORM * ROW_TILES
    block_rows = EXPERT_BM * ROW_TILES
    return pl.pallas_call(
        _dispatch_kernel,
        grid=(hp.shape[0] // tm,),
        in_specs=[pl.BlockSpec(memory_space=pltpu.SMEM),
                  pl.BlockSpec(memory_space=pltpu.SMEM),
                  pl.BlockSpec((TM_NORM * TOP_K,), lambda i: (i,), memory_space=pltpu.SMEM),
                  pl.BlockSpec((tm, LANES), lambda i: (i, 0))],
        out_specs=pl.BlockSpec(memory_space=pl.ANY),
        out_shape=jax.ShapeDtypeStruct((n_blocks * block_rows, LANES), U32),
        scratch_shapes=[pltpu.VMEM((block_rows, LANES), U32),
                        pltpu.SemaphoreType.DMA(()),
                        pltpu.SemaphoreType.DMA(())],
        compiler_params=_cparams(("arbitrary",)),
        name="moe_dispatch",
    )(fill, meta, dest, hp)


N_WSLOTS = 2


def _expert_kernel(uidx_ref, ue_ref, meta_ref, xs_ref, wi_hbm, wo_hbm, ys_ref,
                   wi_st, wo_st, wi_s, wo_s, sem):
    b = pl.program_id(0)
    n_used = meta_ref[0]
    n_ue = meta_ref[1]

    def weight_copies(j, slot):
        e = ue_ref[j]
        copies = []
        for m, (hbm, st) in enumerate(((wi_hbm, wi_st), (wo_hbm, wo_st))):
            half = hbm.shape[1] // 2
            for q in range(2):
                rows = pl.ds(q * half, half)
                copies.append((pltpu.make_async_copy(hbm.at[e, rows], st.at[slot, rows], sem.at[2 * m + q, slot]), q))
        return copies

    @pl.when(b < n_used)
    def _():
        j = uidx_ref[b]

        @pl.when(b == 0)
        def _():
            for cp, prio in weight_copies(0, 0):
                cp.start(priority=prio)

            @pl.when(n_ue > 1)
            def _():
                for cp, prio in weight_copies(1, 1):
                    cp.start(priority=prio)

        @pl.when((b == 0) | (j != uidx_ref[jnp.maximum(b - 1, 0)]))
        def _():
            slot = j % N_WSLOTS
            for cp, _ in weight_copies(j, slot):
                cp.wait()
            rows = 256

            def cast_in(i, c):
                r = pl.multiple_of(i * rows, rows)
                wi_s[pl.ds(r, rows), :] = wi_st[slot, pl.ds(r, rows), :].astype(BF16)
                return c

            def cast_out(i, c):
                r = pl.multiple_of(i * rows, rows)
                wo_s[pl.ds(r, rows), :] = wo_st[slot, pl.ds(r, rows), :].astype(BF16)
                return c

            lax.fori_loop(0, D_MODEL // rows, cast_in, 0)
            lax.fori_loop(0, D_EXPERT // rows, cast_out, 0)

            @pl.when(j + N_WSLOTS < n_ue)
            def _():
                for cp, prio in weight_copies(j + N_WSLOTS, slot):
                    cp.start(priority=prio)

        bm = EXPERT_BM
        x = jnp.concatenate([p.astype(BF16) for p in _load_row_tiles(xs_ref, bm)], axis=1)
        gu = jnp.dot(x, wi_s[...], preferred_element_type=F32)
        act = (_silu(gu[:, :D_EXPERT]) * gu[:, D_EXPERT:]).astype(BF16)
        _store_row_tiles(ys_ref, jnp.dot(act, wo_s[...], preferred_element_type=F32))


def _experts(uidx, ue, meta, xs, w_exp_in, w_exp_out):
    rows = EXPERT_BM * ROW_TILES
    nb = xs.shape[0] // rows
    live = lambda b, uidx, ue, meta: (jnp.minimum(b, meta[0] - 1), 0)
    grid_spec = pltpu.PrefetchScalarGridSpec(
        num_scalar_prefetch=3,
        grid=(nb,),
        in_specs=[pl.BlockSpec((rows, LANES), live),
                  pl.BlockSpec(memory_space=pl.ANY),
                  pl.BlockSpec(memory_space=pl.ANY)],
        out_specs=pl.BlockSpec((rows, LANES), live),
        scratch_shapes=[pltpu.VMEM((N_WSLOTS, D_MODEL, 2 * D_EXPERT), F32),
                        pltpu.VMEM((N_WSLOTS, D_EXPERT, D_MODEL), F32),
                        pltpu.VMEM((D_MODEL, 2 * D_EXPERT), BF16),
                        pltpu.VMEM((D_EXPERT, D_MODEL), BF16),
                        pltpu.SemaphoreType.DMA((4, N_WSLOTS))],
    )
    return pl.pallas_call(
        _expert_kernel,
        grid_spec=grid_spec,
        out_shape=jax.ShapeDtypeStruct(xs.shape, U32),
        input_output_aliases={3: 0},
        compiler_params=_cparams(("arbitrary",), 52),
        name="routed_experts",
    )(uidx, ue, meta, xs, w_exp_in, w_exp_out)


def _combine_kernel(dest_ref, dnext_ref, wt_ref, x1_ref, hhi_ref, g2_ref, wsi_ref, wso_ref, nf_ref, ys_ref,
                    yc_ref, yl_ref, rows_a, rows_b, moe_scr, sem, *, n_ctx_tok):
    tm = x1_ref.shape[0]
    i = pl.program_id(0)
    n = pl.num_programs(0)

    def wait_tile(buf, s):
        for k in range(TOP_K):
            pltpu.make_async_copy(ys_ref.at[pl.ds(0, tm * ROW_TILES)], buf.at[k], sem.at[s]).wait()

    @pl.when(i == 0)
    def _():
        def issue(t, c):
            for k in range(TOP_K):
                r = pl.multiple_of(dest_ref[t * TOP_K + k], ROW_TILES)
                dst = rows_a.at[k, pl.ds(pl.multiple_of(t * ROW_TILES, ROW_TILES), ROW_TILES)]
                pltpu.make_async_copy(ys_ref.at[pl.ds(r, ROW_TILES)], dst, sem.at[0]).start(priority=k % 2)
            return c

        lax.fori_loop(0, tm, issue, 0)

    def step(cur, cur_sem, nxt, nxt_sem):
        wait_tile(cur, cur_sem)

        for t in range(tm):
            for k in range(TOP_K):
                r = pl.multiple_of(dnext_ref[t * TOP_K + k], ROW_TILES)
                dst = nxt.at[k, pl.ds(t * ROW_TILES, ROW_TILES)]
                pltpu.make_async_copy(ys_ref.at[pl.ds(r, ROW_TILES)], dst, sem.at[nxt_sem]).start(priority=k % 2)

        su = jnp.dot(hhi_ref[...], wsi_ref[...], preferred_element_type=F32)
        sact = (_silu(su[:, :D_SHARED]) * su[:, D_SHARED:]).astype(BF16)
        shared = jnp.dot(sact, wso_ref[...], preferred_element_type=F32)

        sub = 128
        for r0 in range(0, tm, sub):
            wks = [wt_ref[r0:r0 + sub, k:k + 1] for k in range(TOP_K)]
            for s in range(ROW_TILES):
                acc_lo = acc_hi = None
                for k in range(TOP_K):
                    lo, hi = _unpack_bf16_pair(cur[k, pl.ds(r0 * ROW_TILES + s, sub, SUBLANES), :])
                    acc_lo = wks[k] * lo if acc_lo is None else acc_lo + wks[k] * lo
                    acc_hi = wks[k] * hi if acc_hi is None else acc_hi + wks[k] * hi
                col = 2 * LANES * s
                moe_scr[r0:r0 + sub, col:col + LANES] = acc_lo
                moe_scr[r0:r0 + sub, col + LANES:col + 2 * LANES] = acc_hi
        moe = moe_scr[...] + shared
        x2 = x1_ref[...] + g2_ref[...] * moe
        y = _rms(x2, nf_ref[...])
        is_ctx = i * tm < n_ctx_tok

        @pl.when(is_ctx)
        def _():
            yc_ref[...] = y

        @pl.when(jnp.logical_not(is_ctx))
        def _():
            yl_ref[...] = y

        @pl.when(i == n - 1)
        def _():
            wait_tile(nxt, nxt_sem)

    @pl.when(i % 2 == 0)
    def _():
        step(rows_a, 0, rows_b, 1)

    @pl.when(i % 2 == 1)
    def _():
        step(rows_b, 1, rows_a, 0)


def _combine(dest, wts_t, x1, h_hi, mod4, w_sh_in, w_sh_out, norm_final, ys, n_ctx_tok, dec_seq):
    t = x1.shape[0]
    tm = TM_NORM
    n = t // tm
    row = lambda i: (i, 0)
    const = lambda i: (0, 0)
    return pl.pallas_call(
        functools.partial(_combine_kernel, n_ctx_tok=n_ctx_tok),
        grid=(n,),
        in_specs=[pl.BlockSpec((tm * TOP_K,), lambda i: (i,), memory_space=pltpu.SMEM),
                  pl.BlockSpec((tm * TOP_K,), lambda i: (jnp.minimum(i + 1, n - 1),), memory_space=pltpu.SMEM),
                  pl.BlockSpec((tm, TOP_K), row),
                  pl.BlockSpec((tm, D_MODEL), row),
                  pl.BlockSpec((tm, D_MODEL), row),
                  _mod_spec(5, tm, n_ctx_tok, dec_seq),
                  pl.BlockSpec((D_MODEL, 2 * D_SHARED), const),
                  pl.BlockSpec((D_SHARED, D_MODEL), const),
                  pl.BlockSpec((1, D_MODEL), const),
                  pl.BlockSpec(memory_space=pl.ANY)],
        out_specs=list(_path_specs(tm, D_MODEL, n_ctx_tok)),
        out_shape=[jax.ShapeDtypeStruct((n_ctx_tok, D_MODEL), F32),
                   jax.ShapeDtypeStruct((t - n_ctx_tok, D_MODEL), F32)],
        scratch_shapes=[pltpu.VMEM((TOP_K, tm * ROW_TILES, LANES), U32),
                        pltpu.VMEM((TOP_K, tm * ROW_TILES, LANES), U32),
                        pltpu.VMEM((tm, D_MODEL), F32),
                        pltpu.SemaphoreType.DMA((2,))],
        compiler_params=_cparams(("arbitrary",), 56),
        name="moe_combine",
    )(dest, dest, wts_t, x1, h_hi, mod4, w_sh_in, w_sh_out, norm_final, ys)


def _rope_tables(tm, dec_seq):
    rows = dec_seq // GRID_W
    row = jnp.broadcast_to(jnp.arange(rows, dtype=F32)[:, None], (rows, GRID_W)).reshape(-1)
    col = jnp.broadcast_to(jnp.arange(GRID_W, dtype=F32)[None, :], (rows, GRID_W)).reshape(-1)
    axis_dim = MLA_ROPE // 2
    freqs = jnp.power(ROPE_BASE, -jnp.arange(0, axis_dim, 2, dtype=F32) / axis_dim)
    ang_r = row[:, None] * freqs
    ang_c = col[:, None] * freqs
    ang = jnp.concatenate([ang_r, ang_r, ang_c, ang_c], axis=-1)
    pad = LANES - MLA_ROPE
    cos = jnp.concatenate([jnp.cos(ang), jnp.ones((dec_seq, pad), F32)], axis=1)
    sin = jnp.concatenate([jnp.sin(ang), jnp.zeros((dec_seq, pad), F32)], axis=1)
    cos = jnp.concatenate([jnp.ones((tm, LANES), F32), cos], axis=0)
    sin = jnp.concatenate([jnp.zeros((tm, LANES), F32), sin], axis=0)
    return cos, sin


def _block_plan(counts, n_blocks):
    bm = EXPERT_BM
    nblk = (counts + bm - 1) // bm
    bend = jnp.cumsum(nblk)
    bstart = bend - nblk
    present = nblk > 0
    pcum = jnp.cumsum(present.astype(I32))
    bidx = jnp.arange(n_blocks, dtype=I32)
    uidx = jnp.sum((present[None, :] & (bstart[None, :] <= bidx[:, None])).astype(I32), axis=1) - 1
    order = jnp.arange(N_EXPERTS, dtype=I32)
    ue = jnp.minimum(jnp.sum((pcum[None, :] <= order[:, None]).astype(I32), axis=1), N_EXPERTS - 1)
    meta = jnp.stack([bend[-1], pcum[-1]]).astype(I32)
    pstart = bstart * bm
    fill = jnp.stack([pstart + counts, nblk * bm - counts]).astype(I32)
    return pstart, uidx.astype(I32), ue.astype(I32), meta, fill


def kernel(x_prompt, x_sample, c, cache_mla_ckv, cache_mla_krope, state_ret, c_ctx, w_mod, b_mod, norm_mix, norm_ffn, w_in, ret_decay_logit, ret_gn_w, w_ret_o, mla_q_norm, w_q_b, mla_kv_norm, w_kv_b, w_mla_o, w_out, w_router, router_bias, w_exp_in, w_exp_out, w_shared_in, w_shared_out, norm_final):
    n_ctx, seq, d = x_prompt.shape
    n_lat, dec_seq, _ = x_sample.shape
    past = cache_mla_ckv.shape[2]
    n_ctx_tok = n_ctx * seq
    n_lat_tok = n_lat * dec_seq
    t = n_ctx_tok + n_lat_tok
    assert d == D_MODEL and w_mod.shape[0] == 1 and n_lat + 1 <= MOD_ROWS
    assert seq % TM_NORM == 0 and dec_seq % TM_NORM == 0 and n_ctx_tok % dec_seq == 0

    x_ctx = x_prompt.reshape(n_ctx_tok, d)
    x_lat = x_sample.reshape(n_lat_tok, d)

    cond = jnp.concatenate([c_ctx[None, :], c, jnp.zeros((MOD_ROWS - 1 - n_lat, d), F32)], axis=0)
    mod = _modulation(cond, w_mod.reshape(d, N_MOD * d), b_mod.reshape(1, N_MOD * d))
    mod4 = mod.reshape(MOD_ROWS, N_MOD, 1, d)

    w_in2 = w_in.reshape(d, -1)
    o_g = 2 * RET_QK_W + RET_V_W
    o_qa = o_g + RET_V_W
    o_kpe = o_qa + MLA_Q_RANK + MLA_KV_RANK
    o_gr = o_kpe + MLA_ROPE
    w_qkvg = w_in2[:, :o_qa].astype(BF16)
    w_low = jnp.pad(w_in2[:, o_qa:o_gr], ((0, 0), (0, LANES - MLA_ROPE))).astype(BF16)
    w_gates = w_in2[:, o_gr:].astype(BF16)
    wq3 = w_q_b.reshape(MLA_Q_RANK, MLA_HEADS, MLA_NOPE + MLA_ROPE)
    w_qb = jnp.pad(wq3, ((0, 0), (0, 0), (0, Q_HEAD_W - MLA_NOPE - MLA_ROPE))
                   ).reshape(MLA_Q_RANK, MLA_HEADS * Q_HEAD_W).astype(BF16)
    w_kvb = w_kv_b.reshape(MLA_KV_RANK, -1).astype(BF16)
    cos_t, sin_t = _rope_tables(TM_NORM, dec_seq)

    h1 = _norm_modulate(x_ctx, x_lat, norm_mix.reshape(1, d), mod4, n_ctx_tok, dec_seq)
    qkvg = _matmul(h1, w_qkvg, tm=2048, tn=1024, name="proj_qkvg")
    gates = _matmul(h1, w_gates, tm=2048, tn=1024, act="sigmoid", name="proj_gates")
    qn, ckv, ckv_b, kpe_raw, kpe_rot = _lowrank(
        h1, w_low, mla_q_norm.reshape(1, -1), mla_kv_norm.reshape(1, -1), cos_t, sin_t, n_ctx_tok, dec_seq)

    logits = ret_decay_logit.reshape(2, RET_HEADS)
    gn_w = ret_gn_w.reshape(1, RET_V_W)
    ctx_group = dec_seq // seq
    r_ctx, ret_state = _retention(qkvg, logits, gn_w, None, n_seq=n_ctx, seq=seq, group=ctx_group,
                                  row_block0=0, emit_state=True)
    (r_lat,) = _retention(qkvg, logits, gn_w, state_ret.reshape(n_lat, 2, RET_HEADS, RET_DK, RET_DV),
                          n_seq=n_lat, seq=dec_seq, group=1, row_block0=n_ctx_tok // dec_seq,
                          emit_state=False)

    q_all = _q_up(qn, w_qb, cos_t, sin_t, n_ctx_tok, dec_seq)
    kv_all = _matmul(ckv_b, w_kvb, tm=1024, tn=1024, name="kv_up")
    kv_ctx = _matmul(cache_mla_ckv.reshape(n_lat * past, MLA_KV_RANK), w_kvb, tm=1024, tn=1024,
                     name="kv_up_cache")
    kpe_ctx = jnp.pad(cache_mla_krope.reshape(n_lat * past, MLA_ROPE),
                      ((0, 0), (0, LANES - MLA_ROPE))).astype(BF16)
    a_ctx = _attention_ctx(q_all, kv_all, kpe_rot, n_ctx, seq)
    a_lat = _attention_lat(q_all, kv_ctx, kv_all, kpe_ctx, kpe_rot, n_lat, dec_seq, past, n_ctx_tok)

    merged = _merge(r_ctx, r_lat, a_ctx, a_lat, w_ret_o.reshape(RET_V_W, d).astype(BF16),
                    w_mla_o.reshape(MLA_HEADS * MLA_V, d).astype(BF16), gates, n_ctx_tok)
    x1, h_hi, h_lo, hp = _out_proj(merged, w_out.reshape(d, d).astype(BF16), x_ctx, x_lat,
                                   norm_ffn.reshape(1, d), mod4, n_ctx_tok, dec_seq)

    eidx, wts, rank, cnt = _router(h_hi, h_lo, w_router.reshape(d, N_EXPERTS).T,
                                   router_bias.reshape(N_EXPERTS, 1))
    n_blocks = t * TOP_K // EXPERT_BM + N_EXPERTS
    pstart, uidx, ue, meta, fill = _block_plan(cnt[:, 0].astype(I32), n_blocks)
    dest = _dest_rows(eidx, rank, pstart.astype(F32).reshape(N_EXPERTS, 1)).T.reshape(-1)
    xs = _dispatch(fill, meta, dest, hp, n_blocks)
    ys = _experts(uidx, ue, meta, xs, w_exp_in.reshape(N_EXPERTS, d, 2 * D_EXPERT),
                  w_exp_out.reshape(N_EXPERTS, D_EXPERT, d))
    y_ctx, y_lat = _combine(dest, wts.T, x1, h_hi, mod4, w_shared_in.reshape(d, 2 * D_SHARED).astype(BF16),
                            w_shared_out.reshape(D_SHARED, d).astype(BF16), norm_final.reshape(1, d), ys,
                            n_ctx_tok, dec_seq)

    y_prompt = y_ctx.reshape(n_ctx, seq, d)
    y_sample = y_lat.reshape(n_lat, dec_seq, d)
    new_ckv = ckv[:n_ctx_tok].reshape(n_ctx, 1, seq, MLA_KV_RANK)
    new_krope = kpe_raw[:n_ctx_tok, :MLA_ROPE].reshape(n_ctx, 1, seq, MLA_ROPE)
    new_state = ret_state.reshape(n_ctx, 1, 2, RET_HEADS, RET_DK, RET_DV)
    return (y_prompt, y_sample, new_ckv, new_krope, new_state)
```
